```python
import math
import jax, jax.numpy as jnp
from jax import lax
import numpy as np

D_MODEL = 2048
BATCH = 4
SEQ = 2048
DEPTH = 4

N_MIXERS = 3
EPS = 1e-6
Q_BLOCK = 128

MLSTM_HEADS = 8
MLSTM_DV = D_MODEL // MLSTM_HEADS
MLSTM_DQK = MLSTM_DV // 2
MLSTM_CHUNK = 128
MLSTM_CONV = 3
MLSTM_QK_CH = 2 * MLSTM_HEADS * MLSTM_DQK
MLSTM_COLS = MLSTM_QK_CH + 2 * MLSTM_HEADS * MLSTM_DV + 4 * MLSTM_HEADS

MLA_HEADS = D_MODEL // 128
MLA_NOPE = 128
MLA_ROPE = 64
MLA_DV = 128
MLA_KV_RANK = 512
ROPE_BASE = 10000.0
MLA_COLS = MLA_HEADS * (MLA_NOPE + MLA_ROPE) + MLA_KV_RANK + MLA_ROPE

DIL_GROUPS = ((128, 1), (512, 4), (2048, 16))
DIL_HEADS = D_MODEL // 256
DIL_DH = 128
DIL_COLS = len(DIL_GROUPS) * 3 * DIL_HEADS * DIL_DH

N_EXPERTS = 16
N_GROUPS = 4
EXPERTS_PER_GROUP = N_EXPERTS // N_GROUPS
TOP_K = 2
D_EXPERT = D_MODEL // 4

kernel_name = "hybrid_mlstm_mla_dilated_moe_encoder"


def rmsnorm(x, g):
    xf = x.astype(jnp.float32)
    y = xf * lax.rsqrt(jnp.mean(xf * xf, axis=-1, keepdims=True) + EPS)
    return (y * g.astype(jnp.float32)).astype(x.dtype)


def centred_depthwise_conv(x, w):
    c = x.shape[-1]
    return lax.conv_general_dilated(x, w[:, None, :].astype(x.dtype), window_strides=(1,), padding='SAME',
                                    dimension_numbers=('NWC', 'WIO', 'NWC'), feature_group_count=c)


def rope(x, pos):
    half = x.shape[-1] // 2
    inv = ROPE_BASE ** (-jnp.arange(half, dtype=jnp.float32) / half)
    ang = pos.astype(jnp.float32)[:, None] * inv[None, :]
    cos = jnp.cos(ang)[None, :, None, :]
    sin = jnp.sin(ang)[None, :, None, :]
    x1, x2 = x[..., :half], x[..., half:]
    return jnp.concatenate([x1 * cos - x2 * sin, x2 * cos + x1 * sin], axis=-1).astype(x.dtype)


def alibi_slopes(n):
    return jnp.exp2(-8.0 * jnp.arange(1, n + 1, dtype=jnp.float32) / n)


def mlstm_chunkwise(q, k, v, li, lf):
    bsz, nh, s, dk = q.shape
    dv = v.shape[-1]
    L = MLSTM_CHUNK
    nc = s // L
    f32 = jnp.float32
    qc = q.astype(f32).reshape(bsz, nh, nc, L, dk)
    kc = k.astype(f32).reshape(bsz, nh, nc, L, dk)
    vc = v.astype(f32).reshape(bsz, nh, nc, L, dv)
    lic = li.reshape(bsz, nh, nc, L)
    b = jnp.cumsum(lf.reshape(bsz, nh, nc, L), axis=-1)
    g = b[..., -1]
    lower = jnp.tril(jnp.ones((L, L), dtype=bool))
    dlog = jnp.where(lower, b[..., :, None] - b[..., None, :] + lic[..., None, :], -jnp.inf)
    m_intra = jnp.max(dlog, axis=-1)
    w = g[..., None] - b + lic
    m_loc = jnp.max(w, axis=-1)
    ew = jnp.exp(w - m_loc[..., None])
    c_loc = jnp.einsum('bhcl,bhcld,bhcle->bhcde', ew, kc, vc)
    n_loc = jnp.einsum('bhcl,bhcld->bhcd', ew, kc)

    def step(carry, xs):
        c_st, n_st, m_st = carry
        g_c, ml_c, cl_c, nl_c = xs
        m_new = jnp.maximum(g_c + m_st, ml_c)
        a = jnp.exp(g_c + m_st - m_new)
        e = jnp.exp(ml_c - m_new)
        c_new = a[..., None, None] * c_st + e[..., None, None] * cl_c
        n_new = a[..., None] * n_st + e[..., None] * nl_c
        return (c_new, n_new, m_new), (c_st, n_st, m_st)

    init = (jnp.zeros((bsz, nh, dk, dv), f32), jnp.zeros((bsz, nh, dk), f32), jnp.zeros((bsz, nh), f32))
    xs = (jnp.moveaxis(g, 2, 0), jnp.moveaxis(m_loc, 2, 0), jnp.moveaxis(c_loc, 2, 0), jnp.moveaxis(n_loc, 2, 0))
    _, (c_prev, n_prev, m_prev) = lax.scan(step, init, xs)
    c_prev = jnp.moveaxis(c_prev, 0, 2)
    n_prev = jnp.moveaxis(n_prev, 0, 2)
    m_prev = jnp.moveaxis(m_prev, 0, 2)

    a_inter = b + m_prev[..., None]
    m = jnp.maximum(a_inter, m_intra)
    e_inter = jnp.exp(a_inter - m)
    dmat = jnp.exp(dlog - m[..., None]) * jnp.einsum('bhcld,bhcsd->bhcls', qc, kc)
    num = e_inter[..., None] * jnp.einsum('bhcld,bhcde->bhcle', qc, c_prev) + jnp.einsum('bhcls,bhcse->bhcle', dmat, vc)
    den = e_inter * jnp.einsum('bhcld,bhcd->bhcl', qc, n_prev) + jnp.sum(dmat, axis=-1)
    h = num / jnp.maximum(jnp.abs(den), jnp.exp(-m))[..., None]
    return h.reshape(bsz, nh, s, dv)


def mlstm_mixer(xn, w_in, conv_w, gate_b, out_norm, w_out):
    bsz, s, _ = xn.shape
    H = MLSTM_HEADS
    p = xn @ w_in
    qk = jax.nn.silu(centred_depthwise_conv(p[..., :MLSTM_QK_CH], conv_w))
    off = MLSTM_QK_CH
    v = p[..., off:off + H * MLSTM_DV]
    o = p[..., off + H * MLSTM_DV:off + 2 * H * MLSTM_DV]
    gates = p[..., off + 2 * H * MLSTM_DV:].astype(jnp.float32) + gate_b.astype(jnp.float32)
    gates = gates.reshape(bsz, s, 4, H).transpose(2, 0, 3, 1)
    q = qk[..., :H * MLSTM_DQK].reshape(bsz, s, H, MLSTM_DQK).transpose(0, 2, 1, 3) * (MLSTM_DQK ** -0.5)
    k = qk[..., H * MLSTM_DQK:].reshape(bsz, s, H, MLSTM_DQK).transpose(0, 2, 1, 3)
    v = v.reshape(bsz, s, H, MLSTM_DV).transpose(0, 2, 1, 3)
    h_f = mlstm_chunkwise(q, k, v, gates[0], jax.nn.log_sigmoid(gates[1]))
    flip = lambda t: jnp.flip(t, axis=2)
    h_b = flip(mlstm_chunkwise(flip(q), flip(k), flip(v), jnp.flip(gates[2], -1),
                               jnp.flip(jax.nn.log_sigmoid(gates[3]), -1)))
    h = (h_f + h_b).astype(xn.dtype).transpose(0, 2, 1, 3)
    h = rmsnorm(h, out_norm.reshape(H, MLSTM_DV)).reshape(bsz, s, H * MLSTM_DV)
    return (jax.nn.sigmoid(o) * h) @ w_out


def dense_block_attention(q, k, v, scale):
    bsz, s, nh, dq = q.shape
    dv = v.shape[-1]
    nb = s // Q_BLOCK
    qb = q.reshape(bsz, nb, Q_BLOCK, nh, dq).swapaxes(0, 1)

    def one(qblk):
        sc = jnp.einsum('bqhd,bkhd->bhqk', qblk, k).astype(jnp.float32) * scale
        pr = jax.nn.softmax(sc, axis=-1).astype(v.dtype)
        return jnp.einsum('bhqk,bkhd->bqhd', pr, v)

    o = lax.map(one, qb)
    return o.swapaxes(0, 1).reshape(bsz, s, nh, dv)


def mla_mixer(xn, w_in, kv_norm, w_ukv, q_norm, k_norm, w_out):
    bsz, s, _ = xn.shape
    H = MLA_HEADS
    dqk = MLA_NOPE + MLA_ROPE
    p = xn @ w_in
    q = p[..., :H * dqk].reshape(bsz, s, H, dqk)
    ckv = p[..., H * dqk:H * dqk + MLA_KV_RANK]
    k_rope = p[..., H * dqk + MLA_KV_RANK:]
    kv = (rmsnorm(ckv, kv_norm) @ w_ukv).reshape(bsz, s, H, MLA_NOPE + MLA_DV)
    k_nope, v = kv[..., :MLA_NOPE], kv[..., MLA_NOPE:]
    k = jnp.concatenate([k_nope, jnp.broadcast_to(k_rope[:, :, None, :], (bsz, s, H, MLA_ROPE))], axis=-1)
    q = rmsnorm(q, q_norm)
    k = rmsnorm(k, k_norm)
    pos = jnp.arange(s)
    q = jnp.concatenate([q[..., :MLA_NOPE], rope(q[..., MLA_NOPE:], pos)], axis=-1)
    k = jnp.concatenate([k[..., :MLA_NOPE], rope(k[..., MLA_NOPE:], pos)], axis=-1)
    o = dense_block_attention(q, k, v, dqk ** -0.5)
    return o.reshape(bsz, s, H * MLA_DV) @ w_out


def dilated_mixer(xn, w_in, q_norm, k_norm, w_out):
    bsz, s, _ = xn.shape
    G, H, dh = len(DIL_GROUPS), DIL_HEADS, DIL_DH
    p = (xn @ w_in).reshape(bsz, s, G, 3, H, dh)
    q = rmsnorm(p[:, :, :, 0], q_norm[:, None, :])
    k = rmsnorm(p[:, :, :, 1], k_norm[:, None, :])
    v = p[:, :, :, 2]
    slopes = alibi_slopes(G * H).reshape(G, H)
    scale = dh ** -0.5
    nb = s // Q_BLOCK
    qb = q.reshape(bsz, nb, Q_BLOCK, G, H, dh).swapaxes(0, 1)
    starts = jnp.arange(nb) * Q_BLOCK

    def one(args):
        qblk, t0 = args
        t = t0 + jnp.arange(Q_BLOCK)
        outs, lses = [], []
        for gi, (win, dil) in enumerate(DIL_GROUPS):
            side = win // (2 * dil)
            offs = dil * jnp.arange(-side, side + 1)
            pos = t[:, None] + offs[None, :]
            valid = (pos >= 0) & (pos < s)
            idx = jnp.clip(pos, 0, s - 1)
            kg = jnp.take(k[:, :, gi], idx, axis=1)
            vg = jnp.take(v[:, :, gi], idx, axis=1)
            sc = jnp.einsum('bqhd,bqjhd->bhqj', qblk[:, :, gi], kg).astype(jnp.float32) * scale
            sc = sc - slopes[gi][:, None, None] * jnp.abs(offs).astype(jnp.float32)[None, None, :]
            sc = jnp.where(valid[None, None], sc, -jnp.inf)
            mx = jnp.max(sc, axis=-1, keepdims=True)
            e = jnp.exp(sc - mx)
            den = jnp.sum(e, axis=-1)
            o = jnp.einsum('bhqj,bqjhd->bqhd', (e / den[..., None]).astype(v.dtype), vg)
            outs.append(o)
            lses.append(mx[..., 0] + jnp.log(den))
        wts = jax.nn.softmax(jnp.stack(lses, axis=0), axis=0)
        comb = outs[0] * wts[0].transpose(0, 2, 1)[..., None].astype(v.dtype)
        for gi in range(1, G):
            comb = comb + outs[gi] * wts[gi].transpose(0, 2, 1)[..., None].astype(v.dtype)
        return comb

    o = lax.map(one, (qb, starts))
    o = o.swapaxes(0, 1).reshape(bsz, s, H * dh)
    return o @ w_out


def moe_ffn(xn, router_w, router_b, w_gate, w_up, w_down):
    bsz, s, d = xn.shape
    xt = xn.reshape(bsz * s, d)
    scores = jax.nn.sigmoid((xt @ router_w).astype(jnp.float32))
    sel = scores + router_b.astype(jnp.float32)
    grp_score = jnp.sum(lax.top_k(sel.reshape(-1, N_GROUPS, EXPERTS_PER_GROUP), 2)[0], axis=-1)
    gmask = jax.nn.one_hot(jnp.argmax(grp_score, axis=-1), N_GROUPS, dtype=jnp.bool_)
    masked = jnp.where(jnp.repeat(gmask, EXPERTS_PER_GROUP, axis=-1), sel, -jnp.inf)
    _, eidx = lax.top_k(masked, TOP_K)
    w = jnp.take_along_axis(scores, eidx, axis=-1)
    w = w / jnp.sum(w, axis=-1, keepdims=True)
    gates = jnp.sum(jax.nn.one_hot(eidx, N_EXPERTS, dtype=jnp.float32) * w[..., None], axis=1)
    hg = jnp.einsum('td,edf->tef', xt, w_gate)
    hu = jnp.einsum('td,edf->tef', xt, w_up)
    a = jax.nn.silu(hg) * hu * gates[..., None].astype(xt.dtype)
    y = jnp.einsum('tef,efd->td', a, w_down)
    return y.reshape(bsz, s, d)


def setup_inputs(seed: int = 0) -> dict:
    key = jax.random.key(seed)
    ks = iter(jax.random.split(key, 32))
    nrm = lambda shape, std: jax.random.normal(next(ks), shape, jnp.float32) * std
    n_a = len(range(0, DEPTH, N_MIXERS))
    n_b = len(range(1, DEPTH, N_MIXERS))
    n_c = len(range(2, DEPTH, N_MIXERS))
    res = (2 * DEPTH) ** -0.5
    ib = nrm((n_a, 2, MLSTM_HEADS), 0.1)
    fb = 3.0 + nrm((n_a, 2, MLSTM_HEADS), 0.5)
    gate_b = jnp.stack([ib[:, 0], fb[:, 0], ib[:, 1], fb[:, 1]], axis=1).reshape(n_a, 4 * MLSTM_HEADS)
    dil_w_out_in = DIL_HEADS * DIL_DH
    return {
        "x": nrm((BATCH, SEQ, D_MODEL), 1.0),
        "norm_mix": 1.0 + nrm((DEPTH, D_MODEL), 0.02),
        "norm_ffn": 1.0 + nrm((DEPTH, D_MODEL), 0.02),
        "a_w_in": nrm((n_a, D_MODEL, MLSTM_COLS), D_MODEL ** -0.5),
        "a_conv_w": nrm((n_a, MLSTM_CONV, MLSTM_QK_CH), MLSTM_CONV ** -0.5),
        "a_gate_b": gate_b,
        "a_out_norm": 1.0 + nrm((n_a, MLSTM_HEADS * MLSTM_DV), 0.02),
        "a_w_out": nrm((n_a, MLSTM_HEADS * MLSTM_DV, D_MODEL), (MLSTM_HEADS * MLSTM_DV) ** -0.5 * res),
        "b_w_in": nrm((n_b, D_MODEL, MLA_COLS), D_MODEL ** -0.5),
        "b_kv_norm": 1.0 + nrm((n_b, MLA_KV_RANK), 0.02),
        "b_w_ukv": nrm((n_b, MLA_KV_RANK, MLA_HEADS * (MLA_NOPE + MLA_DV)), MLA_KV_RANK ** -0.5),
        "b_q_norm": 1.0 + nrm((n_b, MLA_NOPE + MLA_ROPE), 0.02),
        "b_k_norm": 1.0 + nrm((n_b, MLA_NOPE + MLA_ROPE), 0.02),
        "b_w_out": nrm((n_b, MLA_HEADS * MLA_DV, D_MODEL), (MLA_HEADS * MLA_DV) ** -0.5 * res),
        "c_w_in": nrm((n_c, D_MODEL, DIL_COLS), D_MODEL ** -0.5),
        "c_q_norm": 1.0 + nrm((n_c, len(DIL_GROUPS), DIL_DH), 0.02),
        "c_k_norm": 1.0 + nrm((n_c, len(DIL_GROUPS), DIL_DH), 0.02),
        "c_w_out": nrm((n_c, dil_w_out_in, D_MODEL), dil_w_out_in ** -0.5 * res),
        "router_w": nrm((D_MODEL, N_EXPERTS), D_MODEL ** -0.5),
        "router_b": nrm((N_EXPERTS,), 0.01),
        "moe_w_gate": nrm((DEPTH, N_EXPERTS, D_MODEL, D_EXPERT), D_MODEL ** -0.5),
        "moe_w_up": nrm((DEPTH, N_EXPERTS, D_MODEL, D_EXPERT), D_MODEL ** -0.5),
        "moe_w_down": nrm((DEPTH, N_EXPERTS, D_EXPERT, D_MODEL), D_EXPERT ** -0.5 * res),
    }


def reference(x, norm_mix, norm_ffn, a_w_in, a_conv_w, a_gate_b, a_out_norm, a_w_out,
              b_w_in, b_kv_norm, b_w_ukv, b_q_norm, b_k_norm, b_w_out,
              c_w_in, c_q_norm, c_k_norm, c_w_out,
              router_w, router_b, moe_w_gate, moe_w_up, moe_w_down):
    for i in range(DEPTH):
        kind = i % N_MIXERS
        j = i // N_MIXERS
        xn = rmsnorm(x, norm_mix[i])
        if kind == 0:
            y = mlstm_mixer(xn, a_w_in[j], a_conv_w[j], a_gate_b[j], a_out_norm[j], a_w_out[j])
        elif kind == 1:
            y = mla_mixer(xn, b_w_in[j], b_kv_norm[j], b_w_ukv[j], b_q_norm[j], b_k_norm[j], b_w_out[j])
        else:
            y = dilated_mixer(xn, c_w_in[j], c_q_norm[j], c_k_norm[j], c_w_out[j])
        x = x + y
        x = x + moe_ffn(rmsnorm(x, norm_ffn[i]), router_w, router_b, moe_w_gate[i], moe_w_up[i], moe_w_down[i])
    return x
```

```python
import functools
import math

import jax
import jax.numpy as jnp
from jax import lax
from jax.experimental import pallas as pl
from jax.experimental.pallas import tpu as pltpu

F32 = jnp.float32
BF16 = jnp.bfloat16

EPS = 1e-6
LANES = 128
VMEM_LIMIT_BYTES = 56 * 1024 * 1024

MLSTM_HEADS = 8
MLSTM_DQK = 128
MLSTM_DV = 256
MLSTM_CHUNK = 128

MLA_HEADS = 16
MLA_NOPE = 128
MLA_ROPE = 64
MLA_DV = 128
MLA_KV_RANK = 512
ROPE_BASE = 10000.0

DIL_GROUPS = ((128, 1), (512, 4), (2048, 16))
DIL_HEADS = 8
DIL_DH = 128
DIL_SIDE = 64

N_EXPERTS = 16
N_GROUPS = 4
EXPERTS_PER_GROUP = 4

NT_DIMS = (((1,), (1,)), ((), ()))
TN_DIMS = (((0,), (0,)), ((), ()))


def _params(*sem):
    return pltpu.CompilerParams(dimension_semantics=sem, vmem_limit_bytes=VMEM_LIMIT_BYTES)


def _rms_scale(x, width):
    return lax.rsqrt(jnp.sum(x * x, axis=-1, keepdims=True) * (1.0 / width) + EPS)


def _norm_matmul_kernel(x_ref, g_ref, w_ref, o_ref, xn_ref):
    @pl.when(pl.program_id(1) == 0)
    def _():
        x = x_ref[...].astype(F32)
        xn_ref[...] = (x * _rms_scale(x, x.shape[-1]) * g_ref[...]).astype(BF16)

    o_ref[...] = jnp.dot(xn_ref[...], w_ref[...], preferred_element_type=F32).astype(o_ref.dtype)


def norm_matmul(x, g, w, *, kdim, tm, tn, out_dtype=F32):
    t = x.shape[0]
    n = w.shape[1]
    return pl.pallas_call(
        _norm_matmul_kernel,
        grid=(t // tm, n // tn),
        in_specs=[
            pl.BlockSpec((tm, kdim), lambda i, j: (i, 0)),
            pl.BlockSpec((1, kdim), lambda i, j: (0, 0)),
            pl.BlockSpec((kdim, tn), lambda i, j: (0, j)),
        ],
        out_specs=pl.BlockSpec((tm, tn), lambda i, j: (i, j)),
        out_shape=jax.ShapeDtypeStruct((t, n), out_dtype),
        scratch_shapes=[pltpu.VMEM((tm, kdim), BF16)],
        compiler_params=_params("parallel", "arbitrary"),
        name="norm_matmul",
    )(x, g.reshape(1, kdim), w)


def _matmul_res_kernel(a_ref, w_ref, r_ref, o_ref):
    o_ref[...] = r_ref[...] + jnp.dot(a_ref[...].astype(BF16), w_ref[...], preferred_element_type=F32)


def matmul_res(a, w, res, *, tm, tn):
    t, k = a.shape
    n = w.shape[1]
    return pl.pallas_call(
        _matmul_res_kernel,
        grid=(t // tm, n // tn),
        in_specs=[
            pl.BlockSpec((tm, k), lambda i, j: (i, 0)),
            pl.BlockSpec((k, tn), lambda i, j: (0, j)),
            pl.BlockSpec((tm, tn), lambda i, j: (i, j)),
        ],
        out_specs=pl.BlockSpec((tm, tn), lambda i, j: (i, j)),
        out_shape=jax.ShapeDtypeStruct((t, n), F32),
        compiler_params=_params("parallel", "arbitrary"),
        name="matmul_res",
    )(a, w, res)


def _log_sigmoid(x):
    return jnp.minimum(x, 0.0) - jnp.log(1.0 + jnp.exp(-jnp.abs(x)))


def _mlstm_kernel(q_ref, k_ref, v_ref, o_ref, cwq_ref, cwk_ref, gates_ref, onorm_ref, out_ref,
                  qs_ref, ks_ref, h_ref):
    s = q_ref.shape[1]
    chunk = MLSTM_CHUNK
    n_chunks = s // chunk
    row = lax.broadcasted_iota(jnp.int32, (s, 1), 0)

    def conv_silu(p, w):
        prev = jnp.where(row == 0, 0.0, pltpu.roll(p, 1, axis=0))
        nxt = jnp.where(row == s - 1, 0.0, pltpu.roll(p, s - 1, axis=0))
        y = w[0:1, :] * prev + w[1:2, :] * p + w[2:3, :] * nxt
        return y * jax.nn.sigmoid(y)

    qs_ref[...] = (conv_silu(q_ref[0], cwq_ref[...]) * (MLSTM_DQK ** -0.5)).astype(BF16)
    ks_ref[...] = conv_silu(k_ref[0], cwk_ref[...]).astype(BF16)

    ti = lax.broadcasted_iota(jnp.int32, (chunk, chunk), 0)
    si = lax.broadcasted_iota(jnp.int32, (chunk, chunk), 1)
    eye = ti == si

    def to_col(r):
        return jnp.sum(jnp.where(eye, jnp.broadcast_to(r, (chunk, chunk)), 0.0), axis=1, keepdims=True)

    def to_row(c):
        return jnp.sum(jnp.where(eye, jnp.broadcast_to(c, (chunk, chunk)), 0.0), axis=0, keepdims=True)

    def run(i_idx, f_idx, reverse):
        mask = (si >= ti) if reverse else (si <= ti)

        def body(ci, carry):
            c_st, n_st, m_st = carry
            c = (n_chunks - 1 - ci) if reverse else ci
            r0 = pl.multiple_of(c * chunk, chunk)
            q = qs_ref[pl.ds(r0, chunk), :]
            k = ks_ref[pl.ds(r0, chunk), :]
            v = v_ref[0, pl.ds(r0, chunk), :].astype(BF16)
            li = gates_ref[0, 0, i_idx:i_idx + 1, pl.ds(r0, chunk)]
            lf = _log_sigmoid(gates_ref[0, 0, f_idx:f_idx + 1, pl.ds(r0, chunk)])
            b_col = jnp.sum(jnp.where(mask, jnp.broadcast_to(lf, (chunk, chunk)), 0.0), axis=1, keepdims=True)
            b_row = to_row(b_col)
            g = jnp.sum(lf, axis=1, keepdims=True)
            dlog = jnp.where(mask, b_col - b_row + li, -jnp.inf)
            m_intra = jnp.max(dlog, axis=1, keepdims=True)
            a_inter = b_col + m_st
            m = jnp.maximum(a_inter, m_intra)
            e_inter = jnp.exp(a_inter - m)
            s_qk = lax.dot_general(q, k, NT_DIMS, preferred_element_type=F32)
            dmat = jnp.exp(dlog - m) * s_qk
            num = (e_inter * jnp.dot(q, c_st.astype(BF16), preferred_element_type=F32)
                   + jnp.dot(dmat.astype(BF16), v, preferred_element_type=F32))
            den = (e_inter * jnp.sum(q.astype(F32) * n_st, axis=1, keepdims=True)
                   + jnp.sum(dmat, axis=1, keepdims=True))
            h = num / jnp.maximum(jnp.abs(den), jnp.exp(-m))
            if reverse:
                h_ref[pl.ds(r0, chunk), :] += h
            else:
                h_ref[pl.ds(r0, chunk), :] = h
            w_row = g - b_row + li
            m_loc = jnp.max(w_row, axis=1, keepdims=True)
            ew_col = to_col(jnp.exp(w_row - m_loc))
            kw = k.astype(F32) * ew_col
            c_loc = lax.dot_general(kw.astype(BF16), v, TN_DIMS, preferred_element_type=F32)
            n_loc = jnp.sum(kw, axis=0, keepdims=True)
            m_new = jnp.maximum(g + m_st, m_loc)
            a = jnp.exp(g + m_st - m_new)
            e = jnp.exp(m_loc - m_new)
            return a * c_st + e * c_loc, a * n_st + e * n_loc, m_new

        init = (jnp.zeros((MLSTM_DQK, MLSTM_DV), F32), jnp.zeros((1, MLSTM_DQK), F32), jnp.zeros((1, 1), F32))
        lax.fori_loop(0, n_chunks, body, init)

    run(0, 1, False)
    run(2, 3, True)

    h = h_ref[...]
    y = h * _rms_scale(h, MLSTM_DV) * onorm_ref[...]
    out_ref[0] = (jax.nn.sigmoid(o_ref[0]) * y).astype(out_ref.dtype)


def mlstm_core(p, conv_w, gates_rows, out_norm):
    b, s, _ = p.shape
    nh = MLSTM_HEADS
    k_off = nh * MLSTM_DQK // MLSTM_DQK
    v_off = 2 * nh * MLSTM_DQK // MLSTM_DV
    o_off = v_off + nh
    return pl.pallas_call(
        _mlstm_kernel,
        grid=(b, nh),
        in_specs=[
            pl.BlockSpec((1, s, MLSTM_DQK), lambda i, h: (i, 0, h)),
            pl.BlockSpec((1, s, MLSTM_DQK), lambda i, h: (i, 0, k_off + h)),
            pl.BlockSpec((1, s, MLSTM_DV), lambda i, h: (i, 0, v_off + h)),
            pl.BlockSpec((1, s, MLSTM_DV), lambda i, h: (i, 0, o_off + h)),
            pl.BlockSpec((3, MLSTM_DQK), lambda i, h: (0, h)),
            pl.BlockSpec((3, MLSTM_DQK), lambda i, h: (0, k_off + h)),
            pl.BlockSpec((1, 1, 4, s), lambda i, h: (i, h, 0, 0)),
            pl.BlockSpec((1, MLSTM_DV), lambda i, h: (0, h)),
        ],
        out_specs=pl.BlockSpec((1, s, MLSTM_DV), lambda i, h: (i, 0, h)),
        out_shape=jax.ShapeDtypeStruct((b, s, nh * MLSTM_DV), BF16),
        scratch_shapes=[
            pltpu.VMEM((s, MLSTM_DQK), BF16),
            pltpu.VMEM((s, MLSTM_DQK), BF16),
            pltpu.VMEM((s, MLSTM_DV), F32),
        ],
        compiler_params=_params("parallel", "parallel"),
        name="mlstm_core",
    )(p, p, p, p, conv_w, conv_w, gates_rows, out_norm.reshape(1, -1))


def mlstm_mixer(x2d, norm_g, w_in, conv_w, gate_b, out_norm, w_out, bsz, s):
    nh = MLSTM_HEADS
    main_cols = 2 * nh * MLSTM_DQK + 2 * nh * MLSTM_DV
    p = norm_matmul(x2d, norm_g, w_in[:, :main_cols].astype(BF16), kdim=x2d.shape[1], tm=512, tn=512)
    gates = norm_matmul(x2d, norm_g, w_in[:, main_cols:].astype(BF16), kdim=x2d.shape[1], tm=512, tn=4 * nh)
    gates = gates + gate_b.astype(F32)
    gates_rows = gates.reshape(bsz, s, 4, nh).transpose(0, 3, 2, 1)
    hg = mlstm_core(p.reshape(bsz, s, main_cols), conv_w, gates_rows, out_norm)
    return matmul_res(hg.reshape(bsz * s, nh * MLSTM_DV), w_out.astype(BF16), x2d, tm=512, tn=512)


MLA_TQ = 256


def _mla_kernel(qn_ref, qr_ref, kv_ref, kr_ref, cos_ref, sin_ref, gqn_ref, gqr_ref, gkn_ref, gkr_ref,
                out_ref, qs_ref, ks_ref, vs_ref):
    s = qn_ref.shape[1]
    dqk = MLA_NOPE + MLA_ROPE
    half = MLA_ROPE // 2
    cos = cos_ref[...]
    sin = sin_ref[...]

    def prep(nope, ropep, g_nope, g_rope, scale):
        ss = jnp.sum(nope * nope, axis=-1, keepdims=True) + jnp.sum(ropep * ropep, axis=-1, keepdims=True)
        r = lax.rsqrt(ss * (1.0 / dqk) + EPS) * scale
        n = nope * r * g_nope
        rp = ropep * r * g_rope
        x1 = rp[:, :half]
        x2 = rp[:, half:]
        o1 = x1 * cos - x2 * sin
        o2 = x2 * cos + x1 * sin
        pad = jnp.zeros((s, 2 * LANES - dqk), F32)
        return jnp.concatenate([n, o1, o2, pad], axis=1).astype(BF16)

    kr = kr_ref[0]
    for hh in range(2):
        qn = qn_ref[0, :, hh * MLA_NOPE:(hh + 1) * MLA_NOPE]
        qr = qr_ref[0, :, hh * MLA_ROPE:(hh + 1) * MLA_ROPE]
        kn = kv_ref[0, :, hh * 2 * LANES:hh * 2 * LANES + MLA_NOPE]
        vs_ref[...] = kv_ref[0, :, hh * 2 * LANES + MLA_NOPE:(hh + 1) * 2 * LANES].astype(BF16)
        qs_ref[...] = prep(qn, qr, gqn_ref[...], gqr_ref[...], dqk ** -0.5)
        ks_ref[...] = prep(kn, kr, gkn_ref[...], gkr_ref[...], 1.0)

        def qblock(i, _):
            r0 = pl.multiple_of(i * MLA_TQ, MLA_TQ)
            sc = lax.dot_general(qs_ref[pl.ds(r0, MLA_TQ), :], ks_ref[...], NT_DIMS, preferred_element_type=F32)
            mx = jnp.max(sc, axis=-1, keepdims=True)
            e = jnp.exp(sc - mx)
            den = jnp.sum(e, axis=-1, keepdims=True)
            o = jnp.dot(e.astype(BF16), vs_ref[...], preferred_element_type=F32) / den
            out_ref[0, pl.ds(r0, MLA_TQ), hh * MLA_DV:(hh + 1) * MLA_DV] = o.astype(out_ref.dtype)
            return 0

        lax.fori_loop(0, s // MLA_TQ, qblock, 0)


def mla_attention(qall, kv, kr, cos, sin, q_norm, k_norm):
    b, s, _ = qall.shape
    nh = MLA_HEADS
    half = MLA_ROPE // 2
    qr_off = nh * MLA_NOPE // (2 * MLA_ROPE)
    row = lambda t: t.reshape(1, -1).astype(F32)
    vec = lambda w: pl.BlockSpec((1, w), lambda i, h: (0, 0))
    return pl.pallas_call(
        _mla_kernel,
        grid=(b, nh // 2),
        in_specs=[
            pl.BlockSpec((1, s, 2 * MLA_NOPE), lambda i, h: (i, 0, h)),
            pl.BlockSpec((1, s, 2 * MLA_ROPE), lambda i, h: (i, 0, qr_off + h)),
            pl.BlockSpec((1, s, 4 * LANES), lambda i, h: (i, 0, h)),
            pl.BlockSpec((1, s, MLA_ROPE), lambda i, h: (i, 0, 0)),
            pl.BlockSpec((s, half), lambda i, h: (0, 0)),
            pl.BlockSpec((s, half), lambda i, h: (0, 0)),
            vec(MLA_NOPE), vec(MLA_ROPE), vec(MLA_NOPE), vec(MLA_ROPE),
        ],
        out_specs=pl.BlockSpec((1, s, 2 * MLA_DV), lambda i, h: (i, 0, h)),
        out_shape=jax.ShapeDtypeStruct((b, s, nh * MLA_DV), BF16),
        scratch_shapes=[
            pltpu.VMEM((s, 2 * LANES), BF16),
            pltpu.VMEM((s, 2 * LANES), BF16),
            pltpu.VMEM((s, MLA_DV), BF16),
        ],
        compiler_params=_params("parallel", "parallel"),
        name="mla_attention",
    )(qall, qall, kv, kr, cos, sin,
      row(q_norm[:MLA_NOPE]), row(q_norm[MLA_NOPE:]), row(k_norm[:MLA_NOPE]), row(k_norm[MLA_NOPE:]))


def mla_mixer(x2d, norm_g, w_in, kv_norm, w_ukv, q_norm, k_norm, w_out, bsz, s):
    nh = MLA_HEADS
    dqk = MLA_NOPE + MLA_ROPE
    d = x2d.shape[1]
    w_q = w_in[:, :nh * dqk].reshape(d, nh, dqk)
    w_qcat = jnp.concatenate([w_q[:, :, :MLA_NOPE].reshape(d, nh * MLA_NOPE),
                              w_q[:, :, MLA_NOPE:].reshape(d, nh * MLA_ROPE)], axis=1).astype(BF16)
    qall = norm_matmul(x2d, norm_g, w_qcat, kdim=d, tm=512, tn=512)
    lat = norm_matmul(x2d, norm_g, w_in[:, nh * dqk:].astype(BF16), kdim=d, tm=512, tn=MLA_KV_RANK + MLA_ROPE)
    kv = norm_matmul(lat, kv_norm, w_ukv.astype(BF16), kdim=MLA_KV_RANK, tm=1024, tn=512)
    kr = lat[:, MLA_KV_RANK:]
    half = MLA_ROPE // 2
    inv = ROPE_BASE ** (-jnp.arange(half, dtype=F32) / half)
    ang = jnp.arange(s).astype(F32)[:, None] * inv[None, :]
    o = mla_attention(qall.reshape(bsz, s, -1), kv.reshape(bsz, s, -1), kr.reshape(bsz, s, MLA_ROPE),
                      jnp.cos(ang), jnp.sin(ang), q_norm, k_norm)
    return matmul_res(o.reshape(bsz * s, nh * MLA_DV), w_out.astype(BF16), x2d, tm=512, tn=512)


DIL_QB = 128
DIL_KW = 256


def _dilated_kernel(*refs):
    n_g = len(DIL_GROUPS)
    qkv_refs = refs[:3 * n_g]
    gq_ref, gk_ref = refs[3 * n_g], refs[3 * n_g + 1]
    out_ref = refs[3 * n_g + 2]
    qs_ref, ks_ref, vs_ref = refs[3 * n_g + 3:3 * n_g + 6]
    acc_refs = refs[3 * n_g + 6:3 * n_g + 6 + n_g]
    m_refs = refs[3 * n_g + 6 + n_g:3 * n_g + 6 + 2 * n_g]
    l_refs = refs[3 * n_g + 6 + 2 * n_g:3 * n_g + 6 + 3 * n_g]
    s = out_ref.shape[1]
    head = pl.program_id(1)
    scale = DIL_DH ** -0.5

    for gi, (_, dil) in enumerate(DIL_GROUPS):
        q_ref, k_ref, v_ref = qkv_refs[3 * gi:3 * gi + 3]
        n = s // dil
        kw = min(DIL_KW, n)
        slope_arg = jnp.full((1, 1), -8.0 / (n_g * DIL_HEADS), F32) * (gi * DIL_HEADS + head + 1).astype(F32)
        bias_step = jnp.exp2(slope_arg) * float(dil)
        gq = gq_ref[gi:gi + 1, :]
        gk = gk_ref[gi:gi + 1, :]

        def residue(r, _, q_ref=q_ref, k_ref=k_ref, v_ref=v_ref, n=n, kw=kw, dil=dil, gi=gi,
                    bias_step=bias_step, gq=gq, gk=gk):
            def sub(ref):
                if dil == 1:
                    return ref[0]
                return ref[0, pl.ds(r, n, stride=dil), :]

            q = sub(q_ref)
            k = sub(k_ref)
            qs_ref[0:n, :] = (q * _rms_scale(q, DIL_DH) * gq * scale).astype(BF16)
            ks_ref[0:n, :] = (k * _rms_scale(k, DIL_DH) * gk).astype(BF16)
            vs_ref[0:n, :] = sub(v_ref).astype(BF16)
            for qb in range(n // DIL_QB):
                q0 = qb * DIL_QB
                k0 = min(max(q0 - DIL_SIDE, 0), n - kw)
                sc = lax.dot_general(qs_ref[q0:q0 + DIL_QB, :], ks_ref[k0:k0 + kw, :], NT_DIMS,
                                     preferred_element_type=F32)
                dist = jnp.abs((k0 + lax.broadcasted_iota(jnp.int32, (DIL_QB, kw), 1))
                               - (q0 + lax.broadcasted_iota(jnp.int32, (DIL_QB, kw), 0)))
                sc = sc - bias_step * dist.astype(F32)
                sc = jnp.where(dist <= DIL_SIDE, sc, -jnp.inf)
                mx = jnp.max(sc, axis=-1, keepdims=True)
                e = jnp.exp(sc - mx)
                den = jnp.sum(e, axis=-1, keepdims=True)
                acc = jnp.dot(e.astype(BF16), vs_ref[k0:k0 + kw, :], preferred_element_type=F32)
                if dil == 1:
                    rows = pl.ds(q0, DIL_QB)
                else:
                    rows = pl.ds(r + dil * q0, DIL_QB, stride=dil)
                acc_refs[gi][rows, :] = acc
                m_refs[gi][rows, :] = jnp.broadcast_to(mx, (DIL_QB, DIL_DH))
                l_refs[gi][rows, :] = jnp.broadcast_to(den, (DIL_QB, DIL_DH))
            return 0

        if dil == 1:
            residue(0, 0)
        else:
            lax.fori_loop(0, dil, residue, 0)

    m_all = m_refs[0][...]
    for gi in range(1, n_g):
        m_all = jnp.maximum(m_all, m_refs[gi][...])
    num = jnp.zeros((s, DIL_DH), F32)
    den = jnp.zeros((s, DIL_DH), F32)
    for gi in range(n_g):
        w = jnp.exp(m_refs[gi][...] - m_all)
        num = num + acc_refs[gi][...] * w
        den = den + l_refs[gi][...] * w
    out_ref[0] = (num / den).astype(out_ref.dtype)


def dilated_attention(p, q_norm, k_norm):
    b, s, _ = p.shape
    n_g = len(DIL_GROUPS)
    nh = DIL_HEADS

    def spec(col0):
        return pl.BlockSpec((1, s, DIL_DH), lambda i, h: (i, 0, col0 + h))

    in_specs = [spec((gi * 3 + which) * nh) for gi in range(n_g) for which in range(3)]
    in_specs += [pl.BlockSpec((n_g, DIL_DH), lambda i, h: (0, 0))] * 2
    scratch = [pltpu.VMEM((s, DIL_DH), BF16)] * 3 + [pltpu.VMEM((s, DIL_DH), F32)] * (3 * n_g)
    return pl.pallas_call(
        _dilated_kernel,
        grid=(b, nh),
        in_specs=in_specs,
        out_specs=pl.BlockSpec((1, s, DIL_DH), lambda i, h: (i, 0, h)),
        out_shape=jax.ShapeDtypeStruct((b, s, nh * DIL_DH), BF16),
        scratch_shapes=scratch,
        compiler_params=_params("parallel", "parallel"),
        name="dilated_attention",
    )(*([p] * (3 * n_g)), q_norm.astype(F32), k_norm.astype(F32))


def dilated_mixer(x2d, norm_g, w_in, q_norm, k_norm, w_out, bsz, s):
    p = norm_matmul(x2d, norm_g, w_in.astype(BF16), kdim=x2d.shape[1], tm=512, tn=512)
    o = dilated_attention(p.reshape(bsz, s, -1), q_norm, k_norm)
    return matmul_res(o.reshape(bsz * s, -1), w_out.astype(BF16), x2d, tm=512, tn=512)


def _router_kernel(x_ref, g_ref, rw_ref, rb_ref, gates_ref):
    x = x_ref[...]
    xn = x * _rms_scale(x, x.shape[-1]) * g_ref[...]
    logits = lax.dot_general(rw_ref[...], xn, NT_DIMS, precision=lax.Precision.HIGHEST,
                             preferred_element_type=F32)
    scores = jax.nn.sigmoid(logits)
    sel = scores + rb_ref[...]
    ng = N_GROUPS
    slots = range(EXPERTS_PER_GROUP)
    v = [sel[j * ng:(j + 1) * ng, :] for j in slots]
    sc = [scores[j * ng:(j + 1) * ng, :] for j in slots]

    def first_argmax(vals):
        best = vals[0]
        for t in vals[1:]:
            best = jnp.maximum(best, t)
        idx = jnp.full(best.shape, len(vals) - 1, jnp.int32)
        for j in reversed(range(len(vals) - 1)):
            idx = jnp.where(vals[j] == best, j, idx)
        return best, idx

    m1, i1 = first_argmax(v)
    m2, i2 = first_argmax([jnp.where(i1 == j, -jnp.inf, v[j]) for j in slots])
    grp_score = m1 + m2
    grp_iota = lax.broadcasted_iota(jnp.int32, grp_score.shape, 0)
    best = jnp.max(grp_score, axis=0, keepdims=True)
    grp = jnp.min(jnp.where(grp_score == best, grp_iota, ng), axis=0, keepdims=True)
    w1 = sum(jnp.where(i1 == j, sc[j], 0.0) for j in slots)
    w2 = sum(jnp.where(i2 == j, sc[j], 0.0) for j in slots)
    tot = w1 + w2
    chosen = grp_iota == grp
    for j in slots:
        gj = jnp.where(i1 == j, w1 / tot, 0.0) + jnp.where(i2 == j, w2 / tot, 0.0)
        gates_ref[j * ng:(j + 1) * ng, :] = jnp.where(chosen, gj, 0.0)


def router_gates(x2d, norm_g, router_w, router_b, *, tm):
    t, d = x2d.shape
    ne = N_EXPERTS
    perm = jnp.arange(ne).reshape(N_GROUPS, EXPERTS_PER_GROUP).T.reshape(-1)
    rw_t = router_w.astype(F32).T[perm]
    rb = router_b.astype(F32)[perm].reshape(ne, 1)
    gates_t = pl.pallas_call(
        _router_kernel,
        grid=(t // tm,),
        in_specs=[
            pl.BlockSpec((tm, d), lambda i: (i, 0)),
            pl.BlockSpec((1, d), lambda i: (0, 0)),
            pl.BlockSpec((ne, d), lambda i: (0, 0)),
            pl.BlockSpec((ne, 1), lambda i: (0, 0)),
        ],
        out_specs=pl.BlockSpec((ne, tm), lambda i: (0, i)),
        out_shape=jax.ShapeDtypeStruct((ne, t), F32),
        compiler_params=_params("parallel"),
        name="router",
    )(x2d, norm_g.reshape(1, d), rw_t, rb)
    return gates_t[jnp.argsort(perm)].T


def _moe_dense_kernel(x_ref, g_ref, gates_ref, wg_ref, wu_ref, wd_ref, o_ref, xn_ref, acc_ref):
    e = pl.program_id(1)

    @pl.when(e == 0)
    def _():
        x = x_ref[...]
        xn_ref[...] = (x * _rms_scale(x, x.shape[-1]) * g_ref[...]).astype(BF16)
        acc_ref[...] = jnp.zeros_like(acc_ref)

    xn = xn_ref[...]
    hg = jnp.dot(xn, wg_ref[0], preferred_element_type=F32)
    hu = jnp.dot(xn, wu_ref[0], preferred_element_type=F32)
    gates = gates_ref[...]
    lane = lax.broadcasted_iota(jnp.int32, gates.shape, 1)
    gate = jnp.sum(jnp.where(lane == e, gates, 0.0), axis=1, keepdims=True)
    a = hg * jax.nn.sigmoid(hg) * hu * gate
    acc_ref[...] += jnp.dot(a.astype(BF16), wd_ref[0], preferred_element_type=F32)

    @pl.when(e == pl.num_programs(1) - 1)
    def _():
        o_ref[...] = x_ref[...] + acc_ref[...]


def moe_dense(x2d, norm_g, gates, w_gate, w_up, w_down, *, tm):
    t, d = x2d.shape
    ne, _, f = w_gate.shape
    return pl.pallas_call(
        _moe_dense_kernel,
        grid=(t // tm, ne),
        in_specs=[
            pl.BlockSpec((tm, d), lambda i, e: (i, 0)),
            pl.BlockSpec((1, d), lambda i, e: (0, 0)),
            pl.BlockSpec((tm, ne), lambda i, e: (i, 0)),
            pl.BlockSpec((1, d, f), lambda i, e: (e, 0, 0)),
            pl.BlockSpec((1, d, f), lambda i, e: (e, 0, 0)),
            pl.BlockSpec((1, f, d), lambda i, e: (e, 0, 0)),
        ],
        out_specs=pl.BlockSpec((tm, d), lambda i, e: (i, 0)),
        out_shape=jax.ShapeDtypeStruct((t, d), F32),
        scratch_shapes=[pltpu.VMEM((tm, d), BF16), pltpu.VMEM((tm, d), F32)],
        compiler_params=_params("parallel", "arbitrary"),
        name="moe_dense",
    )(x2d, norm_g.reshape(1, d), gates, w_gate, w_up, w_down)


def moe_ffn(x2d, norm_g, router_w, router_b, w_gate, w_up, w_down):
    gates = router_gates(x2d, norm_g, router_w, router_b, tm=512)
    return moe_dense(x2d, norm_g, gates, w_gate.astype(BF16), w_up.astype(BF16), w_down.astype(BF16), tm=512)


def kernel(x, norm_mix, norm_ffn, a_w_in, a_conv_w, a_gate_b, a_out_norm, a_w_out, b_w_in, b_kv_norm, b_w_ukv, b_q_norm, b_k_norm, b_w_out, c_w_in, c_q_norm, c_k_norm, c_w_out, router_w, router_b, moe_w_gate, moe_w_up, moe_w_down):
    bsz, s, d = x.shape
    depth = norm_mix.shape[0]
    x2d = x.reshape(bsz * s, d)
    for i in range(depth):
        kind = i % 3
        j = i // 3
        if kind == 0:
            x2d = mlstm_mixer(x2d, norm_mix[i], a_w_in[j], a_conv_w[j], a_gate_b[j], a_out_norm[j], a_w_out[j],
                              bsz, s)
        elif kind == 1:
            x2d = mla_mixer(x2d, norm_mix[i], b_w_in[j], b_kv_norm[j], b_w_ukv[j], b_q_norm[j], b_k_norm[j],
                            b_w_out[j], bsz, s)
        else:
            x2d = dilated_mixer(x2d, norm_mix[i], c_w_in[j], c_q_norm[j], c_k_norm[j], c_w_out[j], bsz, s)
        x2d = moe_ffn(x2d, norm_ffn[i], router_w, router_b, moe_w_gate[i], moe_w_up[i], moe_w_down[i])
    return x2d.reshape(bsz, s, d)
```

```python
import functools
import math

import jax
import jax.numpy as jnp
from jax import lax
from jax.experimental import pallas as pl
from jax.experimental.pallas import tpu as pltpu

F32 = jnp.float32
BF16 = jnp.bfloat16

EPS = 1e-6
LANES = 128
VMEM_LIMIT_BYTES = 56 * 1024 * 1024

MLSTM_HEADS = 8
MLSTM_DQK = 128
MLSTM_DV = 256
MLSTM_CHUNK = 128

MLA_HEADS = 16
MLA_NOPE = 128
MLA_ROPE = 64
MLA_DV = 128
MLA_KV_RANK = 512
ROPE_BASE = 10000.0

DIL_GROUPS = ((128, 1), (512, 4), (2048, 16))
DIL_HEADS = 8
DIL_DH = 128
DIL_SIDE = 64

N_EXPERTS = 16
N_GROUPS = 4
EXPERTS_PER_GROUP = 4
TOP_K = 2
MOE_TM = 256
MOE_TC = 256
DMA_UNROLL = 8

NT_DIMS = (((1,), (1,)), ((), ()))
TN_DIMS = (((0,), (0,)), ((), ()))


def _params(*sem):
    return pltpu.CompilerParams(dimension_semantics=sem, vmem_limit_bytes=VMEM_LIMIT_BYTES)


def _rms_scale(x, width):
    return lax.rsqrt(jnp.sum(x * x, axis=-1, keepdims=True) * (1.0 / width) + EPS)


def _norm_matmul_kernel(x_ref, g_ref, w_ref, o_ref, xn_ref):
    @pl.when(pl.program_id(1) == 0)
    def _():
        x = x_ref[...].astype(F32)
        xn_ref[...] = (x * _rms_scale(x, x.shape[-1]) * g_ref[...]).astype(BF16)

    o_ref[...] = jnp.dot(xn_ref[...], w_ref[...], preferred_element_type=F32).astype(o_ref.dtype)


def norm_matmul(x, g, w, *, kdim, tm, tn, out_dtype=F32):
    t = x.shape[0]
    n = w.shape[1]
    return pl.pallas_call(
        _norm_matmul_kernel,
        grid=(t // tm, n // tn),
        in_specs=[
            pl.BlockSpec((tm, kdim), lambda i, j: (i, 0)),
            pl.BlockSpec((1, kdim), lambda i, j: (0, 0)),
            pl.BlockSpec((kdim, tn), lambda i, j: (0, j)),
        ],
        out_specs=pl.BlockSpec((tm, tn), lambda i, j: (i, j)),
        out_shape=jax.ShapeDtypeStruct((t, n), out_dtype),
        scratch_shapes=[pltpu.VMEM((tm, kdim), BF16)],
        compiler_params=_params("parallel", "arbitrary"),
        name="norm_matmul",
    )(x, g.reshape(1, kdim), w)


def _matmul_res_kernel(a_ref, w_ref, r_ref, o_ref):
    o_ref[...] = r_ref[...] + jnp.dot(a_ref[...].astype(BF16), w_ref[...], preferred_element_type=F32)


def matmul_res(a, w, res, *, tm, tn):
    t, k = a.shape
    n = w.shape[1]
    return pl.pallas_call(
        _matmul_res_kernel,
        grid=(t // tm, n // tn),
        in_specs=[
            pl.BlockSpec((tm, k), lambda i, j: (i, 0)),
            pl.BlockSpec((k, tn), lambda i, j: (0, j)),
            pl.BlockSpec((tm, tn), lambda i, j: (i, j)),
        ],
        out_specs=pl.BlockSpec((tm, tn), lambda i, j: (i, j)),
        out_shape=jax.ShapeDtypeStruct((t, n), F32),
        compiler_params=_params("parallel", "arbitrary"),
        name="matmul_res",
    )(a, w, res)


def _log_sigmoid(x):
    return jnp.minimum(x, 0.0) - jnp.log(1.0 + jnp.exp(-jnp.abs(x)))


def _mlstm_kernel(q_ref, k_ref, v_ref, o_ref, cwq_ref, cwk_ref, gates_ref, onorm_ref, out_ref,
                  qs_ref, ks_ref, hf_ref, hb_ref):
    s = q_ref.shape[1]
    chunk = MLSTM_CHUNK
    n_chunks = s // chunk
    row = lax.broadcasted_iota(jnp.int32, (s, 1), 0)

    def conv_silu(p, w):
        prev = jnp.where(row == 0, 0.0, pltpu.roll(p, 1, axis=0))
        nxt = jnp.where(row == s - 1, 0.0, pltpu.roll(p, s - 1, axis=0))
        y = w[0:1, :] * prev + w[1:2, :] * p + w[2:3, :] * nxt
        return y * jax.nn.sigmoid(y)

    qs_ref[...] = (conv_silu(q_ref[0], cwq_ref[...]) * (MLSTM_DQK ** -0.5)).astype(BF16)
    ks_ref[...] = conv_silu(k_ref[0], cwk_ref[...]).astype(BF16)

    ti = lax.broadcasted_iota(jnp.int32, (chunk, chunk), 0)
    si = lax.broadcasted_iota(jnp.int32, (chunk, chunk), 1)
    eye = ti == si

    def to_col(r):
        return jnp.sum(jnp.where(eye, jnp.broadcast_to(r, (chunk, chunk)), 0.0), axis=1, keepdims=True)

    def to_row(c):
        return jnp.sum(jnp.where(eye, jnp.broadcast_to(c, (chunk, chunk)), 0.0), axis=0, keepdims=True)

    def chunk_step(c, i_idx, f_idx, reverse, carry, hdir_ref):
        mask = (si >= ti) if reverse else (si <= ti)
        c_st, n_st, m_st = carry
        r0 = pl.multiple_of(c * chunk, chunk)
        q = qs_ref[pl.ds(r0, chunk), :]
        k = ks_ref[pl.ds(r0, chunk), :]
        v = v_ref[0, pl.ds(r0, chunk), :].astype(BF16)
        li = gates_ref[0, 0, i_idx:i_idx + 1, pl.ds(r0, chunk)]
        lf = _log_sigmoid(gates_ref[0, 0, f_idx:f_idx + 1, pl.ds(r0, chunk)])
        b_col = jnp.sum(jnp.where(mask, jnp.broadcast_to(lf, (chunk, chunk)), 0.0), axis=1, keepdims=True)
        b_row = to_row(b_col)
        g = jnp.sum(lf, axis=1, keepdims=True)
        dlog = jnp.where(mask, b_col - b_row + li, -jnp.inf)
        m_intra = jnp.max(dlog, axis=1, keepdims=True)
        a_inter = b_col + m_st
        m = jnp.maximum(a_inter, m_intra)
        e_inter = jnp.exp(a_inter - m)
        s_qk = lax.dot_general(q, k, NT_DIMS, preferred_element_type=F32)
        dmat = jnp.exp(dlog - m) * s_qk
        num = (e_inter * jnp.dot(q, c_st.astype(BF16), preferred_element_type=F32)
               + jnp.dot(dmat.astype(BF16), v, preferred_element_type=F32))
        den = (e_inter * jnp.sum(q.astype(F32) * n_st, axis=1, keepdims=True)
               + jnp.sum(dmat, axis=1, keepdims=True))
        hdir_ref[pl.ds(r0, chunk), :] = num / jnp.maximum(jnp.abs(den), jnp.exp(-m))
        w_row = g - b_row + li
        m_loc = jnp.max(w_row, axis=1, keepdims=True)
        ew_col = to_col(jnp.exp(w_row - m_loc))
        kw = k.astype(F32) * ew_col
        c_loc = lax.dot_general(kw.astype(BF16), v, TN_DIMS, preferred_element_type=F32)
        n_loc = jnp.sum(kw, axis=0, keepdims=True)
        m_new = jnp.maximum(g + m_st, m_loc)
        a = jnp.exp(g + m_st - m_new)
        e = jnp.exp(m_loc - m_new)
        return a * c_st + e * c_loc, a * n_st + e * n_loc, m_new

    def body(ci, carry):
        fwd = chunk_step(ci, 0, 1, False, carry[0], hf_ref)
        bwd = chunk_step(n_chunks - 1 - ci, 2, 3, True, carry[1], hb_ref)
        return fwd, bwd

    init = (jnp.zeros((MLSTM_DQK, MLSTM_DV), F32), jnp.zeros((1, MLSTM_DQK), F32), jnp.zeros((1, 1), F32))
    lax.fori_loop(0, n_chunks, body, (init, init), unroll=2)

    h = hf_ref[...] + hb_ref[...]
    y = h * _rms_scale(h, MLSTM_DV) * onorm_ref[...]
    out_ref[0] = (jax.nn.sigmoid(o_ref[0]) * y).astype(out_ref.dtype)


def mlstm_core(p, conv_w, gates_rows, out_norm):
    b, s, _ = p.shape
    nh = MLSTM_HEADS
    k_off = nh * MLSTM_DQK // MLSTM_DQK
    v_off = 2 * nh * MLSTM_DQK // MLSTM_DV
    o_off = v_off + nh
    return pl.pallas_call(
        _mlstm_kernel,
        grid=(b, nh),
        in_specs=[
            pl.BlockSpec((1, s, MLSTM_DQK), lambda i, h: (i, 0, h)),
            pl.BlockSpec((1, s, MLSTM_DQK), lambda i, h: (i, 0, k_off + h)),
            pl.BlockSpec((1, s, MLSTM_DV), lambda i, h: (i, 0, v_off + h)),
            pl.BlockSpec((1, s, MLSTM_DV), lambda i, h: (i, 0, o_off + h)),
            pl.BlockSpec((3, MLSTM_DQK), lambda i, h: (0, h)),
            pl.BlockSpec((3, MLSTM_DQK), lambda i, h: (0, k_off + h)),
            pl.BlockSpec((1, 1, 4, s), lambda i, h: (i, h, 0, 0)),
            pl.BlockSpec((1, MLSTM_DV), lambda i, h: (0, h)),
        ],
        out_specs=pl.BlockSpec((1, s, MLSTM_DV), lambda i, h: (i, 0, h)),
        out_shape=jax.ShapeDtypeStruct((b, s, nh * MLSTM_DV), BF16),
        scratch_shapes=[
            pltpu.VMEM((s, MLSTM_DQK), BF16),
            pltpu.VMEM((s, MLSTM_DQK), BF16),
            pltpu.VMEM((s, MLSTM_DV), F32),
            pltpu.VMEM((s, MLSTM_DV), F32),
        ],
        compiler_params=_params("parallel", "parallel"),
        name="mlstm_core",
    )(p, p, p, p, conv_w, conv_w, gates_rows, out_norm.reshape(1, -1))


def mlstm_mixer(x2d, norm_g, w_in, conv_w, gate_b, out_norm, w_out, bsz, s):
    nh = MLSTM_HEADS
    main_cols = 2 * nh * MLSTM_DQK + 2 * nh * MLSTM_DV
    p = norm_matmul(x2d, norm_g, w_in[:, :main_cols].astype(BF16), kdim=x2d.shape[1], tm=512, tn=512)
    gates = norm_matmul(x2d, norm_g, w_in[:, main_cols:].astype(BF16), kdim=x2d.shape[1], tm=512, tn=4 * nh)
    gates = gates + gate_b.astype(F32)
    gates_rows = gates.reshape(bsz, s, 4, nh).transpose(0, 3, 2, 1)
    hg = mlstm_core(p.reshape(bsz, s, main_cols), conv_w, gates_rows, out_norm)
    return matmul_res(hg.reshape(bsz * s, nh * MLSTM_DV), w_out.astype(BF16), x2d, tm=512, tn=512)


MLA_TQ = 256


def _mla_kernel(qn_ref, qr_ref, kv_ref, kr_ref, cos_ref, sin_ref, gqn_ref, gqr_ref, gkn_ref, gkr_ref,
                out_ref, qs_ref, ks_ref, vs_ref):
    s = qn_ref.shape[1]
    dqk = MLA_NOPE + MLA_ROPE
    half = MLA_ROPE // 2
    cos = cos_ref[...]
    sin = sin_ref[...]
    lane = lax.broadcasted_iota(jnp.int32, (1, LANES), 1)
    first_half = (lane // half) % 2 == 0

    def rope(x):
        rot = jnp.where(first_half, -pltpu.roll(x, LANES - half, axis=1), pltpu.roll(x, half, axis=1))
        return x * cos + rot * sin

    qr_raw = qr_ref[0]
    kr_raw = kr_ref[0]
    q_rope = rope(qr_raw * gqr_ref[...])
    k_rope = rope(kr_raw * gkr_ref[...])
    kr_ss = 0.5 * jnp.sum(kr_raw * kr_raw, axis=-1, keepdims=True)
    for hh in range(2):
        own = (lane // MLA_ROPE) == hh
        qn = qn_ref[0, :, hh * MLA_NOPE:(hh + 1) * MLA_NOPE]
        kn = kv_ref[0, :, hh * 2 * LANES:hh * 2 * LANES + MLA_NOPE]
        vs_ref[...] = kv_ref[0, :, hh * 2 * LANES + MLA_NOPE:(hh + 1) * 2 * LANES].astype(BF16)
        q_ss = (jnp.sum(qn * qn, axis=-1, keepdims=True)
                + jnp.sum(jnp.where(own, qr_raw * qr_raw, 0.0), axis=-1, keepdims=True))
        rq = lax.rsqrt(q_ss * (1.0 / dqk) + EPS) * (dqk ** -0.5)
        rk = lax.rsqrt((jnp.sum(kn * kn, axis=-1, keepdims=True) + kr_ss) * (1.0 / dqk) + EPS)
        qs_ref[:, 0:LANES] = (qn * rq * gqn_ref[...]).astype(BF16)
        qs_ref[:, LANES:2 * LANES] = (jnp.where(own, q_rope, 0.0) * rq).astype(BF16)
        ks_ref[:, 0:LANES] = (kn * rk * gkn_ref[...]).astype(BF16)
        ks_ref[:, LANES:2 * LANES] = (k_rope * rk).astype(BF16)

        def qblock(i, _):
            r0 = pl.multiple_of(i * MLA_TQ, MLA_TQ)
            sc = lax.dot_general(qs_ref[pl.ds(r0, MLA_TQ), :], ks_ref[...], NT_DIMS, preferred_element_type=F32)
            mx = jnp.max(sc, axis=-1, keepdims=True)
            e = jnp.exp(sc - mx)
            den = jnp.sum(e, axis=-1, keepdims=True)
            o = jnp.dot(e.astype(BF16), vs_ref[...], preferred_element_type=F32) / den
            out_ref[0, pl.ds(r0, MLA_TQ), hh * MLA_DV:(hh + 1) * MLA_DV] = o.astype(out_ref.dtype)
            return 0

        lax.fori_loop(0, s // MLA_TQ, qblock, 0)


def mla_attention(qall, kv, kr, cos, sin, q_norm, k_norm):
    b, s, _ = qall.shape
    nh = MLA_HEADS
    qr_off = nh * MLA_NOPE // (2 * MLA_ROPE)
    row = lambda t: t.reshape(1, -1).astype(F32)
    twice = lambda t: jnp.concatenate([t, t], axis=-1)
    vec = pl.BlockSpec((1, LANES), lambda i, h: (0, 0))
    cos = jnp.tile(cos, (1, LANES // cos.shape[1]))
    sin = jnp.tile(sin, (1, LANES // sin.shape[1]))
    return pl.pallas_call(
        _mla_kernel,
        grid=(b, nh // 2),
        in_specs=[
            pl.BlockSpec((1, s, 2 * MLA_NOPE), lambda i, h: (i, 0, h)),
            pl.BlockSpec((1, s, 2 * MLA_ROPE), lambda i, h: (i, 0, qr_off + h)),
            pl.BlockSpec((1, s, 4 * LANES), lambda i, h: (i, 0, h)),
            pl.BlockSpec((1, s, LANES), lambda i, h: (i, 0, 0)),
            pl.BlockSpec((s, LANES), lambda i, h: (0, 0)),
            pl.BlockSpec((s, LANES), lambda i, h: (0, 0)),
            vec, vec, vec, vec,
        ],
        out_specs=pl.BlockSpec((1, s, 2 * MLA_DV), lambda i, h: (i, 0, h)),
        out_shape=jax.ShapeDtypeStruct((b, s, nh * MLA_DV), BF16),
        scratch_shapes=[
            pltpu.VMEM((s, 2 * LANES), BF16),
            pltpu.VMEM((s, 2 * LANES), BF16),
            pltpu.VMEM((s, MLA_DV), BF16),
        ],
        compiler_params=_params("parallel", "parallel"),
        name="mla_attention",
    )(qall, qall, kv, twice(kr), cos, sin,
      row(q_norm[:MLA_NOPE]), twice(row(q_norm[MLA_NOPE:])), row(k_norm[:MLA_NOPE]), twice(row(k_norm[MLA_NOPE:])))


def mla_mixer(x2d, norm_g, w_in, kv_norm, w_ukv, q_norm, k_norm, w_out, bsz, s):
    nh = MLA_HEADS
    dqk = MLA_NOPE + MLA_ROPE
    d = x2d.shape[1]
    w_q = w_in[:, :nh * dqk].reshape(d, nh, dqk)
    w_qcat = jnp.concatenate([w_q[:, :, :MLA_NOPE].reshape(d, nh * MLA_NOPE),
                              w_q[:, :, MLA_NOPE:].reshape(d, nh * MLA_ROPE)], axis=1).astype(BF16)
    qall = norm_matmul(x2d, norm_g, w_qcat, kdim=d, tm=512, tn=512)
    lat = norm_matmul(x2d, norm_g, w_in[:, nh * dqk:].astype(BF16), kdim=d, tm=512, tn=MLA_KV_RANK + MLA_ROPE)
    kv = norm_matmul(lat, kv_norm, w_ukv.astype(BF16), kdim=MLA_KV_RANK, tm=1024, tn=512)
    kr = lat[:, MLA_KV_RANK:]
    half = MLA_ROPE // 2
    inv = ROPE_BASE ** (-jnp.arange(half, dtype=F32) / half)
    ang = jnp.arange(s).astype(F32)[:, None] * inv[None, :]
    o = mla_attention(qall.reshape(bsz, s, -1), kv.reshape(bsz, s, -1), kr.reshape(bsz, s, MLA_ROPE),
                      jnp.cos(ang), jnp.sin(ang), q_norm, k_norm)
    return matmul_res(o.reshape(bsz * s, nh * MLA_DV), w_out.astype(BF16), x2d, tm=512, tn=512)


DIL_QB = 128
DIL_KW = 256
DIL_UNROLL = 4


def _dilated_kernel(*refs):
    n_g = len(DIL_GROUPS)
    qkv_refs = refs[:3 * n_g]
    gq_ref, gk_ref = refs[3 * n_g], refs[3 * n_g + 1]
    out_ref = refs[3 * n_g + 2]
    qs_ref, ks_ref, vs_ref = refs[3 * n_g + 3:3 * n_g + 6]
    acc_refs = refs[3 * n_g + 6:3 * n_g + 6 + n_g]
    m_refs = refs[3 * n_g + 6 + n_g:3 * n_g + 6 + 2 * n_g]
    l_refs = refs[3 * n_g + 6 + 2 * n_g:3 * n_g + 6 + 3 * n_g]
    s = out_ref.shape[1]
    head = pl.program_id(1)
    scale = DIL_DH ** -0.5

    for gi, (_, dil) in enumerate(DIL_GROUPS):
        q_ref, k_ref, v_ref = qkv_refs[3 * gi:3 * gi + 3]
        n = s // dil
        kw = min(DIL_KW, n)
        slope_arg = jnp.full((1, 1), -8.0 / (n_g * DIL_HEADS), F32) * (gi * DIL_HEADS + head + 1).astype(F32)
        bias_step = jnp.exp2(slope_arg) * float(dil)
        gq = gq_ref[gi:gi + 1, :]
        gk = gk_ref[gi:gi + 1, :]

        def residue(r, _, q_ref=q_ref, k_ref=k_ref, v_ref=v_ref, n=n, kw=kw, dil=dil, gi=gi,
                    bias_step=bias_step, gq=gq, gk=gk):
            def sub(ref):
                if dil == 1:
                    return ref[0]
                return ref[0, pl.ds(r, n, stride=dil), :]

            q = sub(q_ref)
            k = sub(k_ref)
            base = 0 if dil == 1 else pl.multiple_of(r * n, n)
            qs_ref[pl.ds(base, n), :] = (q * _rms_scale(q, DIL_DH) * gq * scale).astype(BF16)
            ks_ref[pl.ds(base, n), :] = (k * _rms_scale(k, DIL_DH) * gk).astype(BF16)
            vs_ref[pl.ds(base, n), :] = sub(v_ref).astype(BF16)
            for qb in range(n // DIL_QB):
                q0 = qb * DIL_QB
                k0 = min(max(q0 - DIL_SIDE, 0), n - kw)
                sc = lax.dot_general(qs_ref[pl.ds(base + q0, DIL_QB), :], ks_ref[pl.ds(base + k0, kw), :],
                                     NT_DIMS, preferred_element_type=F32)
                dist = jnp.abs((k0 + lax.broadcasted_iota(jnp.int32, (DIL_QB, kw), 1))
                               - (q0 + lax.broadcasted_iota(jnp.int32, (DIL_QB, kw), 0)))
                sc = sc - bias_step * dist.astype(F32)
                sc = jnp.where(dist <= DIL_SIDE, sc, -jnp.inf)
                mx = jnp.max(sc, axis=-1, keepdims=True)
                e = jnp.exp(sc - mx)
                den = jnp.sum(e, axis=-1, keepdims=True)
                acc = jnp.dot(e.astype(BF16), vs_ref[pl.ds(base + k0, kw), :], preferred_element_type=F32)
                if dil == 1:
                    rows = pl.ds(q0, DIL_QB)
                else:
                    rows = pl.ds(r + dil * q0, DIL_QB, stride=dil)
                acc_refs[gi][rows, :] = acc
                m_refs[gi][rows, :] = jnp.broadcast_to(mx, (DIL_QB, DIL_DH))
                l_refs[gi][rows, :] = jnp.broadcast_to(den, (DIL_QB, DIL_DH))
            return 0

        if dil == 1:
            residue(0, 0)
        else:
            lax.fori_loop(0, dil, residue, 0, unroll=min(dil, DIL_UNROLL))

    m_all = m_refs[0][...]
    for gi in range(1, n_g):
        m_all = jnp.maximum(m_all, m_refs[gi][...])
    num = jnp.zeros((s, DIL_DH), F32)
    den = jnp.zeros((s, DIL_DH), F32)
    for gi in range(n_g):
        w = jnp.exp(m_refs[gi][...] - m_all)
        num = num + acc_refs[gi][...] * w
        den = den + l_refs[gi][...] * w
    out_ref[0] = (num / den).astype(out_ref.dtype)


def dilated_attention(p, q_norm, k_norm):
    b, s, _ = p.shape
    n_g = len(DIL_GROUPS)
    nh = DIL_HEADS

    def spec(col0):
        return pl.BlockSpec((1, s, DIL_DH), lambda i, h: (i, 0, col0 + h))

    in_specs = [spec((gi * 3 + which) * nh) for gi in range(n_g) for which in range(3)]
    in_specs += [pl.BlockSpec((n_g, DIL_DH), lambda i, h: (0, 0))] * 2
    scratch = [pltpu.VMEM((s, DIL_DH), BF16)] * 3 + [pltpu.VMEM((s, DIL_DH), F32)] * (3 * n_g)
    return pl.pallas_call(
        _dilated_kernel,
        grid=(b, nh),
        in_specs=in_specs,
        out_specs=pl.BlockSpec((1, s, DIL_DH), lambda i, h: (i, 0, h)),
        out_shape=jax.ShapeDtypeStruct((b, s, nh * DIL_DH), BF16),
        scratch_shapes=scratch,
        compiler_params=_params("parallel", "parallel"),
        name="dilated_attention",
    )(*([p] * (3 * n_g)), q_norm.astype(F32), k_norm.astype(F32))


def dilated_mixer(x2d, norm_g, w_in, q_norm, k_norm, w_out, bsz, s):
    p = norm_matmul(x2d, norm_g, w_in.astype(BF16), kdim=x2d.shape[1], tm=512, tn=512)
    o = dilated_attention(p.reshape(bsz, s, -1), q_norm, k_norm)
    return matmul_res(o.reshape(bsz * s, -1), w_out.astype(BF16), x2d, tm=512, tn=512)


def _router_kernel(x_ref, g_ref, rw_ref, rb_ref, idx_ref, wts_ref):
    x = x_ref[...]
    xn = x * _rms_scale(x, x.shape[-1]) * g_ref[...]
    logits = lax.dot_general(rw_ref[...], xn, NT_DIMS, precision=lax.Precision.HIGHEST,
                             preferred_element_type=F32)
    scores = jax.nn.sigmoid(logits)
    sel = scores + rb_ref[...]
    ng = N_GROUPS
    slots = range(EXPERTS_PER_GROUP)
    v = [sel[j * ng:(j + 1) * ng, :] for j in slots]
    sc = [scores[j * ng:(j + 1) * ng, :] for j in slots]

    def first_argmax(vals):
        best = vals[0]
        for t in vals[1:]:
            best = jnp.maximum(best, t)
        idx = jnp.full(best.shape, len(vals) - 1, jnp.int32)
        for j in reversed(range(len(vals) - 1)):
            idx = jnp.where(vals[j] == best, j, idx)
        return best, idx

    m1, i1 = first_argmax(v)
    m2, i2 = first_argmax([jnp.where(i1 == j, -jnp.inf, v[j]) for j in slots])
    grp_score = m1 + m2
    grp_iota = lax.broadcasted_iota(jnp.int32, grp_score.shape, 0)
    best = jnp.max(grp_score, axis=0, keepdims=True)
    grp = jnp.min(jnp.where(grp_score == best, grp_iota, ng), axis=0, keepdims=True)
    w1 = sum(jnp.where(i1 == j, sc[j], 0.0) for j in slots)
    w2 = sum(jnp.where(i2 == j, sc[j], 0.0) for j in slots)
    tot = w1 + w2
    chosen = grp_iota == grp

    def pick(v):
        return jnp.sum(jnp.where(chosen, v, jnp.zeros_like(v)), axis=0, keepdims=True)

    idx_ref[0:1, :] = grp * EXPERTS_PER_GROUP + pick(i1)
    idx_ref[1:2, :] = grp * EXPERTS_PER_GROUP + pick(i2)
    wts_ref[0:1, :] = pick(w1 / tot)
    wts_ref[1:2, :] = pick(w2 / tot)


def router_topk(x2d, norm_g, router_w, router_b, *, tm):
    t, d = x2d.shape
    ne = N_EXPERTS
    perm = jnp.arange(ne).reshape(N_GROUPS, EXPERTS_PER_GROUP).T.reshape(-1)
    rw_t = router_w.astype(F32).T[perm]
    rb = router_b.astype(F32)[perm].reshape(ne, 1)
    return pl.pallas_call(
        _router_kernel,
        grid=(t // tm,),
        in_specs=[
            pl.BlockSpec((tm, d), lambda i: (i, 0)),
            pl.BlockSpec((1, d), lambda i: (0, 0)),
            pl.BlockSpec((ne, d), lambda i: (0, 0)),
            pl.BlockSpec((ne, 1), lambda i: (0, 0)),
        ],
        out_specs=[pl.BlockSpec((TOP_K, tm), lambda i: (0, i)), pl.BlockSpec((TOP_K, tm), lambda i: (0, i))],
        out_shape=[jax.ShapeDtypeStruct((TOP_K, t), jnp.int32), jax.ShapeDtypeStruct((TOP_K, t), F32)],
        compiler_params=_params("parallel"),
        name="router",
    )(x2d, norm_g.reshape(1, d), rw_t, rb)


def moe_plan(idx, tm):
    ne = N_EXPERTS
    e_flat = idx.reshape(-1)
    n_tiles = e_flat.shape[0] // tm + ne
    onehot = (e_flat[:, None] == jnp.arange(ne, dtype=jnp.int32)[None, :]).astype(jnp.int32)
    csum = jnp.cumsum(onehot, axis=0)
    counts = csum[-1]
    padded = (counts + tm - 1) // tm * tm
    ends = jnp.cumsum(padded)
    starts = ends - padded
    pos = jnp.sum(onehot * (csum - 1 + starts[None, :]), axis=1)
    tile_row0 = jnp.arange(n_tiles, dtype=jnp.int32) * tm
    tile_expert = jnp.minimum(jnp.sum((tile_row0[:, None] >= ends[None, :]).astype(jnp.int32), axis=1), ne - 1)
    n_used = (ends[-1] // tm).reshape(1)
    return pos.astype(jnp.int32), tile_expert.astype(jnp.int32), n_used.astype(jnp.int32), n_tiles


def _moe_expert_kernel(pos_ref, te_ref, nused_ref, x_hbm, g_ref, wg_ref, wu_ref, wd_ref, ys_ref,
                       src_ref, xbuf, sem, wg_bf, wu_bf, wd_bf):
    i = pl.program_id(0)
    tm = xbuf.shape[1]
    n_used = nused_ref[0]
    n_assign = pos_ref.shape[0]
    n_tok = n_assign // TOP_K

    def row_copy(tok, slot, r):
        return pltpu.make_async_copy(x_hbm.at[pl.ds(tok, 1), :], xbuf.at[slot, pl.ds(r, 1), :], sem.at[slot])

    def issue(tile, slot):
        def body(r, _):
            row_copy(src_ref[tile * tm + r], slot, r).start()
            return 0
        lax.fori_loop(0, tm, body, 0, unroll=DMA_UNROLL)

    def wait(slot):
        pltpu.make_async_copy(x_hbm.at[pl.ds(0, tm), :], xbuf.at[slot], sem.at[slot]).wait()

    @pl.when(i == 0)
    def _():
        def clear(r, _):
            src_ref[r] = 0
            return 0
        lax.fori_loop(0, src_ref.shape[0], clear, 0, unroll=DMA_UNROLL)

        def scatter(n, _):
            src_ref[pos_ref[n]] = jnp.where(n >= n_tok, n - n_tok, n)
            return 0
        lax.fori_loop(0, n_assign, scatter, 0, unroll=DMA_UNROLL)

        @pl.when(n_used > 0)
        def _():
            issue(0, 0)

    slot = i % 2

    @pl.when(i < n_used)
    def _():
        wait(slot)

        @pl.when(i + 1 < n_used)
        def _():
            issue(i + 1, 1 - slot)

        @pl.when((i == 0) | (te_ref[i] != te_ref[jnp.maximum(i - 1, 0)]))
        def _():
            wg_bf[...] = wg_ref[0].astype(BF16)
            wu_bf[...] = wu_ref[0].astype(BF16)
            wd_bf[...] = wd_ref[0].astype(BF16)

        x = xbuf[slot]
        xn = (x * _rms_scale(x, x.shape[-1]) * g_ref[...]).astype(BF16)
        hg = jnp.dot(xn, wg_bf[...], preferred_element_type=F32)
        hu = jnp.dot(xn, wu_bf[...], preferred_element_type=F32)
        a = hg * jax.nn.sigmoid(hg) * hu
        ys_ref[...] = jnp.dot(a.astype(BF16), wd_bf[...], preferred_element_type=F32)

    @pl.when(i >= n_used)
    def _():
        ys_ref[...] = jnp.zeros_like(ys_ref)


def moe_experts(x2d, norm_g, pos, tile_expert, n_used, n_tiles, w_gate, w_up, w_down, *, tm):
    t, d = x2d.shape
    ne, _, f = w_gate.shape
    grid_spec = pltpu.PrefetchScalarGridSpec(
        num_scalar_prefetch=3,
        grid=(n_tiles,),
        in_specs=[
            pl.BlockSpec(memory_space=pl.ANY),
            pl.BlockSpec((1, d), lambda i, pos, te, nu: (0, 0)),
            pl.BlockSpec((1, d, f), lambda i, pos, te, nu: (te[i], 0, 0)),
            pl.BlockSpec((1, d, f), lambda i, pos, te, nu: (te[i], 0, 0)),
            pl.BlockSpec((1, f, d), lambda i, pos, te, nu: (te[i], 0, 0)),
        ],
        out_specs=pl.BlockSpec((tm, d), lambda i, pos, te, nu: (i, 0)),
        scratch_shapes=[
            pltpu.SMEM((n_tiles * tm,), jnp.int32),
            pltpu.VMEM((2, tm, d), F32),
            pltpu.SemaphoreType.DMA((2,)),
            pltpu.VMEM((d, f), BF16),
            pltpu.VMEM((d, f), BF16),
            pltpu.VMEM((f, d), BF16),
        ],
    )
    return pl.pallas_call(
        _moe_expert_kernel,
        grid_spec=grid_spec,
        out_shape=jax.ShapeDtypeStruct((n_tiles * tm, d), F32),
        compiler_params=_params("arbitrary"),
        name="moe_experts",
    )(pos, tile_expert, n_used, x2d, norm_g.reshape(1, d), w_gate, w_up, w_down)


def _moe_combine_kernel(pos_ref, x_ref, w_ref, ys_hbm, o_ref, buf, sem):
    i = pl.program_id(0)
    tc = x_ref.shape[0]
    n_tok = pos_ref.shape[0] // TOP_K

    def row_copy(p, slot, k, r):
        return pltpu.make_async_copy(ys_hbm.at[pl.ds(p, 1), :], buf.at[slot, k, pl.ds(r, 1), :], sem.at[slot])

    def issue(tile, slot):
        def body(r, _):
            for k in range(TOP_K):
                row_copy(pos_ref[k * n_tok + tile * tc + r], slot, k, r).start()
            return 0
        lax.fori_loop(0, tc, body, 0, unroll=DMA_UNROLL)

    def wait(slot):
        for k in range(TOP_K):
            pltpu.make_async_copy(ys_hbm.at[pl.ds(0, tc), :], buf.at[slot, k], sem.at[slot]).wait()

    @pl.when(i == 0)
    def _():
        issue(0, 0)

    slot = i % 2
    wait(slot)

    @pl.when(i + 1 < pl.num_programs(0))
    def _():
        issue(i + 1, 1 - slot)

    w = w_ref[...]
    o_ref[...] = x_ref[...] + w[:, 0:1] * buf[slot, 0] + w[:, 1:2] * buf[slot, 1]


def moe_combine(x2d, wts_tk, pos, ys, *, tc):
    t, d = x2d.shape
    grid_spec = pltpu.PrefetchScalarGridSpec(
        num_scalar_prefetch=1,
        grid=(t // tc,),
        in_specs=[
            pl.BlockSpec((tc, d), lambda i, pos: (i, 0)),
            pl.BlockSpec((tc, TOP_K), lambda i, pos: (i, 0)),
            pl.BlockSpec(memory_space=pl.ANY),
        ],
        out_specs=pl.BlockSpec((tc, d), lambda i, pos: (i, 0)),
        scratch_shapes=[pltpu.VMEM((2, TOP_K, tc, d), F32), pltpu.SemaphoreType.DMA((2,))],
    )
    return pl.pallas_call(
        _moe_combine_kernel,
        grid_spec=grid_spec,
        out_shape=jax.ShapeDtypeStruct((t, d), F32),
        compiler_params=_params("arbitrary"),
        name="moe_combine",
    )(pos, x2d, wts_tk, ys)


def moe_ffn(x2d, norm_g, router_w, router_b, w_gate, w_up, w_down):
    idx, wts = router_topk(x2d, norm_g, router_w, router_b, tm=512)
    pos, tile_expert, n_used, n_tiles = moe_plan(idx, MOE_TM)
    ys = moe_experts(x2d, norm_g, pos, tile_expert, n_used, n_tiles, w_gate, w_up, w_down, tm=MOE_TM)
    return moe_combine(x2d, wts.T, pos, ys, tc=MOE_TC)


def kernel(x, norm_mix, norm_ffn, a_w_in, a_conv_w, a_gate_b, a_out_norm, a_w_out, b_w_in, b_kv_norm, b_w_ukv, b_q_norm, b_k_norm, b_w_out, c_w_in, c_q_norm, c_k_norm, c_w_out, router_w, router_b, moe_w_gate, moe_w_up, moe_w_down):
    bsz, s, d = x.shape
    depth = norm_mix.shape[0]
    x2d = x.reshape(bsz * s, d)
    for i in range(depth):
        kind = i % 3
        j = i // 3
        if kind == 0:
            x2d = mlstm_mixer(x2d, norm_mix[i], a_w_in[j], a_conv_w[j], a_gate_b[j], a_out_norm[j], a_w_out[j],
                              bsz, s)
        elif kind == 1:
            x2d = mla_mixer(x2d, norm_mix[i], b_w_in[j], b_kv_norm[j], b_w_ukv[j], b_q_norm[j], b_k_norm[j],
                            b_w_out[j], bsz, s)
        else:
            x2d = dilated_mixer(x2d, norm_mix[i], c_w_in[j], c_q_norm[j], c_k_norm[j], c_w_out[j], bsz, s)
        x2d = moe_ffn(x2d, norm_ffn[i], router_w, router_b, moe_w_gate[i], moe_w_up[i], moe_w_down[i])
    return x2d.reshape(bsz, s, d)
```

```python
import functools
import math

import jax
import jax.numpy as jnp
from jax import lax
from jax.experimental import pallas as pl
from jax.experimental.pallas import tpu as pltpu

F32 = jnp.float32
BF16 = jnp.bfloat16

EPS = 1e-6
LANES = 128
VMEM_LIMIT_BYTES = 56 * 1024 * 1024
MM_TM = 1024
MM_TN = 1024

MLSTM_HEADS = 8
MLSTM_DQK = 128
MLSTM_DV = 256
MLSTM_CHUNK = 128

MLA_HEADS = 16
MLA_NOPE = 128
MLA_ROPE = 64
MLA_DV = 128
MLA_KV_RANK = 512
ROPE_BASE = 10000.0

DIL_GROUPS = ((128, 1), (512, 4), (2048, 16))
DIL_HEADS = 8
DIL_DH = 128
DIL_SIDE = 64

N_EXPERTS = 16
N_GROUPS = 4
EXPERTS_PER_GROUP = 4
TOP_K = 2
MOE_TM = 512
MOE_TC = 256
DMA_UNROLL = 8

NT_DIMS = (((1,), (1,)), ((), ()))
TN_DIMS = (((0,), (0,)), ((), ()))


def _params(*sem):
    return pltpu.CompilerParams(dimension_semantics=sem, vmem_limit_bytes=VMEM_LIMIT_BYTES)


def _rms_scale(x, width):
    return lax.rsqrt(jnp.sum(x * x, axis=-1, keepdims=True) * (1.0 / width) + EPS)


def _norm_matmul_kernel(x_ref, g_ref, w_ref, o_ref, xn_ref):
    @pl.when(pl.program_id(1) == 0)
    def _():
        x = x_ref[...].astype(F32)
        xn_ref[...] = (x * _rms_scale(x, x.shape[-1]) * g_ref[...]).astype(BF16)

    o_ref[...] = jnp.dot(xn_ref[...], w_ref[...], preferred_element_type=F32).astype(o_ref.dtype)


def norm_matmul(x, g, w, *, kdim, tm, tn, out_dtype=F32):
    t = x.shape[0]
    n = w.shape[1]
    return pl.pallas_call(
        _norm_matmul_kernel,
        grid=(t // tm, n // tn),
        in_specs=[
            pl.BlockSpec((tm, kdim), lambda i, j: (i, 0)),
            pl.BlockSpec((1, kdim), lambda i, j: (0, 0)),
            pl.BlockSpec((kdim, tn), lambda i, j: (0, j)),
        ],
        out_specs=pl.BlockSpec((tm, tn), lambda i, j: (i, j)),
        out_shape=jax.ShapeDtypeStruct((t, n), out_dtype),
        scratch_shapes=[pltpu.VMEM((tm, kdim), BF16)],
        compiler_params=_params("parallel", "arbitrary"),
        name="norm_matmul",
    )(x, g.reshape(1, kdim), w)


def _matmul_res_kernel(a_ref, w_ref, r_ref, o_ref):
    o_ref[...] = r_ref[...] + jnp.dot(a_ref[...].astype(BF16), w_ref[...], preferred_element_type=F32)


def matmul_res(a, w, res, *, tm, tn):
    t, k = a.shape
    n = w.shape[1]
    return pl.pallas_call(
        _matmul_res_kernel,
        grid=(t // tm, n // tn),
        in_specs=[
            pl.BlockSpec((tm, k), lambda i, j: (i, 0)),
            pl.BlockSpec((k, tn), lambda i, j: (0, j)),
            pl.BlockSpec((tm, tn), lambda i, j: (i, j)),
        ],
        out_specs=pl.BlockSpec((tm, tn), lambda i, j: (i, j)),
        out_shape=jax.ShapeDtypeStruct((t, n), F32),
        compiler_params=_params("parallel", "arbitrary"),
        name="matmul_res",
    )(a, w, res)


def _log_sigmoid(x):
    return jnp.minimum(x, 0.0) - jnp.log(1.0 + jnp.exp(-jnp.abs(x)))


def _mlstm_kernel(q_ref, k_ref, v_ref, o_ref, cwq_ref, cwk_ref, gates_ref, onorm_ref, out_ref,
                  qs_ref, ks_ref, hf_ref, hb_ref):
    s = q_ref.shape[1]
    chunk = MLSTM_CHUNK
    n_chunks = s // chunk
    row = lax.broadcasted_iota(jnp.int32, (s, 1), 0)

    def conv_silu(p, w):
        prev = jnp.where(row == 0, 0.0, pltpu.roll(p, 1, axis=0))
        nxt = jnp.where(row == s - 1, 0.0, pltpu.roll(p, s - 1, axis=0))
        y = w[0:1, :] * prev + w[1:2, :] * p + w[2:3, :] * nxt
        return y * jax.nn.sigmoid(y)

    qs_ref[...] = (conv_silu(q_ref[0], cwq_ref[...]) * (MLSTM_DQK ** -0.5)).astype(BF16)
    ks_ref[...] = conv_silu(k_ref[0], cwk_ref[...]).astype(BF16)

    ti = lax.broadcasted_iota(jnp.int32, (chunk, chunk), 0)
    si = lax.broadcasted_iota(jnp.int32, (chunk, chunk), 1)
    eye = ti == si

    def to_col(r):
        return jnp.sum(jnp.where(eye, jnp.broadcast_to(r, (chunk, chunk)), 0.0), axis=1, keepdims=True)

    def to_row(c):
        return jnp.sum(jnp.where(eye, jnp.broadcast_to(c, (chunk, chunk)), 0.0), axis=0, keepdims=True)

    def chunk_step(c, i_idx, f_idx, reverse, carry, hdir_ref):
        mask = (si >= ti) if reverse else (si <= ti)
        c_st, n_st, m_st = carry
        r0 = pl.multiple_of(c * chunk, chunk)
        q = qs_ref[pl.ds(r0, chunk), :]
        k = ks_ref[pl.ds(r0, chunk), :]
        v = v_ref[0, pl.ds(r0, chunk), :].astype(BF16)
        li = gates_ref[0, 0, i_idx:i_idx + 1, pl.ds(r0, chunk)]
        lf = _log_sigmoid(gates_ref[0, 0, f_idx:f_idx + 1, pl.ds(r0, chunk)])
        b_col = jnp.sum(jnp.where(mask, jnp.broadcast_to(lf, (chunk, chunk)), 0.0), axis=1, keepdims=True)
        b_row = to_row(b_col)
        g = jnp.sum(lf, axis=1, keepdims=True)
        dlog = jnp.where(mask, b_col - b_row + li, -jnp.inf)
        m_intra = jnp.max(dlog, axis=1, keepdims=True)
        a_inter = b_col + m_st
        m = jnp.maximum(a_inter, m_intra)
        e_inter = jnp.exp(a_inter - m)
        s_qk = lax.dot_general(q, k, NT_DIMS, preferred_element_type=F32)
        dmat = jnp.exp(dlog - m) * s_qk
        num = (e_inter * jnp.dot(q, c_st.astype(BF16), preferred_element_type=F32)
               + jnp.dot(dmat.astype(BF16), v, preferred_element_type=F32))
        den = (e_inter * jnp.sum(q.astype(F32) * n_st, axis=1, keepdims=True)
               + jnp.sum(dmat, axis=1, keepdims=True))
        hdir_ref[pl.ds(r0, chunk), :] = num / jnp.maximum(jnp.abs(den), jnp.exp(-m))
        w_row = g - b_row + li
        m_loc = jnp.max(w_row, axis=1, keepdims=True)
        ew_col = to_col(jnp.exp(w_row - m_loc))
        kw = k.astype(F32) * ew_col
        c_loc = lax.dot_general(kw.astype(BF16), v, TN_DIMS, preferred_element_type=F32)
        n_loc = jnp.sum(kw, axis=0, keepdims=True)
        m_new = jnp.maximum(g + m_st, m_loc)
        a = jnp.exp(g + m_st - m_new)
        e = jnp.exp(m_loc - m_new)
        return a * c_st + e * c_loc, a * n_st + e * n_loc, m_new

    def body(ci, carry):
        fwd = chunk_step(ci, 0, 1, False, carry[0], hf_ref)
        bwd = chunk_step(n_chunks - 1 - ci, 2, 3, True, carry[1], hb_ref)
        return fwd, bwd

    init = (jnp.zeros((MLSTM_DQK, MLSTM_DV), F32), jnp.zeros((1, MLSTM_DQK), F32), jnp.zeros((1, 1), F32))
    lax.fori_loop(0, n_chunks, body, (init, init), unroll=2)

    h = hf_ref[...] + hb_ref[...]
    y = h * _rms_scale(h, MLSTM_DV) * onorm_ref[...]
    out_ref[0] = (jax.nn.sigmoid(o_ref[0]) * y).astype(out_ref.dtype)


def mlstm_core(p, conv_w, gates_rows, out_norm):
    b, s, _ = p.shape
    nh = MLSTM_HEADS
    k_off = nh * MLSTM_DQK // MLSTM_DQK
    v_off = 2 * nh * MLSTM_DQK // MLSTM_DV
    o_off = v_off + nh
    return pl.pallas_call(
        _mlstm_kernel,
        grid=(b, nh),
        in_specs=[
            pl.BlockSpec((1, s, MLSTM_DQK), lambda i, h: (i, 0, h)),
            pl.BlockSpec((1, s, MLSTM_DQK), lambda i, h: (i, 0, k_off + h)),
            pl.BlockSpec((1, s, MLSTM_DV), lambda i, h: (i, 0, v_off + h)),
            pl.BlockSpec((1, s, MLSTM_DV), lambda i, h: (i, 0, o_off + h)),
            pl.BlockSpec((3, MLSTM_DQK), lambda i, h: (0, h)),
            pl.BlockSpec((3, MLSTM_DQK), lambda i, h: (0, k_off + h)),
            pl.BlockSpec((1, 1, 4, s), lambda i, h: (i, h, 0, 0)),
            pl.BlockSpec((1, MLSTM_DV), lambda i, h: (0, h)),
        ],
        out_specs=pl.BlockSpec((1, s, MLSTM_DV), lambda i, h: (i, 0, h)),
        out_shape=jax.ShapeDtypeStruct((b, s, nh * MLSTM_DV), BF16),
        scratch_shapes=[
            pltpu.VMEM((s, MLSTM_DQK), BF16),
            pltpu.VMEM((s, MLSTM_DQK), BF16),
            pltpu.VMEM((s, MLSTM_DV), F32),
            pltpu.VMEM((s, MLSTM_DV), F32),
        ],
        compiler_params=_params("parallel", "parallel"),
        name="mlstm_core",
    )(p, p, p, p, conv_w, conv_w, gates_rows, out_norm.reshape(1, -1))


def mlstm_mixer(x2d, norm_g, w_in, conv_w, gate_b, out_norm, w_out, bsz, s):
    nh = MLSTM_HEADS
    main_cols = 2 * nh * MLSTM_DQK + 2 * nh * MLSTM_DV
    p = norm_matmul(x2d, norm_g, w_in[:, :main_cols].astype(BF16), kdim=x2d.shape[1], tm=MM_TM, tn=MM_TN)
    gates = norm_matmul(x2d, norm_g, w_in[:, main_cols:].astype(BF16), kdim=x2d.shape[1], tm=MM_TM, tn=4 * nh)
    gates = gates + gate_b.astype(F32)
    gates_rows = gates.reshape(bsz, s, 4, nh).transpose(0, 3, 2, 1)
    hg = mlstm_core(p.reshape(bsz, s, main_cols), conv_w, gates_rows, out_norm)
    return matmul_res(hg.reshape(bsz * s, nh * MLSTM_DV), w_out.astype(BF16), x2d, tm=MM_TM, tn=MM_TN)


MLA_TQ = 256


def _mla_kernel(qn_ref, qr_ref, kv_ref, kr_ref, cos_ref, sin_ref, gqn_ref, gqr_ref, gkn_ref, gkr_ref,
                out_ref, qs_ref, ks_ref, vs_ref):
    s = qn_ref.shape[1]
    dqk = MLA_NOPE + MLA_ROPE
    half = MLA_ROPE // 2
    cos = cos_ref[...]
    sin = sin_ref[...]
    lane = lax.broadcasted_iota(jnp.int32, (1, LANES), 1)
    first_half = (lane // half) % 2 == 0

    def rope(x):
        rot = jnp.where(first_half, -pltpu.roll(x, LANES - half, axis=1), pltpu.roll(x, half, axis=1))
        return x * cos + rot * sin

    qr_raw = qr_ref[0]
    kr_raw = kr_ref[0]
    q_rope = rope(qr_raw * gqr_ref[...])
    k_rope = rope(kr_raw * gkr_ref[...])
    kr_ss = 0.5 * jnp.sum(kr_raw * kr_raw, axis=-1, keepdims=True)
    for hh in range(2):
        own = (lane // MLA_ROPE) == hh
        qn = qn_ref[0, :, hh * MLA_NOPE:(hh + 1) * MLA_NOPE]
        kn = kv_ref[0, :, hh * 2 * LANES:hh * 2 * LANES + MLA_NOPE]
        vs_ref[...] = kv_ref[0, :, hh * 2 * LANES + MLA_NOPE:(hh + 1) * 2 * LANES].astype(BF16)
        q_ss = (jnp.sum(qn * qn, axis=-1, keepdims=True)
                + jnp.sum(jnp.where(own, qr_raw * qr_raw, 0.0), axis=-1, keepdims=True))
        rq = lax.rsqrt(q_ss * (1.0 / dqk) + EPS) * (dqk ** -0.5)
        rk = lax.rsqrt((jnp.sum(kn * kn, axis=-1, keepdims=True) + kr_ss) * (1.0 / dqk) + EPS)
        qs_ref[:, 0:LANES] = (qn * rq * gqn_ref[...]).astype(BF16)
        qs_ref[:, LANES:2 * LANES] = (jnp.where(own, q_rope, 0.0) * rq).astype(BF16)
        ks_ref[:, 0:LANES] = (kn * rk * gkn_ref[...]).astype(BF16)
        ks_ref[:, LANES:2 * LANES] = (k_rope * rk).astype(BF16)

        def qblock(i, _):
            r0 = pl.multiple_of(i * MLA_TQ, MLA_TQ)
            sc = lax.dot_general(qs_ref[pl.ds(r0, MLA_TQ), :], ks_ref[...], NT_DIMS, preferred_element_type=F32)
            mx = jnp.max(sc, axis=-1, keepdims=True)
            e = jnp.exp(sc - mx)
            den = jnp.sum(e, axis=-1, keepdims=True)
            o = jnp.dot(e.astype(BF16), vs_ref[...], preferred_element_type=F32) / den
            out_ref[0, pl.ds(r0, MLA_TQ), hh * MLA_DV:(hh + 1) * MLA_DV] = o.astype(out_ref.dtype)
            return 0

        lax.fori_loop(0, s // MLA_TQ, qblock, 0, unroll=2)


def mla_attention(qall, kv, kr, cos, sin, q_norm, k_norm):
    b, s, _ = qall.shape
    nh = MLA_HEADS
    qr_off = nh * MLA_NOPE // (2 * MLA_ROPE)
    row = lambda t: t.reshape(1, -1).astype(F32)
    twice = lambda t: jnp.concatenate([t, t], axis=-1)
    vec = pl.BlockSpec((1, LANES), lambda i, h: (0, 0))
    cos = jnp.tile(cos, (1, LANES // cos.shape[1]))
    sin = jnp.tile(sin, (1, LANES // sin.shape[1]))
    return pl.pallas_call(
        _mla_kernel,
        grid=(b, nh // 2),
        in_specs=[
            pl.BlockSpec((1, s, 2 * MLA_NOPE), lambda i, h: (i, 0, h)),
            pl.BlockSpec((1, s, 2 * MLA_ROPE), lambda i, h: (i, 0, qr_off + h)),
            pl.BlockSpec((1, s, 4 * LANES), lambda i, h: (i, 0, h)),
            pl.BlockSpec((1, s, LANES), lambda i, h: (i, 0, 0)),
            pl.BlockSpec((s, LANES), lambda i, h: (0, 0)),
            pl.BlockSpec((s, LANES), lambda i, h: (0, 0)),
            vec, vec, vec, vec,
        ],
        out_specs=pl.BlockSpec((1, s, 2 * MLA_DV), lambda i, h: (i, 0, h)),
        out_shape=jax.ShapeDtypeStruct((b, s, nh * MLA_DV), BF16),
        scratch_shapes=[
            pltpu.VMEM((s, 2 * LANES), BF16),
            pltpu.VMEM((s, 2 * LANES), BF16),
            pltpu.VMEM((s, MLA_DV), BF16),
        ],
        compiler_params=_params("parallel", "parallel"),
        name="mla_attention",
    )(qall, qall, kv, twice(kr), cos, sin,
      row(q_norm[:MLA_NOPE]), twice(row(q_norm[MLA_NOPE:])), row(k_norm[:MLA_NOPE]), twice(row(k_norm[MLA_NOPE:])))


def mla_mixer(x2d, norm_g, w_in, kv_norm, w_ukv, q_norm, k_norm, w_out, bsz, s):
    nh = MLA_HEADS
    dqk = MLA_NOPE + MLA_ROPE
    d = x2d.shape[1]
    w_q = w_in[:, :nh * dqk].reshape(d, nh, dqk)
    w_qcat = jnp.concatenate([w_q[:, :, :MLA_NOPE].reshape(d, nh * MLA_NOPE),
                              w_q[:, :, MLA_NOPE:].reshape(d, nh * MLA_ROPE)], axis=1).astype(BF16)
    qall = norm_matmul(x2d, norm_g, w_qcat, kdim=d, tm=MM_TM, tn=MM_TN)
    lat = norm_matmul(x2d, norm_g, w_in[:, nh * dqk:].astype(BF16), kdim=d, tm=MM_TM, tn=MLA_KV_RANK + MLA_ROPE)
    kv = norm_matmul(lat, kv_norm, w_ukv.astype(BF16), kdim=MLA_KV_RANK, tm=MM_TM, tn=MM_TN)
    kr = lat[:, MLA_KV_RANK:]
    half = MLA_ROPE // 2
    inv = ROPE_BASE ** (-jnp.arange(half, dtype=F32) / half)
    ang = jnp.arange(s).astype(F32)[:, None] * inv[None, :]
    o = mla_attention(qall.reshape(bsz, s, -1), kv.reshape(bsz, s, -1), kr.reshape(bsz, s, MLA_ROPE),
                      jnp.cos(ang), jnp.sin(ang), q_norm, k_norm)
    return matmul_res(o.reshape(bsz * s, nh * MLA_DV), w_out.astype(BF16), x2d, tm=MM_TM, tn=MM_TN)


DIL_QB = 128
DIL_KW = 256
DIL_UNROLL = 4


def _dilated_kernel(*refs):
    n_g = len(DIL_GROUPS)
    qkv_refs = refs[:3 * n_g]
    gq_ref, gk_ref = refs[3 * n_g], refs[3 * n_g + 1]
    out_ref = refs[3 * n_g + 2]
    qs_ref, ks_ref, vs_ref = refs[3 * n_g + 3:3 * n_g + 6]
    acc_refs = refs[3 * n_g + 6:3 * n_g + 6 + n_g]
    m_refs = refs[3 * n_g + 6 + n_g:3 * n_g + 6 + 2 * n_g]
    l_refs = refs[3 * n_g + 6 + 2 * n_g:3 * n_g + 6 + 3 * n_g]
    s = out_ref.shape[1]
    head = pl.program_id(1)
    scale = DIL_DH ** -0.5

    for gi, (_, dil) in enumerate(DIL_GROUPS):
        q_ref, k_ref, v_ref = qkv_refs[3 * gi:3 * gi + 3]
        n = s // dil
        kw = min(DIL_KW, n)
        slope_arg = jnp.full((1, 1), -8.0 / (n_g * DIL_HEADS), F32) * (gi * DIL_HEADS + head + 1).astype(F32)
        bias_step = jnp.exp2(slope_arg) * float(dil)
        gq = gq_ref[gi:gi + 1, :]
        gk = gk_ref[gi:gi + 1, :]

        def residue(r, _, q_ref=q_ref, k_ref=k_ref, v_ref=v_ref, n=n, kw=kw, dil=dil, gi=gi,
                    bias_step=bias_step, gq=gq, gk=gk):
            def sub(ref):
                if dil == 1:
                    return ref[0]
                return ref[0, pl.ds(r, n, stride=dil), :]

            q = sub(q_ref)
            k = sub(k_ref)
            base = 0 if dil == 1 else pl.multiple_of(r * n, n)
            qs_ref[pl.ds(base, n), :] = (q * _rms_scale(q, DIL_DH) * gq * scale).astype(BF16)
            ks_ref[pl.ds(base, n), :] = (k * _rms_scale(k, DIL_DH) * gk).astype(BF16)
            vs_ref[pl.ds(base, n), :] = sub(v_ref).astype(BF16)
            for qb in range(n // DIL_QB):
                q0 = qb * DIL_QB
                k0 = min(max(q0 - DIL_SIDE, 0), n - kw)
                sc = lax.dot_general(qs_ref[pl.ds(base + q0, DIL_QB), :], ks_ref[pl.ds(base + k0, kw), :],
                                     NT_DIMS, preferred_element_type=F32)
                dist = jnp.abs((k0 + lax.broadcasted_iota(jnp.int32, (DIL_QB, kw), 1))
                               - (q0 + lax.broadcasted_iota(jnp.int32, (DIL_QB, kw), 0)))
                sc = sc - bias_step * dist.astype(F32)
                sc = jnp.where(dist <= DIL_SIDE, sc, -jnp.inf)
                mx = jnp.max(sc, axis=-1, keepdims=True)
                e = jnp.exp(sc - mx)
                den = jnp.sum(e, axis=-1, keepdims=True)
                acc = jnp.dot(e.astype(BF16), vs_ref[pl.ds(base + k0, kw), :], preferred_element_type=F32)
                if dil == 1:
                    rows = pl.ds(q0, DIL_QB)
                else:
                    rows = pl.ds(r + dil * q0, DIL_QB, stride=dil)
                acc_refs[gi][rows, :] = acc
                m_refs[gi][rows, :] = jnp.broadcast_to(mx, (DIL_QB, DIL_DH))
                l_refs[gi][rows, :] = jnp.broadcast_to(den, (DIL_QB, DIL_DH))
            return 0

        if dil == 1:
            residue(0, 0)
        else:
            lax.fori_loop(0, dil, residue, 0, unroll=min(dil, DIL_UNROLL))

    m_all = m_refs[0][...]
    for gi in range(1, n_g):
        m_all = jnp.maximum(m_all, m_refs[gi][...])
    num = jnp.zeros((s, DIL_DH), F32)
    den = jnp.zeros((s, DIL_DH), F32)
    for gi in range(n_g):
        w = jnp.exp(m_refs[gi][...] - m_all)
        num = num + acc_refs[gi][...] * w
        den = den + l_refs[gi][...] * w
    out_ref[0] = (num / den).astype(out_ref.dtype)


def dilated_attention(p, q_norm, k_norm):
    b, s, _ = p.shape
    n_g = len(DIL_GROUPS)
    nh = DIL_HEADS

    def spec(col0):
        return pl.BlockSpec((1, s, DIL_DH), lambda i, h: (i, 0, col0 + h))

    in_specs = [spec((gi * 3 + which) * nh) for gi in range(n_g) for which in range(3)]
    in_specs += [pl.BlockSpec((n_g, DIL_DH), lambda i, h: (0, 0))] * 2
    scratch = [pltpu.VMEM((s, DIL_DH), BF16)] * 3 + [pltpu.VMEM((s, DIL_DH), F32)] * (3 * n_g)
    return pl.pallas_call(
        _dilated_kernel,
        grid=(b, nh),
        in_specs=in_specs,
        out_specs=pl.BlockSpec((1, s, DIL_DH), lambda i, h: (i, 0, h)),
        out_shape=jax.ShapeDtypeStruct((b, s, nh * DIL_DH), BF16),
        scratch_shapes=scratch,
        compiler_params=_params("parallel", "parallel"),
        name="dilated_attention",
    )(*([p] * (3 * n_g)), q_norm.astype(F32), k_norm.astype(F32))


def dilated_mixer(x2d, norm_g, w_in, q_norm, k_norm, w_out, bsz, s):
    p = norm_matmul(x2d, norm_g, w_in.astype(BF16), kdim=x2d.shape[1], tm=MM_TM, tn=MM_TN)
    o = dilated_attention(p.reshape(bsz, s, -1), q_norm, k_norm)
    return matmul_res(o.reshape(bsz * s, -1), w_out.astype(BF16), x2d, tm=MM_TM, tn=MM_TN)


def _router_kernel(x_ref, g_ref, rw_ref, rb_ref, idx_ref, wts_ref):
    x = x_ref[...]
    xn = x * _rms_scale(x, x.shape[-1]) * g_ref[...]
    logits = lax.dot_general(rw_ref[...], xn, NT_DIMS, precision=lax.Precision.HIGHEST,
                             preferred_element_type=F32)
    scores = jax.nn.sigmoid(logits)
    sel = scores + rb_ref[...]
    ng = N_GROUPS
    slots = range(EXPERTS_PER_GROUP)
    v = [sel[j * ng:(j + 1) * ng, :] for j in slots]
    sc = [scores[j * ng:(j + 1) * ng, :] for j in slots]

    def first_argmax(vals):
        best = vals[0]
        for t in vals[1:]:
            best = jnp.maximum(best, t)
        idx = jnp.full(best.shape, len(vals) - 1, jnp.int32)
        for j in reversed(range(len(vals) - 1)):
            idx = jnp.where(vals[j] == best, j, idx)
        return best, idx

    m1, i1 = first_argmax(v)
    m2, i2 = first_argmax([jnp.where(i1 == j, -jnp.inf, v[j]) for j in slots])
    grp_score = m1 + m2
    grp_iota = lax.broadcasted_iota(jnp.int32, grp_score.shape, 0)
    best = jnp.max(grp_score, axis=0, keepdims=True)
    grp = jnp.min(jnp.where(grp_score == best, grp_iota, ng), axis=0, keepdims=True)
    w1 = sum(jnp.where(i1 == j, sc[j], 0.0) for j in slots)
    w2 = sum(jnp.where(i2 == j, sc[j], 0.0) for j in slots)
    tot = w1 + w2
    chosen = grp_iota == grp

    def pick(v):
        return jnp.sum(jnp.where(chosen, v, jnp.zeros_like(v)), axis=0, keepdims=True)

    idx_ref[0:1, :] = grp * EXPERTS_PER_GROUP + pick(i1)
    idx_ref[1:2, :] = grp * EXPERTS_PER_GROUP + pick(i2)
    wts_ref[0:1, :] = pick(w1 / tot)
    wts_ref[1:2, :] = pick(w2 / tot)


def router_topk(x2d, norm_g, router_w, router_b, *, tm):
    t, d = x2d.shape
    ne = N_EXPERTS
    perm = jnp.arange(ne).reshape(N_GROUPS, EXPERTS_PER_GROUP).T.reshape(-1)
    rw_t = router_w.astype(F32).T[perm]
    rb = router_b.astype(F32)[perm].reshape(ne, 1)
    return pl.pallas_call(
        _router_kernel,
        grid=(t // tm,),
        in_specs=[
            pl.BlockSpec((tm, d), lambda i: (i, 0)),
            pl.BlockSpec((1, d), lambda i: (0, 0)),
            pl.BlockSpec((ne, d), lambda i: (0, 0)),
            pl.BlockSpec((ne, 1), lambda i: (0, 0)),
        ],
        out_specs=[pl.BlockSpec((TOP_K, tm), lambda i: (0, i)), pl.BlockSpec((TOP_K, tm), lambda i: (0, i))],
        out_shape=[jax.ShapeDtypeStruct((TOP_K, t), jnp.int32), jax.ShapeDtypeStruct((TOP_K, t), F32)],
        compiler_params=_params("parallel"),
        name="router",
    )(x2d, norm_g.reshape(1, d), rw_t, rb)


def moe_plan(idx, tm):
    ne = N_EXPERTS
    e_flat = idx.reshape(-1)
    n_tiles = e_flat.shape[0] // tm + ne + 1
    onehot = (e_flat[:, None] == jnp.arange(ne, dtype=jnp.int32)[None, :]).astype(jnp.int32)
    csum = jnp.cumsum(onehot, axis=0)
    counts = csum[-1]
    padded = (counts + tm - 1) // tm * tm
    ends = jnp.cumsum(padded)
    starts = ends - padded
    pos = jnp.sum(onehot * (csum - 1 + starts[None, :]), axis=1)
    tile_row0 = jnp.arange(n_tiles, dtype=jnp.int32) * tm
    tile_expert = jnp.minimum(jnp.sum((tile_row0[:, None] >= ends[None, :]).astype(jnp.int32), axis=1), ne - 1)
    n_used = (ends[-1] // tm).reshape(1)
    return pos.astype(jnp.int32), tile_expert.astype(jnp.int32), n_used.astype(jnp.int32), n_tiles


def _moe_expert_kernel(pos_ref, te_ref, nused_ref, x_hbm, g_ref, wg_ref, wu_ref, wd_ref, ys_ref,
                       src_ref, xbuf, sem, wg_bf, wu_bf, wd_bf):
    i = pl.program_id(0)
    tm = xbuf.shape[1]
    n_used = nused_ref[0]
    n_assign = pos_ref.shape[0]
    n_tok = n_assign // TOP_K

    def row_copy(tok, slot, r):
        return pltpu.make_async_copy(x_hbm.at[pl.ds(tok, 1), :], xbuf.at[slot, pl.ds(r, 1), :], sem.at[slot])

    def wait(slot):
        pltpu.make_async_copy(x_hbm.at[pl.ds(0, tm), :], xbuf.at[slot], sem.at[slot]).wait()

    @pl.when(i == 0)
    def _():
        def clear(r, _):
            src_ref[r] = 0
            return 0
        lax.fori_loop(0, src_ref.shape[0], clear, 0, unroll=DMA_UNROLL)

        def scatter(n, _):
            src_ref[pos_ref[n]] = jnp.where(n >= n_tok, n - n_tok, n)
            return 0
        lax.fori_loop(0, n_assign, scatter, 0, unroll=DMA_UNROLL)

        def first(r, _):
            row_copy(src_ref[r], 0, r).start()
            return 0
        lax.fori_loop(0, tm, first, 0, unroll=DMA_UNROLL)

    slot = i % 2

    @pl.when(i <= n_used)
    def _():
        wait(slot)

    @pl.when(i < n_used)
    def _():
        @pl.when((i == 0) | (te_ref[i] != te_ref[jnp.maximum(i - 1, 0)]))
        def _():
            wg_bf[...] = wg_ref[0, 0].astype(BF16)
            wu_bf[...] = wu_ref[0, 0].astype(BF16)
            wd_bf[...] = wd_ref[0, 0].astype(BF16)

        next_base = (i + 1) * tm
        part = tm // 3

        def issue(lo, hi):
            for r in range(lo, hi):
                row_copy(src_ref[next_base + r], 1 - slot, r).start()

        x = xbuf[slot]
        xn = (x * _rms_scale(x, x.shape[-1]) * g_ref[...]).astype(BF16)
        issue(0, part)
        hg = jnp.dot(xn, wg_bf[...], preferred_element_type=F32)
        issue(part, 2 * part)
        hu = jnp.dot(xn, wu_bf[...], preferred_element_type=F32)
        a = hg * jax.nn.sigmoid(hg) * hu
        issue(2 * part, tm)
        ys_ref[...] = jnp.dot(a.astype(BF16), wd_bf[...], preferred_element_type=F32)

    @pl.when(i >= n_used)
    def _():
        ys_ref[...] = jnp.zeros_like(ys_ref)


def moe_experts(x2d, norm_g, pos, tile_expert, n_used, n_tiles, layer, w_gate, w_up, w_down, *, tm):
    t, d = x2d.shape
    f = w_gate.shape[-1]
    grid_spec = pltpu.PrefetchScalarGridSpec(
        num_scalar_prefetch=3,
        grid=(n_tiles,),
        in_specs=[
            pl.BlockSpec(memory_space=pl.ANY),
            pl.BlockSpec((1, d), lambda i, pos, te, nu: (0, 0)),
            pl.BlockSpec((1, 1, d, f), lambda i, pos, te, nu: (layer, te[i], 0, 0)),
            pl.BlockSpec((1, 1, d, f), lambda i, pos, te, nu: (layer, te[i], 0, 0)),
            pl.BlockSpec((1, 1, f, d), lambda i, pos, te, nu: (layer, te[i], 0, 0)),
        ],
        out_specs=pl.BlockSpec((tm, d), lambda i, pos, te, nu: (i, 0)),
        scratch_shapes=[
            pltpu.SMEM((n_tiles * tm,), jnp.int32),
            pltpu.VMEM((2, tm, d), F32),
            pltpu.SemaphoreType.DMA((2,)),
            pltpu.VMEM((d, f), BF16),
            pltpu.VMEM((d, f), BF16),
            pltpu.VMEM((f, d), BF16),
        ],
    )
    return pl.pallas_call(
        _moe_expert_kernel,
        grid_spec=grid_spec,
        out_shape=jax.ShapeDtypeStruct((n_tiles * tm, d), F32),
        compiler_params=_params("arbitrary"),
        name="moe_experts",
    )(pos, tile_expert, n_used, x2d, norm_g.reshape(1, d), w_gate, w_up, w_down)


def _moe_combine_kernel(pos_ref, x_ref, w_ref, ys_hbm, o_ref, buf, sem):
    i = pl.program_id(0)
    tc = x_ref.shape[0]
    n_tok = pos_ref.shape[0] // TOP_K

    def row_copy(p, slot, k, r):
        return pltpu.make_async_copy(ys_hbm.at[pl.ds(p, 1), :], buf.at[slot, k, pl.ds(r, 1), :], sem.at[slot])

    def issue(tile, slot):
        def body(r, _):
            for k in range(TOP_K):
                row_copy(pos_ref[k * n_tok + tile * tc + r], slot, k, r).start()
            return 0
        lax.fori_loop(0, tc, body, 0, unroll=DMA_UNROLL)

    def wait(slot):
        for k in range(TOP_K):
            pltpu.make_async_copy(ys_hbm.at[pl.ds(0, tc), :], buf.at[slot, k], sem.at[slot]).wait()

    @pl.when(i == 0)
    def _():
        issue(0, 0)

    slot = i % 2
    wait(slot)

    @pl.when(i + 1 < pl.num_programs(0))
    def _():
        issue(i + 1, 1 - slot)

    w = w_ref[...]
    o_ref[...] = x_ref[...] + w[:, 0:1] * buf[slot, 0] + w[:, 1:2] * buf[slot, 1]


def moe_combine(x2d, wts_tk, pos, ys, *, tc):
    t, d = x2d.shape
    grid_spec = pltpu.PrefetchScalarGridSpec(
        num_scalar_prefetch=1,
        grid=(t // tc,),
        in_specs=[
            pl.BlockSpec((tc, d), lambda i, pos: (i, 0)),
            pl.BlockSpec((tc, TOP_K), lambda i, pos: (i, 0)),
            pl.BlockSpec(memory_space=pl.ANY),
        ],
        out_specs=pl.BlockSpec((tc, d), lambda i, pos: (i, 0)),
        scratch_shapes=[pltpu.VMEM((2, TOP_K, tc, d), F32), pltpu.SemaphoreType.DMA((2,))],
    )
    return pl.pallas_call(
        _moe_combine_kernel,
        grid_spec=grid_spec,
        out_shape=jax.ShapeDtypeStruct((t, d), F32),
        compiler_params=_params("arbitrary"),
        name="moe_combine",
    )(pos, x2d, wts_tk, ys)


def moe_ffn(x2d, norm_g, router_w, router_b, layer, w_gate, w_up, w_down):
    idx, wts = router_topk(x2d, norm_g, router_w, router_b, tm=512)
    pos, tile_expert, n_used, n_tiles = moe_plan(idx, MOE_TM)
    ys = moe_experts(x2d, norm_g, pos, tile_expert, n_used, n_tiles, layer, w_gate, w_up, w_down, tm=MOE_TM)
    return moe_combine(x2d, wts.T, pos, ys, tc=MOE_TC)


def kernel(x, norm_mix, norm_ffn, a_w_in, a_conv_w, a_gate_b, a_out_norm, a_w_out, b_w_in, b_kv_norm, b_w_ukv, b_q_norm, b_k_norm, b_w_out, c_w_in, c_q_norm, c_k_norm, c_w_out, router_w, router_b, moe_w_gate, moe_w_up, moe_w_down):
    bsz, s, d = x.shape
    depth = norm_mix.shape[0]
    x2d = x.reshape(bsz * s, d)
    for i in range(depth):
        kind = i % 3
        j = i // 3
        if kind == 0:
            x2d = mlstm_mixer(x2d, norm_mix[i], a_w_in[j], a_conv_w[j], a_gate_b[j], a_out_norm[j], a_w_out[j],
                              bsz, s)
        elif kind == 1:
            x2d = mla_mixer(x2d, norm_mix[i], b_w_in[j], b_kv_norm[j], b_w_ukv[j], b_q_norm[j], b_k_norm[j],
                            b_w_out[j], bsz, s)
        else:
            x2d = dilated_mixer(x2d, norm_mix[i], c_w_in[j], c_q_norm[j], c_k_norm[j], c_w_out[j], bsz, s)
        x2d = moe_ffn(x2d, norm_ffn[i], router_w, router_b, i, moe_w_gate, moe_w_up, moe_w_down)
    return x2d.reshape(bsz, s, d)
```

```python
import functools
import math

import jax
import jax.numpy as jnp
from jax import lax
from jax.experimental import pallas as pl
from jax.experimental.pallas import tpu as pltpu

F32 = jnp.float32
BF16 = jnp.bfloat16

EPS = 1e-6
LANES = 128
VMEM_LIMIT_BYTES = 56 * 1024 * 1024
MM_TM = 1024
MM_TN = 1024

MLSTM_HEADS = 8
MLSTM_DQK = 128
MLSTM_DV = 256
MLSTM_CHUNK = 128

MLA_HEADS = 16
MLA_NOPE = 128
MLA_ROPE = 64
MLA_DV = 128
MLA_KV_RANK = 512
ROPE_BASE = 10000.0

DIL_GROUPS = ((128, 1), (512, 4), (2048, 16))
DIL_HEADS = 8
DIL_DH = 128
DIL_SIDE = 64

N_EXPERTS = 16
N_GROUPS = 4
EXPERTS_PER_GROUP = 4
TOP_K = 2
MOE_TM = 512
MOE_TC = 256
DMA_UNROLL = 8

NT_DIMS = (((1,), (1,)), ((), ()))
TN_DIMS = (((0,), (0,)), ((), ()))


def _params(*sem):
    return pltpu.CompilerParams(dimension_semantics=sem, vmem_limit_bytes=VMEM_LIMIT_BYTES)


def _rms_scale(x, width):
    return lax.rsqrt(jnp.sum(x * x, axis=-1, keepdims=True) * (1.0 / width) + EPS)


def _norm_matmul_kernel(x_ref, g_ref, w_ref, o_ref, xn_ref):
    @pl.when(pl.program_id(1) == 0)
    def _():
        x = x_ref[...].astype(F32)
        xn_ref[...] = (x * _rms_scale(x, x.shape[-1]) * g_ref[...]).astype(BF16)

    o_ref[...] = jnp.dot(xn_ref[...], w_ref[...], preferred_element_type=F32).astype(o_ref.dtype)


def norm_matmul(x, g, w, *, kdim, tm, tn, out_dtype=F32):
    t = x.shape[0]
    n = w.shape[1]
    return pl.pallas_call(
        _norm_matmul_kernel,
        grid=(t // tm, n // tn),
        in_specs=[
            pl.BlockSpec((tm, kdim), lambda i, j: (i, 0)),
            pl.BlockSpec((1, kdim), lambda i, j: (0, 0)),
            pl.BlockSpec((kdim, tn), lambda i, j: (0, j)),
        ],
        out_specs=pl.BlockSpec((tm, tn), lambda i, j: (i, j)),
        out_shape=jax.ShapeDtypeStruct((t, n), out_dtype),
        scratch_shapes=[pltpu.VMEM((tm, kdim), BF16)],
        compiler_params=_params("parallel", "arbitrary"),
        name="norm_matmul",
    )(x, g.reshape(1, kdim), w)


def _matmul_res_kernel(a_ref, w_ref, r_ref, o_ref):
    o_ref[...] = r_ref[...] + jnp.dot(a_ref[...].astype(BF16), w_ref[...], preferred_element_type=F32)


def matmul_res(a, w, res, *, tm, tn):
    t, k = a.shape
    n = w.shape[1]
    return pl.pallas_call(
        _matmul_res_kernel,
        grid=(t // tm, n // tn),
        in_specs=[
            pl.BlockSpec((tm, k), lambda i, j: (i, 0)),
            pl.BlockSpec((k, tn), lambda i, j: (0, j)),
            pl.BlockSpec((tm, tn), lambda i, j: (i, j)),
        ],
        out_specs=pl.BlockSpec((tm, tn), lambda i, j: (i, j)),
        out_shape=jax.ShapeDtypeStruct((t, n), F32),
        compiler_params=_params("parallel", "arbitrary"),
        name="matmul_res",
    )(a, w, res)


def _log_sigmoid(x):
    return jnp.minimum(x, 0.0) - jnp.log(1.0 + jnp.exp(-jnp.abs(x)))


def _mlstm_kernel(q_ref, k_ref, v_ref, o_ref, cwq_ref, cwk_ref, gates_ref, onorm_ref, out_ref,
                  qs_ref, ks_ref, hf_ref, hb_ref):
    s = q_ref.shape[1]
    chunk = MLSTM_CHUNK
    n_chunks = s // chunk
    row = lax.broadcasted_iota(jnp.int32, (s, 1), 0)

    def conv_silu(p, w):
        prev = jnp.where(row == 0, 0.0, pltpu.roll(p, 1, axis=0))
        nxt = jnp.where(row == s - 1, 0.0, pltpu.roll(p, s - 1, axis=0))
        y = w[0:1, :] * prev + w[1:2, :] * p + w[2:3, :] * nxt
        return y * jax.nn.sigmoid(y)

    qs_ref[...] = (conv_silu(q_ref[0], cwq_ref[...]) * (MLSTM_DQK ** -0.5)).astype(BF16)
    ks_ref[...] = conv_silu(k_ref[0], cwk_ref[...]).astype(BF16)

    ti = lax.broadcasted_iota(jnp.int32, (chunk, chunk), 0)
    si = lax.broadcasted_iota(jnp.int32, (chunk, chunk), 1)
    eye = ti == si

    def to_col(r):
        return jnp.sum(jnp.where(eye, jnp.broadcast_to(r, (chunk, chunk)), 0.0), axis=1, keepdims=True)

    def to_row(c):
        return jnp.sum(jnp.where(eye, jnp.broadcast_to(c, (chunk, chunk)), 0.0), axis=0, keepdims=True)

    def chunk_step(c, i_idx, f_idx, reverse, carry, hdir_ref):
        mask = (si >= ti) if reverse else (si <= ti)
        c_st, n_st, m_st = carry
        r0 = pl.multiple_of(c * chunk, chunk)
        q = qs_ref[pl.ds(r0, chunk), :]
        k = ks_ref[pl.ds(r0, chunk), :]
        v = v_ref[0, pl.ds(r0, chunk), :].astype(BF16)
        li = gates_ref[0, 0, i_idx:i_idx + 1, pl.ds(r0, chunk)]
        lf = _log_sigmoid(gates_ref[0, 0, f_idx:f_idx + 1, pl.ds(r0, chunk)])
        b_col = jnp.sum(jnp.where(mask, jnp.broadcast_to(lf, (chunk, chunk)), 0.0), axis=1, keepdims=True)
        b_row = to_row(b_col)
        g = jnp.sum(lf, axis=1, keepdims=True)
        dlog = jnp.where(mask, b_col - b_row + li, -jnp.inf)
        m_intra = jnp.max(dlog, axis=1, keepdims=True)
        a_inter = b_col + m_st
        m = jnp.maximum(a_inter, m_intra)
        e_inter = jnp.exp(a_inter - m)
        s_qk = lax.dot_general(q, k, NT_DIMS, preferred_element_type=F32)
        dmat = jnp.exp(dlog - m) * s_qk
        num = (e_inter * jnp.dot(q, c_st.astype(BF16), preferred_element_type=F32)
               + jnp.dot(dmat.astype(BF16), v, preferred_element_type=F32))
        den = (e_inter * jnp.sum(q.astype(F32) * n_st, axis=1, keepdims=True)
               + jnp.sum(dmat, axis=1, keepdims=True))
        hdir_ref[pl.ds(r0, chunk), :] = num / jnp.maximum(jnp.abs(den), jnp.exp(-m))
        w_row = g - b_row + li
        m_loc = jnp.max(w_row, axis=1, keepdims=True)
        ew_col = to_col(jnp.exp(w_row - m_loc))
        kw = k.astype(F32) * ew_col
        c_loc = lax.dot_general(kw.astype(BF16), v, TN_DIMS, preferred_element_type=F32)
        n_loc = jnp.sum(kw, axis=0, keepdims=True)
        m_new = jnp.maximum(g + m_st, m_loc)
        a = jnp.exp(g + m_st - m_new)
        e = jnp.exp(m_loc - m_new)
        return a * c_st + e * c_loc, a * n_st + e * n_loc, m_new

    def body(ci, carry):
        fwd = chunk_step(ci, 0, 1, False, carry[0], hf_ref)
        bwd = chunk_step(n_chunks - 1 - ci, 2, 3, True, carry[1], hb_ref)
        return fwd, bwd

    init = (jnp.zeros((MLSTM_DQK, MLSTM_DV), F32), jnp.zeros((1, MLSTM_DQK), F32), jnp.zeros((1, 1), F32))
    lax.fori_loop(0, n_chunks, body, (init, init), unroll=2)

    h = hf_ref[...] + hb_ref[...]
    y = h * _rms_scale(h, MLSTM_DV) * onorm_ref[...]
    out_ref[0] = (jax.nn.sigmoid(o_ref[0]) * y).astype(out_ref.dtype)


def mlstm_core(p, conv_w, gates_rows, out_norm):
    b, s, _ = p.shape
    nh = MLSTM_HEADS
    k_off = nh * MLSTM_DQK // MLSTM_DQK
    v_off = 2 * nh * MLSTM_DQK // MLSTM_DV
    o_off = v_off + nh
    return pl.pallas_call(
        _mlstm_kernel,
        grid=(b, nh),
        in_specs=[
            pl.BlockSpec((1, s, MLSTM_DQK), lambda i, h: (i, 0, h)),
            pl.BlockSpec((1, s, MLSTM_DQK), lambda i, h: (i, 0, k_off + h)),
            pl.BlockSpec((1, s, MLSTM_DV), lambda i, h: (i, 0, v_off + h)),
            pl.BlockSpec((1, s, MLSTM_DV), lambda i, h: (i, 0, o_off + h)),
            pl.BlockSpec((3, MLSTM_DQK), lambda i, h: (0, h)),
            pl.BlockSpec((3, MLSTM_DQK), lambda i, h: (0, k_off + h)),
            pl.BlockSpec((1, 1, 4, s), lambda i, h: (i, h, 0, 0)),
            pl.BlockSpec((1, MLSTM_DV), lambda i, h: (0, h)),
        ],
        out_specs=pl.BlockSpec((1, s, MLSTM_DV), lambda i, h: (i, 0, h)),
        out_shape=jax.ShapeDtypeStruct((b, s, nh * MLSTM_DV), BF16),
        scratch_shapes=[
            pltpu.VMEM((s, MLSTM_DQK), BF16),
            pltpu.VMEM((s, MLSTM_DQK), BF16),
            pltpu.VMEM((s, MLSTM_DV), F32),
            pltpu.VMEM((s, MLSTM_DV), F32),
        ],
        compiler_params=_params("parallel", "parallel"),
        name="mlstm_core",
    )(p, p, p, p, conv_w, conv_w, gates_rows, out_norm.reshape(1, -1))


def mlstm_mixer(x2d, norm_g, w_in, conv_w, gate_b, out_norm, w_out, bsz, s):
    nh = MLSTM_HEADS
    main_cols = 2 * nh * MLSTM_DQK + 2 * nh * MLSTM_DV
    p = norm_matmul(x2d, norm_g, w_in[:, :main_cols].astype(BF16), kdim=x2d.shape[1], tm=MM_TM, tn=MM_TN)
    gates = norm_matmul(x2d, norm_g, w_in[:, main_cols:].astype(BF16), kdim=x2d.shape[1], tm=MM_TM, tn=4 * nh)
    gates = gates + gate_b.astype(F32)
    gates_rows = gates.reshape(bsz, s, 4, nh).transpose(0, 3, 2, 1)
    hg = mlstm_core(p.reshape(bsz, s, main_cols), conv_w, gates_rows, out_norm)
    return matmul_res(hg.reshape(bsz * s, nh * MLSTM_DV), w_out.astype(BF16), x2d, tm=MM_TM, tn=MM_TN)


MLA_TQ = 256


def _mla_kernel(qn_ref, qr_ref, kv_ref, kr_ref, cos_ref, sin_ref, gqn_ref, gqr_ref, gkn_ref, gkr_ref,
                out_ref, qs_ref, ks_ref, vs_ref):
    s = qn_ref.shape[1]
    dqk = MLA_NOPE + MLA_ROPE
    half = MLA_ROPE // 2
    cos = cos_ref[...]
    sin = sin_ref[...]
    lane = lax.broadcasted_iota(jnp.int32, (1, LANES), 1)
    first_half = (lane // half) % 2 == 0

    def rope(x):
        rot = jnp.where(first_half, -pltpu.roll(x, LANES - half, axis=1), pltpu.roll(x, half, axis=1))
        return x * cos + rot * sin

    qr_raw = qr_ref[0]
    kr_raw = kr_ref[0]
    q_rope = rope(qr_raw * gqr_ref[...])
    k_rope = rope(kr_raw * gkr_ref[...])
    kr_ss = 0.5 * jnp.sum(kr_raw * kr_raw, axis=-1, keepdims=True)
    for hh in range(2):
        own = (lane // MLA_ROPE) == hh
        qn = qn_ref[0, :, hh * MLA_NOPE:(hh + 1) * MLA_NOPE]
        kn = kv_ref[0, :, hh * 2 * LANES:hh * 2 * LANES + MLA_NOPE]
        vs_ref[...] = kv_ref[0, :, hh * 2 * LANES + MLA_NOPE:(hh + 1) * 2 * LANES].astype(BF16)
        q_ss = (jnp.sum(qn * qn, axis=-1, keepdims=True)
                + jnp.sum(jnp.where(own, qr_raw * qr_raw, 0.0), axis=-1, keepdims=True))
        rq = lax.rsqrt(q_ss * (1.0 / dqk) + EPS) * (dqk ** -0.5)
        rk = lax.rsqrt((jnp.sum(kn * kn, axis=-1, keepdims=True) + kr_ss) * (1.0 / dqk) + EPS)
        qs_ref[:, 0:LANES] = (qn * rq * gqn_ref[...]).astype(BF16)
        qs_ref[:, LANES:2 * LANES] = (jnp.where(own, q_rope, 0.0) * rq).astype(BF16)
        ks_ref[:, 0:LANES] = (kn * rk * gkn_ref[...]).astype(BF16)
        ks_ref[:, LANES:2 * LANES] = (k_rope * rk).astype(BF16)

        def qblock(i, _):
            r0 = pl.multiple_of(i * MLA_TQ, MLA_TQ)
            sc = lax.dot_general(qs_ref[pl.ds(r0, MLA_TQ), :], ks_ref[...], NT_DIMS, preferred_element_type=F32)
            mx = jnp.max(sc, axis=-1, keepdims=True)
            e = jnp.exp(sc - mx)
            den = jnp.sum(e, axis=-1, keepdims=True)
            o = jnp.dot(e.astype(BF16), vs_ref[...], preferred_element_type=F32) / den
            out_ref[0, pl.ds(r0, MLA_TQ), hh * MLA_DV:(hh + 1) * MLA_DV] = o.astype(out_ref.dtype)
            return 0

        lax.fori_loop(0, s // MLA_TQ, qblock, 0, unroll=2)


def mla_attention(qall, kv, kr, cos, sin, q_norm, k_norm):
    b, s, _ = qall.shape
    nh = MLA_HEADS
    qr_off = nh * MLA_NOPE // (2 * MLA_ROPE)
    row = lambda t: t.reshape(1, -1).astype(F32)
    twice = lambda t: jnp.concatenate([t, t], axis=-1)
    vec = pl.BlockSpec((1, LANES), lambda i, h: (0, 0))
    cos = jnp.tile(cos, (1, LANES // cos.shape[1]))
    sin = jnp.tile(sin, (1, LANES // sin.shape[1]))
    return pl.pallas_call(
        _mla_kernel,
        grid=(b, nh // 2),
        in_specs=[
            pl.BlockSpec((1, s, 2 * MLA_NOPE), lambda i, h: (i, 0, h)),
            pl.BlockSpec((1, s, 2 * MLA_ROPE), lambda i, h: (i, 0, qr_off + h)),
            pl.BlockSpec((1, s, 4 * LANES), lambda i, h: (i, 0, h)),
            pl.BlockSpec((1, s, LANES), lambda i, h: (i, 0, 0)),
            pl.BlockSpec((s, LANES), lambda i, h: (0, 0)),
            pl.BlockSpec((s, LANES), lambda i, h: (0, 0)),
            vec, vec, vec, vec,
        ],
        out_specs=pl.BlockSpec((1, s, 2 * MLA_DV), lambda i, h: (i, 0, h)),
        out_shape=jax.ShapeDtypeStruct((b, s, nh * MLA_DV), BF16),
        scratch_shapes=[
            pltpu.VMEM((s, 2 * LANES), BF16),
            pltpu.VMEM((s, 2 * LANES), BF16),
            pltpu.VMEM((s, MLA_DV), BF16),
        ],
        compiler_params=_params("parallel", "parallel"),
        name="mla_attention",
    )(qall, qall, kv, twice(kr), cos, sin,
      row(q_norm[:MLA_NOPE]), twice(row(q_norm[MLA_NOPE:])), row(k_norm[:MLA_NOPE]), twice(row(k_norm[MLA_NOPE:])))


def mla_mixer(x2d, norm_g, w_in, kv_norm, w_ukv, q_norm, k_norm, w_out, bsz, s):
    nh = MLA_HEADS
    dqk = MLA_NOPE + MLA_ROPE
    d = x2d.shape[1]
    w_q = w_in[:, :nh * dqk].reshape(d, nh, dqk)
    w_qcat = jnp.concatenate([w_q[:, :, :MLA_NOPE].reshape(d, nh * MLA_NOPE),
                              w_q[:, :, MLA_NOPE:].reshape(d, nh * MLA_ROPE)], axis=1).astype(BF16)
    qall = norm_matmul(x2d, norm_g, w_qcat, kdim=d, tm=MM_TM, tn=MM_TN)
    lat = norm_matmul(x2d, norm_g, w_in[:, nh * dqk:].astype(BF16), kdim=d, tm=MM_TM, tn=MLA_KV_RANK + MLA_ROPE)
    kv = norm_matmul(lat, kv_norm, w_ukv.astype(BF16), kdim=MLA_KV_RANK, tm=MM_TM, tn=MM_TN)
    kr = lat[:, MLA_KV_RANK:]
    half = MLA_ROPE // 2
    inv = ROPE_BASE ** (-jnp.arange(half, dtype=F32) / half)
    ang = jnp.arange(s).astype(F32)[:, None] * inv[None, :]
    o = mla_attention(qall.reshape(bsz, s, -1), kv.reshape(bsz, s, -1), kr.reshape(bsz, s, MLA_ROPE),
                      jnp.cos(ang), jnp.sin(ang), q_norm, k_norm)
    return matmul_res(o.reshape(bsz * s, nh * MLA_DV), w_out.astype(BF16), x2d, tm=MM_TM, tn=MM_TN)


DIL_QB = 128
DIL_KW = 256
DIL_UNROLL = 4


def _dilated_kernel(*refs):
    n_g = len(DIL_GROUPS)
    qkv_refs = refs[:3 * n_g]
    gq_ref, gk_ref = refs[3 * n_g], refs[3 * n_g + 1]
    out_ref = refs[3 * n_g + 2]
    qs_ref, ks_ref, vs_ref = refs[3 * n_g + 3:3 * n_g + 6]
    acc_refs = refs[3 * n_g + 6:3 * n_g + 6 + n_g]
    m_refs = refs[3 * n_g + 6 + n_g:3 * n_g + 6 + 2 * n_g]
    l_refs = refs[3 * n_g + 6 + 2 * n_g:3 * n_g + 6 + 3 * n_g]
    s = out_ref.shape[1]
    head = pl.program_id(1)
    scale = DIL_DH ** -0.5

    for gi, (_, dil) in enumerate(DIL_GROUPS):
        q_ref, k_ref, v_ref = qkv_refs[3 * gi:3 * gi + 3]
        n = s // dil
        kw = min(DIL_KW, n)
        slope_arg = jnp.full((1, 1), -8.0 / (n_g * DIL_HEADS), F32) * (gi * DIL_HEADS + head + 1).astype(F32)
        bias_step = jnp.exp2(slope_arg) * float(dil)
        gq = gq_ref[gi:gi + 1, :]
        gk = gk_ref[gi:gi + 1, :]

        def residue(r, _, q_ref=q_ref, k_ref=k_ref, v_ref=v_ref, n=n, kw=kw, dil=dil, gi=gi,
                    bias_step=bias_step, gq=gq, gk=gk):
            def sub(ref):
                if dil == 1:
                    return ref[0]
                return ref[0, pl.ds(r, n, stride=dil), :]

            q = sub(q_ref)
            k = sub(k_ref)
            base = 0 if dil == 1 else pl.multiple_of(r * n, n)
            qs_ref[pl.ds(base, n), :] = (q * _rms_scale(q, DIL_DH) * gq * scale).astype(BF16)
            ks_ref[pl.ds(base, n), :] = (k * _rms_scale(k, DIL_DH) * gk).astype(BF16)
            vs_ref[pl.ds(base, n), :] = sub(v_ref).astype(BF16)
            for qb in range(n // DIL_QB):
                q0 = qb * DIL_QB
                k0 = min(max(q0 - DIL_SIDE, 0), n - kw)
                sc = lax.dot_general(qs_ref[pl.ds(base + q0, DIL_QB), :], ks_ref[pl.ds(base + k0, kw), :],
                                     NT_DIMS, preferred_element_type=F32)
                dist = jnp.abs((k0 + lax.broadcasted_iota(jnp.int32, (DIL_QB, kw), 1))
                               - (q0 + lax.broadcasted_iota(jnp.int32, (DIL_QB, kw), 0)))
                sc = sc - bias_step * dist.astype(F32)
                sc = jnp.where(dist <= DIL_SIDE, sc, -jnp.inf)
                mx = jnp.max(sc, axis=-1, keepdims=True)
                e = jnp.exp(sc - mx)
                den = jnp.sum(e, axis=-1, keepdims=True)
                acc = jnp.dot(e.astype(BF16), vs_ref[pl.ds(base + k0, kw), :], preferred_element_type=F32)
                if dil == 1:
                    rows = pl.ds(q0, DIL_QB)
                else:
                    rows = pl.ds(r + dil * q0, DIL_QB, stride=dil)
                acc_refs[gi][rows, :] = acc
                m_refs[gi][rows, :] = jnp.broadcast_to(mx, (DIL_QB, DIL_DH))
                l_refs[gi][rows, :] = jnp.broadcast_to(den, (DIL_QB, DIL_DH))
            return 0

        if dil == 1:
            residue(0, 0)
        else:
            lax.fori_loop(0, dil, residue, 0, unroll=min(dil, DIL_UNROLL))

    m_all = m_refs[0][...]
    for gi in range(1, n_g):
        m_all = jnp.maximum(m_all, m_refs[gi][...])
    num = jnp.zeros((s, DIL_DH), F32)
    den = jnp.zeros((s, DIL_DH), F32)
    for gi in range(n_g):
        w = jnp.exp(m_refs[gi][...] - m_all)
        num = num + acc_refs[gi][...] * w
        den = den + l_refs[gi][...] * w
    out_ref[0] = (num / den).astype(out_ref.dtype)


def dilated_attention(p, q_norm, k_norm):
    b, s, _ = p.shape
    n_g = len(DIL_GROUPS)
    nh = DIL_HEADS

    def spec(col0):
        return pl.BlockSpec((1, s, DIL_DH), lambda i, h: (i, 0, col0 + h))

    in_specs = [spec((gi * 3 + which) * nh) for gi in range(n_g) for which in range(3)]
    in_specs += [pl.BlockSpec((n_g, DIL_DH), lambda i, h: (0, 0))] * 2
    scratch = [pltpu.VMEM((s, DIL_DH), BF16)] * 3 + [pltpu.VMEM((s, DIL_DH), F32)] * (3 * n_g)
    return pl.pallas_call(
        _dilated_kernel,
        grid=(b, nh),
        in_specs=in_specs,
        out_specs=pl.BlockSpec((1, s, DIL_DH), lambda i, h: (i, 0, h)),
        out_shape=jax.ShapeDtypeStruct((b, s, nh * DIL_DH), BF16),
        scratch_shapes=scratch,
        compiler_params=_params("parallel", "parallel"),
        name="dilated_attention",
    )(*([p] * (3 * n_g)), q_norm.astype(F32), k_norm.astype(F32))


def dilated_mixer(x2d, norm_g, w_in, q_norm, k_norm, w_out, bsz, s):
    p = norm_matmul(x2d, norm_g, w_in.astype(BF16), kdim=x2d.shape[1], tm=MM_TM, tn=MM_TN)
    o = dilated_attention(p.reshape(bsz, s, -1), q_norm, k_norm)
    return matmul_res(o.reshape(bsz * s, -1), w_out.astype(BF16), x2d, tm=MM_TM, tn=MM_TN)


def _router_kernel(x_ref, g_ref, rw_ref, rb_ref, idx_ref, wts_ref, xrows_ref):
    x = x_ref[...]
    tm, d = x.shape
    xn = x * _rms_scale(x, d) * g_ref[...]
    n_sub = d // LANES
    for a in range(n_sub):
        xrows_ref[pl.ds(a, tm, stride=n_sub), :] = xn[:, a * LANES:(a + 1) * LANES]
    logits = lax.dot_general(rw_ref[...], xn, NT_DIMS, precision=lax.Precision.HIGHEST,
                             preferred_element_type=F32)
    scores = jax.nn.sigmoid(logits)
    sel = scores + rb_ref[...]
    ng = N_GROUPS
    slots = range(EXPERTS_PER_GROUP)
    v = [sel[j * ng:(j + 1) * ng, :] for j in slots]
    sc = [scores[j * ng:(j + 1) * ng, :] for j in slots]

    def first_argmax(vals):
        best = vals[0]
        for t in vals[1:]:
            best = jnp.maximum(best, t)
        idx = jnp.full(best.shape, len(vals) - 1, jnp.int32)
        for j in reversed(range(len(vals) - 1)):
            idx = jnp.where(vals[j] == best, j, idx)
        return best, idx

    m1, i1 = first_argmax(v)
    m2, i2 = first_argmax([jnp.where(i1 == j, -jnp.inf, v[j]) for j in slots])
    grp_score = m1 + m2
    grp_iota = lax.broadcasted_iota(jnp.int32, grp_score.shape, 0)
    best = jnp.max(grp_score, axis=0, keepdims=True)
    grp = jnp.min(jnp.where(grp_score == best, grp_iota, ng), axis=0, keepdims=True)
    w1 = sum(jnp.where(i1 == j, sc[j], 0.0) for j in slots)
    w2 = sum(jnp.where(i2 == j, sc[j], 0.0) for j in slots)
    tot = w1 + w2
    chosen = grp_iota == grp

    def pick(v):
        return jnp.sum(jnp.where(chosen, v, jnp.zeros_like(v)), axis=0, keepdims=True)

    idx_ref[0:1, :] = grp * EXPERTS_PER_GROUP + pick(i1)
    idx_ref[1:2, :] = grp * EXPERTS_PER_GROUP + pick(i2)
    wts_ref[0:1, :] = pick(w1 / tot)
    wts_ref[1:2, :] = pick(w2 / tot)


def router_topk(x2d, norm_g, router_w, router_b, *, tm):
    t, d = x2d.shape
    ne = N_EXPERTS
    perm = jnp.arange(ne).reshape(N_GROUPS, EXPERTS_PER_GROUP).T.reshape(-1)
    rw_t = router_w.astype(F32).T[perm]
    rb = router_b.astype(F32)[perm].reshape(ne, 1)
    return pl.pallas_call(
        _router_kernel,
        grid=(t // tm,),
        in_specs=[
            pl.BlockSpec((tm, d), lambda i: (i, 0)),
            pl.BlockSpec((1, d), lambda i: (0, 0)),
            pl.BlockSpec((ne, d), lambda i: (0, 0)),
            pl.BlockSpec((ne, 1), lambda i: (0, 0)),
        ],
        out_specs=[pl.BlockSpec((TOP_K, tm), lambda i: (0, i)), pl.BlockSpec((TOP_K, tm), lambda i: (0, i)),
                   pl.BlockSpec((tm * (d // LANES), LANES), lambda i: (i, 0))],
        out_shape=[jax.ShapeDtypeStruct((TOP_K, t), jnp.int32), jax.ShapeDtypeStruct((TOP_K, t), F32),
                   jax.ShapeDtypeStruct((t * (d // LANES), LANES), F32)],
        compiler_params=_params("parallel"),
        name="router",
    )(x2d, norm_g.reshape(1, d), rw_t, rb)


def moe_plan(idx, tm):
    ne = N_EXPERTS
    e_flat = idx.reshape(-1)
    n_tiles = e_flat.shape[0] // tm + ne + 1
    onehot = (e_flat[:, None] == jnp.arange(ne, dtype=jnp.int32)[None, :]).astype(jnp.int32)
    csum = jnp.cumsum(onehot, axis=0)
    counts = csum[-1]
    padded = (counts + tm - 1) // tm * tm
    ends = jnp.cumsum(padded)
    starts = ends - padded
    pos = jnp.sum(onehot * (csum - 1 + starts[None, :]), axis=1)
    tile_row0 = jnp.arange(n_tiles, dtype=jnp.int32) * tm
    tile_expert = jnp.minimum(jnp.sum((tile_row0[:, None] >= ends[None, :]).astype(jnp.int32), axis=1), ne - 1)
    n_used = (ends[-1] // tm).reshape(1)
    return pos.astype(jnp.int32), tile_expert.astype(jnp.int32), n_used.astype(jnp.int32), n_tiles


def _moe_expert_kernel(pos_ref, te_ref, nused_ref, xrows_hbm, wg_ref, wu_ref, wd_ref, ys_ref,
                       src_ref, xbuf, sem, xn_bf, wg_bf, wu_bf, wd_bf):
    i = pl.program_id(0)
    tm, d = xn_bf.shape
    n_sub = d // LANES
    n_used = nused_ref[0]
    n_assign = pos_ref.shape[0]
    n_tok = n_assign // TOP_K

    def row_copy(tok, slot, r):
        return pltpu.make_async_copy(xrows_hbm.at[pl.ds(pl.multiple_of(tok * n_sub, n_sub), n_sub), :],
                                     xbuf.at[slot, pl.ds(r * n_sub, n_sub), :], sem.at[slot])

    def wait(slot):
        pltpu.make_async_copy(xrows_hbm.at[pl.ds(0, tm * n_sub), :], xbuf.at[slot], sem.at[slot]).wait()

    @pl.when(i == 0)
    def _():
        def clear(r, _):
            src_ref[r] = 0
            return 0
        lax.fori_loop(0, src_ref.shape[0], clear, 0, unroll=DMA_UNROLL)

        for k in range(TOP_K):
            def scatter(n, _, k=k):
                src_ref[pos_ref[k * n_tok + n]] = n
                return 0
            lax.fori_loop(0, n_tok, scatter, 0, unroll=DMA_UNROLL)

        def first(r, _):
            row_copy(src_ref[r], 0, r).start()
            return 0
        lax.fori_loop(0, tm, first, 0, unroll=DMA_UNROLL)

    slot = i % 2

    @pl.when(i <= n_used)
    def _():
        wait(slot)

    @pl.when(i < n_used)
    def _():
        @pl.when((i == 0) | (te_ref[i] != te_ref[jnp.maximum(i - 1, 0)]))
        def _():
            wg_bf[...] = wg_ref[0, 0].astype(BF16)
            wu_bf[...] = wu_ref[0, 0].astype(BF16)
            wd_bf[...] = wd_ref[0, 0].astype(BF16)

        next_base = (i + 1) * tm
        part = tm // 3

        def issue(lo, hi):
            for r in range(lo, hi):
                row_copy(src_ref[next_base + r], 1 - slot, r).start()

        for a in range(n_sub):
            xn_bf[:, a * LANES:(a + 1) * LANES] = xbuf[slot, pl.ds(a, tm, stride=n_sub), :].astype(BF16)
        xn = xn_bf[...]
        issue(0, part)
        hg = jnp.dot(xn, wg_bf[...], preferred_element_type=F32)
        issue(part, 2 * part)
        hu = jnp.dot(xn, wu_bf[...], preferred_element_type=F32)
        a = hg * jax.nn.sigmoid(hg) * hu
        issue(2 * part, tm)
        ys_ref[...] = jnp.dot(a.astype(BF16), wd_bf[...], preferred_element_type=F32)

    @pl.when(i >= n_used)
    def _():
        ys_ref[...] = jnp.zeros_like(ys_ref)


def moe_experts(xrows, pos, tile_expert, n_used, n_tiles, layer, w_gate, w_up, w_down, *, tm):
    d, f = w_gate.shape[-2:]
    n_sub = d // LANES
    grid_spec = pltpu.PrefetchScalarGridSpec(
        num_scalar_prefetch=3,
        grid=(n_tiles,),
        in_specs=[
            pl.BlockSpec(memory_space=pl.ANY),
            pl.BlockSpec((1, 1, d, f), lambda i, pos, te, nu: (layer, te[i], 0, 0)),
            pl.BlockSpec((1, 1, d, f), lambda i, pos, te, nu: (layer, te[i], 0, 0)),
            pl.BlockSpec((1, 1, f, d), lambda i, pos, te, nu: (layer, te[i], 0, 0)),
        ],
        out_specs=pl.BlockSpec((tm, d), lambda i, pos, te, nu: (i, 0)),
        scratch_shapes=[
            pltpu.SMEM((n_tiles * tm,), jnp.int32),
            pltpu.VMEM((2, tm * n_sub, LANES), F32),
            pltpu.SemaphoreType.DMA((2,)),
            pltpu.VMEM((tm, d), BF16),
            pltpu.VMEM((d, f), BF16),
            pltpu.VMEM((d, f), BF16),
            pltpu.VMEM((f, d), BF16),
        ],
    )
    return pl.pallas_call(
        _moe_expert_kernel,
        grid_spec=grid_spec,
        out_shape=jax.ShapeDtypeStruct((n_tiles * tm, d), F32),
        compiler_params=_params("arbitrary"),
        name="moe_experts",
    )(pos, tile_expert, n_used, xrows, w_gate, w_up, w_down)


def _moe_combine_kernel(pos_ref, x_ref, w_ref, ys_hbm, o_ref, buf, sem):
    i = pl.program_id(0)
    tc = x_ref.shape[0]
    n_tok = pos_ref.shape[0] // TOP_K

    def row_copy(p, slot, k, r):
        return pltpu.make_async_copy(ys_hbm.at[pl.ds(p, 1), :], buf.at[slot, k, pl.ds(r, 1), :], sem.at[slot])

    def issue(tile, slot):
        def body(r, _):
            for k in range(TOP_K):
                row_copy(pos_ref[k * n_tok + tile * tc + r], slot, k, r).start()
            return 0
        lax.fori_loop(0, tc, body, 0, unroll=DMA_UNROLL)

    def wait(slot):
        for k in range(TOP_K):
            pltpu.make_async_copy(ys_hbm.at[pl.ds(0, tc), :], buf.at[slot, k], sem.at[slot]).wait()

    @pl.when(i == 0)
    def _():
        issue(0, 0)

    slot = i % 2
    wait(slot)

    @pl.when(i + 1 < pl.num_programs(0))
    def _():
        issue(i + 1, 1 - slot)

    w = w_ref[...]
    o_ref[...] = x_ref[...] + w[:, 0:1] * buf[slot, 0] + w[:, 1:2] * buf[slot, 1]


def moe_combine(x2d, wts_tk, pos, ys, *, tc):
    t, d = x2d.shape
    grid_spec = pltpu.PrefetchScalarGridSpec(
        num_scalar_prefetch=1,
        grid=(t // tc,),
        in_specs=[
            pl.BlockSpec((tc, d), lambda i, pos: (i, 0)),
            pl.BlockSpec((tc, TOP_K), lambda i, pos: (i, 0)),
            pl.BlockSpec(memory_space=pl.ANY),
        ],
        out_specs=pl.BlockSpec((tc, d), lambda i, pos: (i, 0)),
        scratch_shapes=[pltpu.VMEM((2, TOP_K, tc, d), F32), pltpu.SemaphoreType.DMA((2,))],
    )
    return pl.pallas_call(
        _moe_combine_kernel,
        grid_spec=grid_spec,
        out_shape=jax.ShapeDtypeStruct((t, d), F32),
        compiler_params=_params("arbitrary"),
        name="moe_combine",
    )(pos, x2d, wts_tk, ys)


def moe_ffn(x2d, norm_g, router_w, router_b, layer, w_gate, w_up, w_down):
    idx, wts, xrows = router_topk(x2d, norm_g, router_w, router_b, tm=512)
    pos, tile_expert, n_used, n_tiles = moe_plan(idx, MOE_TM)
    ys = moe_experts(xrows, pos, tile_expert, n_used, n_tiles, layer, w_gate, w_up, w_down, tm=MOE_TM)
    return moe_combine(x2d, wts.T, pos, ys, tc=MOE_TC)


def kernel(x, norm_mix, norm_ffn, a_w_in, a_conv_w, a_gate_b, a_out_norm, a_w_out, b_w_in, b_kv_norm, b_w_ukv, b_q_norm, b_k_norm, b_w_out, c_w_in, c_q_norm, c_k_norm, c_w_out, router_w, router_b, moe_w_gate, moe_w_up, moe_w_down):
    bsz, s, d = x.shape
    depth = norm_mix.shape[0]
    x2d = x.reshape(bsz * s, d)
    for i in range(depth):
        kind = i % 3
        j = i // 3
        if kind == 0:
            x2d = mlstm_mixer(x2d, norm_mix[i], a_w_in[j], a_conv_w[j], a_gate_b[j], a_out_norm[j], a_w_out[j],
                              bsz, s)
        elif kind == 1:
            x2d = mla_mixer(x2d, norm_mix[i], b_w_in[j], b_kv_norm[j], b_w_ukv[j], b_q_norm[j], b_k_norm[j],
                            b_w_out[j], bsz, s)
        else:
            x2d = dilated_mixer(x2d, norm_mix[i], c_w_in[j], c_q_norm[j], c_k_norm[j], c_w_out[j], bsz, s)
        x2d = moe_ffn(x2d, norm_ffn[i], router_w, router_b, i, moe_w_gate, moe_w_up, moe_w_down)
    return x2d.reshape(bsz, s, d)
```

```python
import functools
import math

import jax
import jax.numpy as jnp
from jax import lax
from jax.experimental import pallas as pl
from jax.experimental.pallas import tpu as pltpu

F32 = jnp.float32
BF16 = jnp.bfloat16

EPS = 1e-6
LANES = 128
SUBLANES = 8
VMEM_LIMIT_BYTES = 56 * 1024 * 1024
MM_TM = 1024
MM_TN = 1024

MLSTM_HEADS = 8
MLSTM_DQK = 128
MLSTM_DV = 256
MLSTM_CHUNK = 128
MLSTM_HEADS_PER_STEP = 2

MLA_HEADS = 16
MLA_NOPE = 128
MLA_ROPE = 64
MLA_DV = 128
MLA_KV_RANK = 512
ROPE_BASE = 10000.0

DIL_GROUPS = ((128, 1), (512, 4), (2048, 16))
DIL_HEADS = 8
DIL_DH = 128
DIL_SIDE = 64

N_EXPERTS = 16
N_GROUPS = 4
EXPERTS_PER_GROUP = 4
MOE_SLOTS = EXPERTS_PER_GROUP
MOE_TM = 512
MOE_TC = 256
MOE_REC = 24
MOE_VMEM_LIMIT_BYTES = 60 * 1024 * 1024
DMA_UNROLL = 8

NT_DIMS = (((1,), (1,)), ((), ()))
TN_DIMS = (((0,), (0,)), ((), ()))


def _params(*sem):
    return pltpu.CompilerParams(dimension_semantics=sem, vmem_limit_bytes=VMEM_LIMIT_BYTES)


def _rms_scale(x, width):
    return lax.rsqrt(jnp.sum(x * x, axis=-1, keepdims=True) * (1.0 / width) + EPS)


def _norm_matmul_kernel(x_ref, g_ref, w_ref, o_ref, xn_ref):
    @pl.when(pl.program_id(1) == 0)
    def _():
        x = x_ref[...].astype(F32)
        xn_ref[...] = (x * _rms_scale(x, x.shape[-1]) * g_ref[...]).astype(BF16)

    o_ref[...] = jnp.dot(xn_ref[...], w_ref[...], preferred_element_type=F32).astype(o_ref.dtype)


def norm_matmul(x, g, w, *, kdim, tm, tn, out_dtype=F32):
    t = x.shape[0]
    n = w.shape[1]
    return pl.pallas_call(
        _norm_matmul_kernel,
        grid=(t // tm, n // tn),
        in_specs=[
            pl.BlockSpec((tm, kdim), lambda i, j: (i, 0)),
            pl.BlockSpec((1, kdim), lambda i, j: (0, 0)),
            pl.BlockSpec((kdim, tn), lambda i, j: (0, j)),
        ],
        out_specs=pl.BlockSpec((tm, tn), lambda i, j: (i, j)),
        out_shape=jax.ShapeDtypeStruct((t, n), out_dtype),
        scratch_shapes=[pltpu.VMEM((tm, kdim), BF16)],
        compiler_params=_params("parallel", "arbitrary"),
        name="norm_matmul",
    )(x, g.reshape(1, kdim), w)


def _matmul_res_kernel(a_ref, w_ref, r_ref, o_ref):
    o_ref[...] = r_ref[...] + jnp.dot(a_ref[...].astype(BF16), w_ref[...], preferred_element_type=F32)


def matmul_res(a, w, res, *, tm, tn):
    t, k = a.shape
    n = w.shape[1]
    return pl.pallas_call(
        _matmul_res_kernel,
        grid=(t // tm, n // tn),
        in_specs=[
            pl.BlockSpec((tm, k), lambda i, j: (i, 0)),
            pl.BlockSpec((k, tn), lambda i, j: (0, j)),
            pl.BlockSpec((tm, tn), lambda i, j: (i, j)),
        ],
        out_specs=pl.BlockSpec((tm, tn), lambda i, j: (i, j)),
        out_shape=jax.ShapeDtypeStruct((t, n), F32),
        compiler_params=_params("parallel", "arbitrary"),
        name="matmul_res",
    )(a, w, res)


def _log_sigmoid(x):
    return jnp.minimum(x, 0.0) - jnp.log(1.0 + jnp.exp(-jnp.abs(x)))


def _mlstm_kernel(q_ref, k_ref, v_ref, o_ref, cwq_ref, cwk_ref, gates_ref, onorm_ref, out_ref,
                  qs_ref, ks_ref, hf_ref, hb_ref):
    s = q_ref.shape[1]
    chunk = MLSTM_CHUNK
    n_chunks = s // chunk
    row = lax.broadcasted_iota(jnp.int32, (s, 1), 0)

    def conv_silu(p, w):
        prev = jnp.where(row == 0, 0.0, pltpu.roll(p, 1, axis=0))
        nxt = jnp.where(row == s - 1, 0.0, pltpu.roll(p, s - 1, axis=0))
        y = w[0:1, :] * prev + w[1:2, :] * p + w[2:3, :] * nxt
        return y * jax.nn.sigmoid(y)

    qs_ref[...] = (conv_silu(q_ref[0], cwq_ref[...]) * (MLSTM_DQK ** -0.5)).astype(BF16)
    ks_ref[...] = conv_silu(k_ref[0], cwk_ref[...]).astype(BF16)

    ti = lax.broadcasted_iota(jnp.int32, (chunk, chunk), 0)
    si = lax.broadcasted_iota(jnp.int32, (chunk, chunk), 1)
    eye = ti == si

    def to_col(r):
        return jnp.sum(jnp.where(eye, jnp.broadcast_to(r, (chunk, chunk)), 0.0), axis=1, keepdims=True)

    def to_row(c):
        return jnp.sum(jnp.where(eye, jnp.broadcast_to(c, (chunk, chunk)), 0.0), axis=0, keepdims=True)

    def chunk_step(c, hh, i_idx, f_idx, reverse, carry, hdir_ref):
        mask = (si >= ti) if reverse else (si <= ti)
        c_st, n_st, m_st = carry
        r0 = pl.multiple_of(c * chunk, chunk)
        q = qs_ref[pl.ds(r0, chunk), hh * MLSTM_DQK:(hh + 1) * MLSTM_DQK]
        k = ks_ref[pl.ds(r0, chunk), hh * MLSTM_DQK:(hh + 1) * MLSTM_DQK]
        v = v_ref[0, pl.ds(r0, chunk), hh * MLSTM_DV:(hh + 1) * MLSTM_DV].astype(BF16)
        li = gates_ref[0, hh, i_idx:i_idx + 1, pl.ds(r0, chunk)]
        lf = _log_sigmoid(gates_ref[0, hh, f_idx:f_idx + 1, pl.ds(r0, chunk)])
        b_col = jnp.sum(jnp.where(mask, jnp.broadcast_to(lf, (chunk, chunk)), 0.0), axis=1, keepdims=True)
        b_row = to_row(b_col)
        g = jnp.sum(lf, axis=1, keepdims=True)
        dlog = jnp.where(mask, b_col - b_row + li, -jnp.inf)
        m_intra = jnp.max(dlog, axis=1, keepdims=True)
        a_inter = b_col + m_st
        m = jnp.maximum(a_inter, m_intra)
        e_inter = jnp.exp(a_inter - m)
        s_qk = lax.dot_general(q, k, NT_DIMS, preferred_element_type=F32)
        dmat = jnp.exp(dlog - m) * s_qk
        num = (e_inter * jnp.dot(q, c_st.astype(BF16), preferred_element_type=F32)
               + jnp.dot(dmat.astype(BF16), v, preferred_element_type=F32))
        den = (e_inter * jnp.sum(q.astype(F32) * n_st, axis=1, keepdims=True)
               + jnp.sum(dmat, axis=1, keepdims=True))
        hdir_ref[pl.ds(r0, chunk), hh * MLSTM_DV:(hh + 1) * MLSTM_DV] = (
            num / jnp.maximum(jnp.abs(den), jnp.exp(-m)))
        w_row = g - b_row + li
        m_loc = jnp.max(w_row, axis=1, keepdims=True)
        ew_col = to_col(jnp.exp(w_row - m_loc))
        kw = k.astype(F32) * ew_col
        c_loc = lax.dot_general(kw.astype(BF16), v, TN_DIMS, preferred_element_type=F32)
        n_loc = jnp.sum(kw, axis=0, keepdims=True)
        m_new = jnp.maximum(g + m_st, m_loc)
        a = jnp.exp(g + m_st - m_new)
        e = jnp.exp(m_loc - m_new)
        return a * c_st + e * c_loc, a * n_st + e * n_loc, m_new

    def body(ci, carry):
        out = []
        for hh in range(MLSTM_HEADS_PER_STEP):
            out.append(chunk_step(ci, hh, 0, 1, False, carry[2 * hh], hf_ref))
            out.append(chunk_step(n_chunks - 1 - ci, hh, 2, 3, True, carry[2 * hh + 1], hb_ref))
        return tuple(out)

    init = (jnp.zeros((MLSTM_DQK, MLSTM_DV), F32), jnp.zeros((1, MLSTM_DQK), F32), jnp.zeros((1, 1), F32))
    lax.fori_loop(0, n_chunks, body, (init,) * (2 * MLSTM_HEADS_PER_STEP))

    for hh in range(MLSTM_HEADS_PER_STEP):
        cols = slice(hh * MLSTM_DV, (hh + 1) * MLSTM_DV)
        h = hf_ref[:, cols] + hb_ref[:, cols]
        y = h * _rms_scale(h, MLSTM_DV) * onorm_ref[:, cols]
        out_ref[0, :, cols] = (jax.nn.sigmoid(o_ref[0, :, cols]) * y).astype(out_ref.dtype)


def mlstm_core(p, conv_w, gates_rows, out_norm):
    b, s, _ = p.shape
    nh = MLSTM_HEADS
    hps = MLSTM_HEADS_PER_STEP
    wqk, wv = hps * MLSTM_DQK, hps * MLSTM_DV
    k_off = nh * MLSTM_DQK // wqk
    v_off = 2 * nh * MLSTM_DQK // wv
    o_off = v_off + nh // hps
    return pl.pallas_call(
        _mlstm_kernel,
        grid=(b, nh // hps),
        in_specs=[
            pl.BlockSpec((1, s, wqk), lambda i, h: (i, 0, h)),
            pl.BlockSpec((1, s, wqk), lambda i, h: (i, 0, k_off + h)),
            pl.BlockSpec((1, s, wv), lambda i, h: (i, 0, v_off + h)),
            pl.BlockSpec((1, s, wv), lambda i, h: (i, 0, o_off + h)),
            pl.BlockSpec((3, wqk), lambda i, h: (0, h)),
            pl.BlockSpec((3, wqk), lambda i, h: (0, k_off + h)),
            pl.BlockSpec((1, hps, 4, s), lambda i, h: (i, h, 0, 0)),
            pl.BlockSpec((1, wv), lambda i, h: (0, h)),
        ],
        out_specs=pl.BlockSpec((1, s, wv), lambda i, h: (i, 0, h)),
        out_shape=jax.ShapeDtypeStruct((b, s, nh * MLSTM_DV), BF16),
        scratch_shapes=[
            pltpu.VMEM((s, wqk), BF16),
            pltpu.VMEM((s, wqk), BF16),
            pltpu.VMEM((s, wv), F32),
            pltpu.VMEM((s, wv), F32),
        ],
        compiler_params=_params("parallel", "parallel"),
        name="mlstm_core",
    )(p, p, p, p, conv_w, conv_w, gates_rows, out_norm.reshape(1, -1))


def mlstm_mixer(x2d, norm_g, w_in, conv_w, gate_b, out_norm, w_out, bsz, s):
    nh = MLSTM_HEADS
    main_cols = 2 * nh * MLSTM_DQK + 2 * nh * MLSTM_DV
    p = norm_matmul(x2d, norm_g, w_in[:, :main_cols].astype(BF16), kdim=x2d.shape[1], tm=MM_TM, tn=MM_TN)
    gates = norm_matmul(x2d, norm_g, w_in[:, main_cols:].astype(BF16), kdim=x2d.shape[1], tm=MM_TM, tn=4 * nh)
    gates = gates + gate_b.astype(F32)
    gates_rows = gates.reshape(bsz, s, 4, nh).transpose(0, 3, 2, 1)
    hg = mlstm_core(p.reshape(bsz, s, main_cols), conv_w, gates_rows, out_norm)
    return matmul_res(hg.reshape(bsz * s, nh * MLSTM_DV), w_out.astype(BF16), x2d, tm=MM_TM, tn=MM_TN)


MLA_TQ = 256


def _mla_kernel(qn_ref, qr_ref, kv_ref, kr_ref, cos_ref, sin_ref, gqn_ref, gqr_ref, gkn_ref, gkr_ref,
                out_ref, qs_ref, ks_ref, vs_ref):
    s = qn_ref.shape[1]
    dqk = MLA_NOPE + MLA_ROPE
    half = MLA_ROPE // 2
    cos = cos_ref[...]
    sin = sin_ref[...]
    lane = lax.broadcasted_iota(jnp.int32, (1, LANES), 1)
    first_half = (lane // half) % 2 == 0

    def rope(x):
        rot = jnp.where(first_half, -pltpu.roll(x, LANES - half, axis=1), pltpu.roll(x, half, axis=1))
        return x * cos + rot * sin

    qr_raw = qr_ref[0]
    kr_raw = kr_ref[0]
    q_rope = rope(qr_raw * gqr_ref[...])
    k_rope = rope(kr_raw * gkr_ref[...])
    kr_ss = 0.5 * jnp.sum(kr_raw * kr_raw, axis=-1, keepdims=True)
    for hh in range(2):
        own = (lane // MLA_ROPE) == hh
        qn = qn_ref[0, :, hh * MLA_NOPE:(hh + 1) * MLA_NOPE]
        kn = kv_ref[0, :, hh * 2 * LANES:hh * 2 * LANES + MLA_NOPE]
        vs_ref[...] = kv_ref[0, :, hh * 2 * LANES + MLA_NOPE:(hh + 1) * 2 * LANES].astype(BF16)
        q_ss = (jnp.sum(qn * qn, axis=-1, keepdims=True)
                + jnp.sum(jnp.where(own, qr_raw * qr_raw, 0.0), axis=-1, keepdims=True))
        rq = lax.rsqrt(q_ss * (1.0 / dqk) + EPS) * (dqk ** -0.5)
        rk = lax.rsqrt((jnp.sum(kn * kn, axis=-1, keepdims=True) + kr_ss) * (1.0 / dqk) + EPS)
        qs_ref[:, 0:LANES] = (qn * rq * gqn_ref[...]).astype(BF16)
        qs_ref[:, LANES:2 * LANES] = (jnp.where(own, q_rope, 0.0) * rq).astype(BF16)
        ks_ref[:, 0:LANES] = (kn * rk * gkn_ref[...]).astype(BF16)
        ks_ref[:, LANES:2 * LANES] = (k_rope * rk).astype(BF16)

        def qblock(i, _):
            r0 = pl.multiple_of(i * MLA_TQ, MLA_TQ)
            sc = lax.dot_general(qs_ref[pl.ds(r0, MLA_TQ), :], ks_ref[...], NT_DIMS, preferred_element_type=F32)
            mx = jnp.max(sc, axis=-1, keepdims=True)
            e = jnp.exp(sc - mx)
            den = jnp.sum(e, axis=-1, keepdims=True)
            o = jnp.dot(e.astype(BF16), vs_ref[...], preferred_element_type=F32) / den
            out_ref[0, pl.ds(r0, MLA_TQ), hh * MLA_DV:(hh + 1) * MLA_DV] = o.astype(out_ref.dtype)
            return 0

        lax.fori_loop(0, s // MLA_TQ, qblock, 0, unroll=2)


def mla_attention(qall, kv, kr, cos, sin, q_norm, k_norm):
    b, s, _ = qall.shape
    nh = MLA_HEADS
    qr_off = nh * MLA_NOPE // (2 * MLA_ROPE)
    row = lambda t: t.reshape(1, -1).astype(F32)
    twice = lambda t: jnp.concatenate([t, t], axis=-1)
    vec = pl.BlockSpec((1, LANES), lambda i, h: (0, 0))
    cos = jnp.tile(cos, (1, LANES // cos.shape[1]))
    sin = jnp.tile(sin, (1, LANES // sin.shape[1]))
    return pl.pallas_call(
        _mla_kernel,
        grid=(b, nh // 2),
        in_specs=[
            pl.BlockSpec((1, s, 2 * MLA_NOPE), lambda i, h: (i, 0, h)),
            pl.BlockSpec((1, s, 2 * MLA_ROPE), lambda i, h: (i, 0, qr_off + h)),
            pl.BlockSpec((1, s, 4 * LANES), lambda i, h: (i, 0, h)),
            pl.BlockSpec((1, s, LANES), lambda i, h: (i, 0, 0)),
            pl.BlockSpec((s, LANES), lambda i, h: (0, 0)),
            pl.BlockSpec((s, LANES), lambda i, h: (0, 0)),
            vec, vec, vec, vec,
        ],
        out_specs=pl.BlockSpec((1, s, 2 * MLA_DV), lambda i, h: (i, 0, h)),
        out_shape=jax.ShapeDtypeStruct((b, s, nh * MLA_DV), BF16),
        scratch_shapes=[
            pltpu.VMEM((s, 2 * LANES), BF16),
            pltpu.VMEM((s, 2 * LANES), BF16),
            pltpu.VMEM((s, MLA_DV), BF16),
        ],
        compiler_params=_params("parallel", "parallel"),
        name="mla_attention",
    )(qall, qall, kv, twice(kr), cos, sin,
      row(q_norm[:MLA_NOPE]), twice(row(q_norm[MLA_NOPE:])), row(k_norm[:MLA_NOPE]), twice(row(k_norm[MLA_NOPE:])))


def mla_mixer(x2d, norm_g, w_in, kv_norm, w_ukv, q_norm, k_norm, w_out, bsz, s):
    nh = MLA_HEADS
    dqk = MLA_NOPE + MLA_ROPE
    d = x2d.shape[1]
    w_q = w_in[:, :nh * dqk].reshape(d, nh, dqk)
    w_qcat = jnp.concatenate([w_q[:, :, :MLA_NOPE].reshape(d, nh * MLA_NOPE),
                              w_q[:, :, MLA_NOPE:].reshape(d, nh * MLA_ROPE)], axis=1).astype(BF16)
    qall = norm_matmul(x2d, norm_g, w_qcat, kdim=d, tm=MM_TM, tn=MM_TN)
    lat = norm_matmul(x2d, norm_g, w_in[:, nh * dqk:].astype(BF16), kdim=d, tm=MM_TM, tn=MLA_KV_RANK + MLA_ROPE)
    kv = norm_matmul(lat, kv_norm, w_ukv.astype(BF16), kdim=MLA_KV_RANK, tm=MM_TM, tn=MM_TN)
    kr = lat[:, MLA_KV_RANK:]
    half = MLA_ROPE // 2
    inv = ROPE_BASE ** (-jnp.arange(half, dtype=F32) / half)
    ang = jnp.arange(s).astype(F32)[:, None] * inv[None, :]
    o = mla_attention(qall.reshape(bsz, s, -1), kv.reshape(bsz, s, -1), kr.reshape(bsz, s, MLA_ROPE),
                      jnp.cos(ang), jnp.sin(ang), q_norm, k_norm)
    return matmul_res(o.reshape(bsz * s, nh * MLA_DV), w_out.astype(BF16), x2d, tm=MM_TM, tn=MM_TN)


DIL_QB = 128
DIL_KW = 256
DIL_UNROLL = 4


def _dilated_kernel(*refs):
    n_g = len(DIL_GROUPS)
    qkv_refs = refs[:3 * n_g]
    gq_ref, gk_ref = refs[3 * n_g], refs[3 * n_g + 1]
    out_ref = refs[3 * n_g + 2]
    qs_ref, ks_ref, vs_ref = refs[3 * n_g + 3:3 * n_g + 6]
    acc_refs = refs[3 * n_g + 6:3 * n_g + 6 + n_g]
    m_refs = refs[3 * n_g + 6 + n_g:3 * n_g + 6 + 2 * n_g]
    l_refs = refs[3 * n_g + 6 + 2 * n_g:3 * n_g + 6 + 3 * n_g]
    s = out_ref.shape[1]
    head = pl.program_id(1)
    scale = DIL_DH ** -0.5

    for gi, (_, dil) in enumerate(DIL_GROUPS):
        q_ref, k_ref, v_ref = qkv_refs[3 * gi:3 * gi + 3]
        n = s // dil
        kw = min(DIL_KW, n)
        slope_arg = jnp.full((1, 1), -8.0 / (n_g * DIL_HEADS), F32) * (gi * DIL_HEADS + head + 1).astype(F32)
        bias_step = jnp.exp2(slope_arg) * float(dil)
        gq = gq_ref[gi:gi + 1, :]
        gk = gk_ref[gi:gi + 1, :]

        def residue(r, _, q_ref=q_ref, k_ref=k_ref, v_ref=v_ref, n=n, kw=kw, dil=dil, gi=gi,
                    bias_step=bias_step, gq=gq, gk=gk):
            def sub(ref):
                if dil == 1:
                    return ref[0]
                return ref[0, pl.ds(r, n, stride=dil), :]

            q = sub(q_ref)
            k = sub(k_ref)
            base = 0 if dil == 1 else pl.multiple_of(r * n, n)
            qs_ref[pl.ds(base, n), :] = (q * _rms_scale(q, DIL_DH) * gq * scale).astype(BF16)
            ks_ref[pl.ds(base, n), :] = (k * _rms_scale(k, DIL_DH) * gk).astype(BF16)
            vs_ref[pl.ds(base, n), :] = sub(v_ref).astype(BF16)
            for qb in range(n // DIL_QB):
                q0 = qb * DIL_QB
                k0 = min(max(q0 - DIL_SIDE, 0), n - kw)
                sc = lax.dot_general(qs_ref[pl.ds(base + q0, DIL_QB), :], ks_ref[pl.ds(base + k0, kw), :],
                                     NT_DIMS, preferred_element_type=F32)
                dist = jnp.abs((k0 + lax.broadcasted_iota(jnp.int32, (DIL_QB, kw), 1))
                               - (q0 + lax.broadcasted_iota(jnp.int32, (DIL_QB, kw), 0)))
                sc = sc - bias_step * dist.astype(F32)
                sc = jnp.where(dist <= DIL_SIDE, sc, -jnp.inf)
                mx = jnp.max(sc, axis=-1, keepdims=True)
                e = jnp.exp(sc - mx)
                den = jnp.sum(e, axis=-1, keepdims=True)
                acc = jnp.dot(e.astype(BF16), vs_ref[pl.ds(base + k0, kw), :], preferred_element_type=F32)
                if dil == 1:
                    rows = pl.ds(q0, DIL_QB)
                else:
                    rows = pl.ds(r + dil * q0, DIL_QB, stride=dil)
                acc_refs[gi][rows, :] = acc
                m_refs[gi][rows, :] = jnp.broadcast_to(mx, (DIL_QB, DIL_DH))
                l_refs[gi][rows, :] = jnp.broadcast_to(den, (DIL_QB, DIL_DH))
            return 0

        if dil == 1:
            residue(0, 0)
        else:
            lax.fori_loop(0, dil, residue, 0, unroll=min(dil, DIL_UNROLL))

    m_all = m_refs[0][...]
    for gi in range(1, n_g):
        m_all = jnp.maximum(m_all, m_refs[gi][...])
    num = jnp.zeros((s, DIL_DH), F32)
    den = jnp.zeros((s, DIL_DH), F32)
    for gi in range(n_g):
        w = jnp.exp(m_refs[gi][...] - m_all)
        num = num + acc_refs[gi][...] * w
        den = den + l_refs[gi][...] * w
    out_ref[0] = (num / den).astype(out_ref.dtype)


def dilated_attention(p, q_norm, k_norm):
    b, s, _ = p.shape
    n_g = len(DIL_GROUPS)
    nh = DIL_HEADS

    def spec(col0):
        return pl.BlockSpec((1, s, DIL_DH), lambda i, h: (i, 0, col0 + h))

    in_specs = [spec((gi * 3 + which) * nh) for gi in range(n_g) for which in range(3)]
    in_specs += [pl.BlockSpec((n_g, DIL_DH), lambda i, h: (0, 0))] * 2
    scratch = [pltpu.VMEM((s, DIL_DH), BF16)] * 3 + [pltpu.VMEM((s, DIL_DH), F32)] * (3 * n_g)
    return pl.pallas_call(
        _dilated_kernel,
        grid=(b, nh),
        in_specs=in_specs,
        out_specs=pl.BlockSpec((1, s, DIL_DH), lambda i, h: (i, 0, h)),
        out_shape=jax.ShapeDtypeStruct((b, s, nh * DIL_DH), BF16),
        scratch_shapes=scratch,
        compiler_params=_params("parallel", "parallel"),
        name="dilated_attention",
    )(*([p] * (3 * n_g)), q_norm.astype(F32), k_norm.astype(F32))


def dilated_mixer(x2d, norm_g, w_in, q_norm, k_norm, w_out, bsz, s):
    p = norm_matmul(x2d, norm_g, w_in.astype(BF16), kdim=x2d.shape[1], tm=MM_TM, tn=MM_TN)
    o = dilated_attention(p.reshape(bsz, s, -1), q_norm, k_norm)
    return matmul_res(o.reshape(bsz * s, -1), w_out.astype(BF16), x2d, tm=MM_TM, tn=MM_TN)


def _router_kernel(x_ref, g_ref, rw_ref, rb_ref, grp_ref, xrows_ref):
    x = x_ref[...]
    tm, d = x.shape
    xn = x * _rms_scale(x, d) * g_ref[...]
    n_sub = d // LANES
    n_rec = xrows_ref.shape[0] // tm
    for a in range(n_sub):
        xrows_ref[pl.ds(a, tm, stride=n_rec), :] = xn[:, a * LANES:(a + 1) * LANES]
    for a in range(n_sub + 1, n_rec):
        xrows_ref[pl.ds(a, tm, stride=n_rec), :] = jnp.zeros((tm, LANES), F32)
    logits = lax.dot_general(rw_ref[...], xn, NT_DIMS, precision=lax.Precision.HIGHEST,
                             preferred_element_type=F32)
    scores = jax.nn.sigmoid(logits)
    sel = scores + rb_ref[...]
    ng = N_GROUPS
    slots = range(EXPERTS_PER_GROUP)
    v = [sel[j * ng:(j + 1) * ng, :] for j in slots]
    sc = [scores[j * ng:(j + 1) * ng, :] for j in slots]

    def first_argmax(vals):
        best = vals[0]
        for t in vals[1:]:
            best = jnp.maximum(best, t)
        idx = jnp.full(best.shape, len(vals) - 1, jnp.int32)
        for j in reversed(range(len(vals) - 1)):
            idx = jnp.where(vals[j] == best, j, idx)
        return best, idx

    m1, i1 = first_argmax(v)
    m2, i2 = first_argmax([jnp.where(i1 == j, -jnp.inf, v[j]) for j in slots])
    grp_score = m1 + m2
    grp_iota = lax.broadcasted_iota(jnp.int32, grp_score.shape, 0)
    best = jnp.max(grp_score, axis=0, keepdims=True)
    grp = jnp.min(jnp.where(grp_score == best, grp_iota, ng), axis=0, keepdims=True)
    w1 = sum(jnp.where(i1 == j, sc[j], 0.0) for j in slots)
    w2 = sum(jnp.where(i2 == j, sc[j], 0.0) for j in slots)
    tot = w1 + w2
    chosen = grp_iota == grp

    def pick(v):
        return jnp.sum(jnp.where(chosen, v, jnp.zeros_like(v)), axis=0, keepdims=True)

    grp_ref[...] = grp
    gates = [pick(jnp.where(i1 == j, w1 / tot, 0.0) + jnp.where(i2 == j, w2 / tot, 0.0)) for j in slots]
    gates_t = jnp.concatenate(gates + [jnp.zeros((SUBLANES - len(gates), tm), F32)], axis=0)
    sel_row = lax.broadcasted_iota(jnp.int32, (SUBLANES, LANES), 0)
    sel_col = lax.broadcasted_iota(jnp.int32, (SUBLANES, LANES), 1)
    select = (sel_row == sel_col).astype(F32)
    xrows_ref[pl.ds(n_sub, tm, stride=n_rec), :] = lax.dot_general(
        gates_t, select, TN_DIMS, precision=lax.Precision.HIGHEST, preferred_element_type=F32)


def router_topk(x2d, norm_g, router_w, router_b, *, tm):
    t, d = x2d.shape
    ne = N_EXPERTS
    perm = jnp.arange(ne).reshape(N_GROUPS, EXPERTS_PER_GROUP).T.reshape(-1)
    rw_t = router_w.astype(F32).T[perm]
    rb = router_b.astype(F32)[perm].reshape(ne, 1)
    return pl.pallas_call(
        _router_kernel,
        grid=(t // tm,),
        in_specs=[
            pl.BlockSpec((tm, d), lambda i: (i, 0)),
            pl.BlockSpec((1, d), lambda i: (0, 0)),
            pl.BlockSpec((ne, d), lambda i: (0, 0)),
            pl.BlockSpec((ne, 1), lambda i: (0, 0)),
        ],
        out_specs=[pl.BlockSpec((1, tm), lambda i: (0, i)),
                   pl.BlockSpec((tm * MOE_REC, LANES), lambda i: (i, 0))],
        out_shape=[jax.ShapeDtypeStruct((1, t), jnp.int32),
                   jax.ShapeDtypeStruct((t * MOE_REC, LANES), F32)],
        compiler_params=_params("parallel"),
        name="router",
    )(x2d, norm_g.reshape(1, d), rw_t, rb)


def moe_plan(grp, tm):
    ne = N_GROUPS
    e_flat = grp.reshape(-1)
    n_tiles = e_flat.shape[0] // tm + ne + 1
    onehot = (e_flat[:, None] == jnp.arange(ne, dtype=jnp.int32)[None, :]).astype(jnp.int32)
    csum = jnp.cumsum(onehot, axis=0)
    counts = csum[-1]
    padded = (counts + tm - 1) // tm * tm
    ends = jnp.cumsum(padded)
    starts = ends - padded
    pos = jnp.sum(onehot * (csum - 1 + starts[None, :]), axis=1)
    tile_row0 = jnp.arange(n_tiles, dtype=jnp.int32) * tm
    tile_expert = jnp.minimum(jnp.sum((tile_row0[:, None] >= ends[None, :]).astype(jnp.int32), axis=1), ne - 1)
    n_used = (ends[-1] // tm).reshape(1)
    return pos.astype(jnp.int32), tile_expert.astype(jnp.int32), n_used.astype(jnp.int32), n_tiles


def _moe_group_kernel(pos_ref, tg_ref, nused_ref, xrows_hbm, wg_ref, wu_ref, wd_ref, ys_ref,
                      src_ref, xbuf, sem, xn_bf, gate_ref):
    i = pl.program_id(0)
    j = pl.program_id(1)
    n_slots = pl.num_programs(1)
    tm, d = xn_bf.shape
    n_sub = d // LANES
    n_rec = xbuf.shape[1] // tm
    n_used = nused_ref[0]
    n_tok = pos_ref.shape[0]
    first = j == 0

    def row_copy(tok, slot, r):
        return pltpu.make_async_copy(xrows_hbm.at[pl.ds(pl.multiple_of(tok * n_rec, SUBLANES), n_rec), :],
                                     xbuf.at[slot, pl.ds(pl.multiple_of(r * n_rec, SUBLANES), n_rec), :],
                                     sem.at[slot])

    def wait(slot):
        pltpu.make_async_copy(xrows_hbm.at[pl.ds(0, tm * n_rec), :], xbuf.at[slot], sem.at[slot]).wait()

    @pl.when((i == 0) & first)
    def _():
        def clear(r, _):
            src_ref[r] = 0
            return 0
        lax.fori_loop(0, src_ref.shape[0], clear, 0, unroll=DMA_UNROLL)

        def scatter(n, _):
            src_ref[pos_ref[n]] = n
            return 0
        lax.fori_loop(0, n_tok, scatter, 0, unroll=DMA_UNROLL)

        def head(r, _):
            row_copy(src_ref[r], 0, r).start()
            return 0
        lax.fori_loop(0, tm, head, 0, unroll=DMA_UNROLL)

    slot = i % 2

    @pl.when(first & (i <= n_used))
    def _():
        wait(slot)

    @pl.when(i < n_used)
    def _():
        @pl.when(first)
        def _():
            for a in range(n_sub):
                xn_bf[:, a * LANES:(a + 1) * LANES] = xbuf[slot, pl.ds(a, tm, stride=n_rec), :].astype(BF16)
            gate_ref[...] = xbuf[slot, pl.ds(n_sub, tm, stride=n_rec), :]

        share = tm // MOE_SLOTS
        parts = 4
        next_base = (i + 1) * tm + j * share

        def issue(q):
            for r in range(share * q // parts, share * (q + 1) // parts):
                row_copy(src_ref[next_base + r], 1 - slot, j * share + r).start()

        xn = xn_bf[...]
        issue(0)
        hg = jnp.dot(xn, wg_ref[0, 0].astype(BF16), preferred_element_type=F32)
        issue(1)
        hu = jnp.dot(xn, wu_ref[0, 0].astype(BF16), preferred_element_type=F32)
        issue(2)
        gates = gate_ref[...]
        lane = lax.broadcasted_iota(jnp.int32, gates.shape, 1)
        gate = jnp.sum(jnp.where(lane == j, gates, 0.0), axis=1, keepdims=True)
        a = hg * jax.nn.sigmoid(hg) * hu * gate
        issue(3)
        y = jnp.dot(a.astype(BF16), wd_ref[0, 0].astype(BF16), preferred_element_type=F32)

        @pl.when(first)
        def _():
            ys_ref[...] = y

        @pl.when(j > 0)
        def _():
            ys_ref[...] += y

    @pl.when(first & (i >= n_used))
    def _():
        ys_ref[...] = jnp.zeros_like(ys_ref)


def moe_experts(xrows, pos, tile_group, n_used, n_tiles, layer, w_gate, w_up, w_down, *, tm):
    d, f = w_gate.shape[-2:]
    ns = MOE_SLOTS
    wmap = lambda i, j, pos, tg, nu: (layer, tg[i] * ns + j, 0, 0)
    grid_spec = pltpu.PrefetchScalarGridSpec(
        num_scalar_prefetch=3,
        grid=(n_tiles, ns),
        in_specs=[
            pl.BlockSpec(memory_space=pl.ANY),
            pl.BlockSpec((1, 1, d, f), wmap),
            pl.BlockSpec((1, 1, d, f), wmap),
            pl.BlockSpec((1, 1, f, d), wmap),
        ],
        out_specs=pl.BlockSpec((tm, d), lambda i, j, pos, tg, nu: (i, 0)),
        scratch_shapes=[
            pltpu.SMEM((n_tiles * tm,), jnp.int32),
            pltpu.VMEM((2, tm * MOE_REC, LANES), F32),
            pltpu.SemaphoreType.DMA((2,)),
            pltpu.VMEM((tm, d), BF16),
            pltpu.VMEM((tm, LANES), F32),
        ],
    )
    return pl.pallas_call(
        _moe_group_kernel,
        grid_spec=grid_spec,
        out_shape=jax.ShapeDtypeStruct((n_tiles * tm, d), F32),
        compiler_params=pltpu.CompilerParams(dimension_semantics=("arbitrary", "arbitrary"),
                                             vmem_limit_bytes=MOE_VMEM_LIMIT_BYTES),
        name="moe_experts",
    )(pos, tile_group, n_used, xrows, w_gate, w_up, w_down)


def _moe_combine_kernel(pos_ref, x_ref, ys_hbm, o_ref, buf, sem):
    i = pl.program_id(0)
    tc = x_ref.shape[0]

    def row_copy(p, slot, r):
        return pltpu.make_async_copy(ys_hbm.at[pl.ds(p, 1), :], buf.at[slot, pl.ds(r, 1), :], sem.at[slot])

    def issue(tile, slot):
        def body(r, _):
            row_copy(pos_ref[tile * tc + r], slot, r).start()
            return 0
        lax.fori_loop(0, tc, body, 0, unroll=DMA_UNROLL)

    def wait(slot):
        pltpu.make_async_copy(ys_hbm.at[pl.ds(0, tc), :], buf.at[slot], sem.at[slot]).wait()

    @pl.when(i == 0)
    def _():
        issue(0, 0)

    slot = i % 2
    wait(slot)

    @pl.when(i + 1 < pl.num_programs(0))
    def _():
        issue(i + 1, 1 - slot)

    o_ref[...] = x_ref[...] + buf[slot]


def moe_combine(x2d, pos, ys, *, tc):
    t, d = x2d.shape
    grid_spec = pltpu.PrefetchScalarGridSpec(
        num_scalar_prefetch=1,
        grid=(t // tc,),
        in_specs=[
            pl.BlockSpec((tc, d), lambda i, pos: (i, 0)),
            pl.BlockSpec(memory_space=pl.ANY),
        ],
        out_specs=pl.BlockSpec((tc, d), lambda i, pos: (i, 0)),
        scratch_shapes=[pltpu.VMEM((2, tc, d), F32), pltpu.SemaphoreType.DMA((2,))],
    )
    return pl.pallas_call(
        _moe_combine_kernel,
        grid_spec=grid_spec,
        out_shape=jax.ShapeDtypeStruct((t, d), F32),
        compiler_params=_params("arbitrary"),
        name="moe_combine",
    )(pos, x2d, ys)


def moe_ffn(x2d, norm_g, router_w, router_b, layer, w_gate, w_up, w_down):
    grp, xrows = router_topk(x2d, norm_g, router_w, router_b, tm=512)
    pos, tile_group, n_used, n_tiles = moe_plan(grp, MOE_TM)
    ys = moe_experts(xrows, pos, tile_group, n_used, n_tiles, layer, w_gate, w_up, w_down, tm=MOE_TM)
    return moe_combine(x2d, pos, ys, tc=MOE_TC)


def kernel(x, norm_mix, norm_ffn, a_w_in, a_conv_w, a_gate_b, a_out_norm, a_w_out, b_w_in, b_kv_norm, b_w_ukv, b_q_norm, b_k_norm, b_w_out, c_w_in, c_q_norm, c_k_norm, c_w_out, router_w, router_b, moe_w_gate, moe_w_up, moe_w_down):
    bsz, s, d = x.shape
    depth = norm_mix.shape[0]
    x2d = x.reshape(bsz * s, d)
    for i in range(depth):
        kind = i % 3
        j = i // 3
        if kind == 0:
            x2d = mlstm_mixer(x2d, norm_mix[i], a_w_in[j], a_conv_w[j], a_gate_b[j], a_out_norm[j], a_w_out[j],
                              bsz, s)
        elif kind == 1:
            x2d = mla_mixer(x2d, norm_mix[i], b_w_in[j], b_kv_norm[j], b_w_ukv[j], b_q_norm[j], b_k_norm[j],
                            b_w_out[j], bsz, s)
        else:
            x2d = dilated_mixer(x2d, norm_mix[i], c_w_in[j], c_q_norm[j], c_k_norm[j], c_w_out[j], bsz, s)
        x2d = moe_ffn(x2d, norm_ffn[i], router_w, router_b, i, moe_w_gate, moe_w_up, moe_w_down)
    return x2d.reshape(bsz, s, d)
```

```python
import functools
import math

import jax
import jax.numpy as jnp
from jax import lax
from jax.experimental import pallas as pl
from jax.experimental.pallas import tpu as pltpu

F32 = jnp.float32
BF16 = jnp.bfloat16

EPS = 1e-6
LANES = 128
SUBLANES = 8
VMEM_LIMIT_BYTES = 56 * 1024 * 1024
MM_TM = 1024
MM_TN = 1024

MLSTM_HEADS = 8
MLSTM_DQK = 128
MLSTM_DV = 256
MLSTM_CHUNK = 128
MLSTM_HEADS_PER_STEP = 1

MLA_HEADS = 16
MLA_NOPE = 128
MLA_ROPE = 64
MLA_DV = 128
MLA_KV_RANK = 512
ROPE_BASE = 10000.0

DIL_GROUPS = ((128, 1), (512, 4), (2048, 16))
DIL_HEADS = 8
DIL_DH = 128
DIL_SIDE = 64

N_EXPERTS = 16
N_GROUPS = 4
EXPERTS_PER_GROUP = 4
MOE_SLOTS = EXPERTS_PER_GROUP
MOE_TM = 512
MOE_TC = 256
MOE_REC = 24
MOE_VMEM_LIMIT_BYTES = 60 * 1024 * 1024
DMA_UNROLL = 8

NT_DIMS = (((1,), (1,)), ((), ()))
TN_DIMS = (((0,), (0,)), ((), ()))


def _params(*sem):
    return pltpu.CompilerParams(dimension_semantics=sem, vmem_limit_bytes=VMEM_LIMIT_BYTES)


def _rms_scale(x, width):
    return lax.rsqrt(jnp.sum(x * x, axis=-1, keepdims=True) * (1.0 / width) + EPS)


def _norm_matmul_kernel(x_ref, g_ref, w_ref, o_ref, xn_ref):
    @pl.when(pl.program_id(1) == 0)
    def _():
        x = x_ref[...].astype(F32)
        xn_ref[...] = (x * _rms_scale(x, x.shape[-1]) * g_ref[...]).astype(BF16)

    o_ref[...] = jnp.dot(xn_ref[...], w_ref[...], preferred_element_type=F32).astype(o_ref.dtype)


def norm_matmul(x, g, w, *, kdim, tm, tn, out_dtype=F32):
    t = x.shape[0]
    n = w.shape[1]
    return pl.pallas_call(
        _norm_matmul_kernel,
        grid=(t // tm, n // tn),
        in_specs=[
            pl.BlockSpec((tm, kdim), lambda i, j: (i, 0)),
            pl.BlockSpec((1, kdim), lambda i, j: (0, 0)),
            pl.BlockSpec((kdim, tn), lambda i, j: (0, j)),
        ],
        out_specs=pl.BlockSpec((tm, tn), lambda i, j: (i, j)),
        out_shape=jax.ShapeDtypeStruct((t, n), out_dtype),
        scratch_shapes=[pltpu.VMEM((tm, kdim), BF16)],
        compiler_params=_params("parallel", "arbitrary"),
        name="norm_matmul",
    )(x, g.reshape(1, kdim), w)


def _matmul_res_kernel(a_ref, w_ref, r_ref, o_ref):
    o_ref[...] = r_ref[...] + jnp.dot(a_ref[...].astype(BF16), w_ref[...], preferred_element_type=F32)


def matmul_res(a, w, res, *, tm, tn):
    t, k = a.shape
    n = w.shape[1]
    return pl.pallas_call(
        _matmul_res_kernel,
        grid=(t // tm, n // tn),
        in_specs=[
            pl.BlockSpec((tm, k), lambda i, j: (i, 0)),
            pl.BlockSpec((k, tn), lambda i, j: (0, j)),
            pl.BlockSpec((tm, tn), lambda i, j: (i, j)),
        ],
        out_specs=pl.BlockSpec((tm, tn), lambda i, j: (i, j)),
        out_shape=jax.ShapeDtypeStruct((t, n), F32),
        compiler_params=_params("parallel", "arbitrary"),
        name="matmul_res",
    )(a, w, res)


def _log_sigmoid(x):
    return jnp.minimum(x, 0.0) - jnp.log(1.0 + jnp.exp(-jnp.abs(x)))


def _mlstm_kernel(q_ref, k_ref, v_ref, o_ref, cwq_ref, cwk_ref, gates_ref, onorm_ref, out_ref,
                  qs_ref, ks_ref, hf_ref, hb_ref):
    s = q_ref.shape[1]
    chunk = MLSTM_CHUNK
    n_chunks = s // chunk
    row = lax.broadcasted_iota(jnp.int32, (s, 1), 0)

    def conv_silu(p, w):
        prev = jnp.where(row == 0, 0.0, pltpu.roll(p, 1, axis=0))
        nxt = jnp.where(row == s - 1, 0.0, pltpu.roll(p, s - 1, axis=0))
        y = w[0:1, :] * prev + w[1:2, :] * p + w[2:3, :] * nxt
        return y * jax.nn.sigmoid(y)

    qs_ref[...] = (conv_silu(q_ref[0], cwq_ref[...]) * (MLSTM_DQK ** -0.5)).astype(BF16)
    ks_ref[...] = conv_silu(k_ref[0], cwk_ref[...]).astype(BF16)

    ti = lax.broadcasted_iota(jnp.int32, (chunk, chunk), 0)
    si = lax.broadcasted_iota(jnp.int32, (chunk, chunk), 1)
    eye = ti == si

    def to_col(r):
        return jnp.sum(jnp.where(eye, jnp.broadcast_to(r, (chunk, chunk)), 0.0), axis=1, keepdims=True)

    def to_row(c):
        return jnp.sum(jnp.where(eye, jnp.broadcast_to(c, (chunk, chunk)), 0.0), axis=0, keepdims=True)

    def chunk_step(c, hh, i_idx, f_idx, reverse, carry, hdir_ref):
        mask = (si >= ti) if reverse else (si <= ti)
        c_st, n_st, m_st = carry
        r0 = pl.multiple_of(c * chunk, chunk)
        q = qs_ref[pl.ds(r0, chunk), hh * MLSTM_DQK:(hh + 1) * MLSTM_DQK]
        k = ks_ref[pl.ds(r0, chunk), hh * MLSTM_DQK:(hh + 1) * MLSTM_DQK]
        v = v_ref[0, pl.ds(r0, chunk), hh * MLSTM_DV:(hh + 1) * MLSTM_DV].astype(BF16)
        li = gates_ref[0, hh, i_idx:i_idx + 1, pl.ds(r0, chunk)]
        lf = _log_sigmoid(gates_ref[0, hh, f_idx:f_idx + 1, pl.ds(r0, chunk)])
        b_col = jnp.sum(jnp.where(mask, jnp.broadcast_to(lf, (chunk, chunk)), 0.0), axis=1, keepdims=True)
        b_row = to_row(b_col)
        g = jnp.sum(lf, axis=1, keepdims=True)
        dlog = jnp.where(mask, b_col - b_row + li, -jnp.inf)
        m_intra = jnp.max(dlog, axis=1, keepdims=True)
        a_inter = b_col + m_st
        m = jnp.maximum(a_inter, m_intra)
        e_inter = jnp.exp(a_inter - m)
        s_qk = lax.dot_general(q, k, NT_DIMS, preferred_element_type=F32)
        dmat = jnp.exp(dlog - m) * s_qk
        num = (e_inter * jnp.dot(q, c_st.astype(BF16), preferred_element_type=F32)
               + jnp.dot(dmat.astype(BF16), v, preferred_element_type=F32))
        den = (e_inter * jnp.sum(q.astype(F32) * n_st, axis=1, keepdims=True)
               + jnp.sum(dmat, axis=1, keepdims=True))
        hdir_ref[pl.ds(r0, chunk), hh * MLSTM_DV:(hh + 1) * MLSTM_DV] = (
            num / jnp.maximum(jnp.abs(den), jnp.exp(-m)))
        w_row = g - b_row + li
        m_loc = jnp.max(w_row, axis=1, keepdims=True)
        ew_col = to_col(jnp.exp(w_row - m_loc))
        kw = k.astype(F32) * ew_col
        c_loc = lax.dot_general(kw.astype(BF16), v, TN_DIMS, preferred_element_type=F32)
        n_loc = jnp.sum(kw, axis=0, keepdims=True)
        m_new = jnp.maximum(g + m_st, m_loc)
        a = jnp.exp(g + m_st - m_new)
        e = jnp.exp(m_loc - m_new)
        return a * c_st + e * c_loc, a * n_st + e * n_loc, m_new

    def body(ci, carry):
        out = []
        for hh in range(MLSTM_HEADS_PER_STEP):
            out.append(chunk_step(ci, hh, 0, 1, False, carry[2 * hh], hf_ref))
            out.append(chunk_step(n_chunks - 1 - ci, hh, 2, 3, True, carry[2 * hh + 1], hb_ref))
        return tuple(out)

    init = (jnp.zeros((MLSTM_DQK, MLSTM_DV), F32), jnp.zeros((1, MLSTM_DQK), F32), jnp.zeros((1, 1), F32))
    lax.fori_loop(0, n_chunks, body, (init,) * (2 * MLSTM_HEADS_PER_STEP))

    for hh in range(MLSTM_HEADS_PER_STEP):
        cols = slice(hh * MLSTM_DV, (hh + 1) * MLSTM_DV)
        h = hf_ref[:, cols] + hb_ref[:, cols]
        y = h * _rms_scale(h, MLSTM_DV) * onorm_ref[:, cols]
        out_ref[0, :, cols] = (jax.nn.sigmoid(o_ref[0, :, cols]) * y).astype(out_ref.dtype)


def mlstm_core(p, conv_w, gates_rows, out_norm):
    b, s, _ = p.shape
    nh = MLSTM_HEADS
    hps = MLSTM_HEADS_PER_STEP
    wqk, wv = hps * MLSTM_DQK, hps * MLSTM_DV
    k_off = nh * MLSTM_DQK // wqk
    v_off = 2 * nh * MLSTM_DQK // wv
    o_off = v_off + nh // hps
    return pl.pallas_call(
        _mlstm_kernel,
        grid=(b, nh // hps),
        in_specs=[
            pl.BlockSpec((1, s, wqk), lambda i, h: (i, 0, h)),
            pl.BlockSpec((1, s, wqk), lambda i, h: (i, 0, k_off + h)),
            pl.BlockSpec((1, s, wv), lambda i, h: (i, 0, v_off + h)),
            pl.BlockSpec((1, s, wv), lambda i, h: (i, 0, o_off + h)),
            pl.BlockSpec((3, wqk), lambda i, h: (0, h)),
            pl.BlockSpec((3, wqk), lambda i, h: (0, k_off + h)),
            pl.BlockSpec((1, hps, 4, s), lambda i, h: (i, h, 0, 0)),
            pl.BlockSpec((1, wv), lambda i, h: (0, h)),
        ],
        out_specs=pl.BlockSpec((1, s, wv), lambda i, h: (i, 0, h)),
        out_shape=jax.ShapeDtypeStruct((b, s, nh * MLSTM_DV), BF16),
        scratch_shapes=[
            pltpu.VMEM((s, wqk), BF16),
            pltpu.VMEM((s, wqk), BF16),
            pltpu.VMEM((s, wv), F32),
            pltpu.VMEM((s, wv), F32),
        ],
        compiler_params=_params("parallel", "parallel"),
        name="mlstm_core",
    )(p, p, p, p, conv_w, conv_w, gates_rows, out_norm.reshape(1, -1))


def mlstm_mixer(x2d, norm_g, w_in, conv_w, gate_b, out_norm, w_out, bsz, s):
    nh = MLSTM_HEADS
    main_cols = 2 * nh * MLSTM_DQK + 2 * nh * MLSTM_DV
    p = norm_matmul(x2d, norm_g, w_in[:, :main_cols].astype(BF16), kdim=x2d.shape[1], tm=MM_TM, tn=MM_TN)
    gates = norm_matmul(x2d, norm_g, w_in[:, main_cols:].astype(BF16), kdim=x2d.shape[1], tm=MM_TM, tn=4 * nh)
    gates = gates + gate_b.astype(F32)
    gates_rows = gates.reshape(bsz, s, 4, nh).transpose(0, 3, 2, 1)
    hg = mlstm_core(p.reshape(bsz, s, main_cols), conv_w, gates_rows, out_norm)
    return matmul_res(hg.reshape(bsz * s, nh * MLSTM_DV), w_out.astype(BF16), x2d, tm=MM_TM, tn=MM_TN)


MLA_TQ = 256


def _mla_kernel(qn_ref, qr_ref, kv_ref, kr_ref, cos_ref, sin_ref, gqn_ref, gqr_ref, gkn_ref, gkr_ref,
                out_ref, qs_ref, ks_ref, vs_ref):
    s = qn_ref.shape[1]
    dqk = MLA_NOPE + MLA_ROPE
    half = MLA_ROPE // 2
    cos = cos_ref[...]
    sin = sin_ref[...]
    lane = lax.broadcasted_iota(jnp.int32, (1, LANES), 1)
    first_half = (lane // half) % 2 == 0

    def rope(x):
        rot = jnp.where(first_half, -pltpu.roll(x, LANES - half, axis=1), pltpu.roll(x, half, axis=1))
        return x * cos + rot * sin

    qr_raw = qr_ref[0]
    kr_raw = kr_ref[0]
    q_rope = rope(qr_raw * gqr_ref[...])
    k_rope = rope(kr_raw * gkr_ref[...])
    kr_ss = 0.5 * jnp.sum(kr_raw * kr_raw, axis=-1, keepdims=True)
    for hh in range(2):
        own = (lane // MLA_ROPE) == hh
        qn = qn_ref[0, :, hh * MLA_NOPE:(hh + 1) * MLA_NOPE]
        kn = kv_ref[0, :, hh * 2 * LANES:hh * 2 * LANES + MLA_NOPE]
        vs_ref[...] = kv_ref[0, :, hh * 2 * LANES + MLA_NOPE:(hh + 1) * 2 * LANES].astype(BF16)
        q_ss = (jnp.sum(qn * qn, axis=-1, keepdims=True)
                + jnp.sum(jnp.where(own, qr_raw * qr_raw, 0.0), axis=-1, keepdims=True))
        rq = lax.rsqrt(q_ss * (1.0 / dqk) + EPS) * (dqk ** -0.5)
        rk = lax.rsqrt((jnp.sum(kn * kn, axis=-1, keepdims=True) + kr_ss) * (1.0 / dqk) + EPS)
        qs_ref[:, 0:LANES] = (qn * rq * gqn_ref[...]).astype(BF16)
        qs_ref[:, LANES:2 * LANES] = (jnp.where(own, q_rope, 0.0) * rq).astype(BF16)
        ks_ref[:, 0:LANES] = (kn * rk * gkn_ref[...]).astype(BF16)
        ks_ref[:, LANES:2 * LANES] = (k_rope * rk).astype(BF16)

        def qblock(i, _):
            r0 = pl.multiple_of(i * MLA_TQ, MLA_TQ)
            sc = lax.dot_general(qs_ref[pl.ds(r0, MLA_TQ), :], ks_ref[...], NT_DIMS, preferred_element_type=F32)
            mx = jnp.max(sc, axis=-1, keepdims=True)
            e = jnp.exp(sc - mx)
            den = jnp.sum(e, axis=-1, keepdims=True)
            o = jnp.dot(e.astype(BF16), vs_ref[...], preferred_element_type=F32) / den
            out_ref[0, pl.ds(r0, MLA_TQ), hh * MLA_DV:(hh + 1) * MLA_DV] = o.astype(out_ref.dtype)
            return 0

        lax.fori_loop(0, s // MLA_TQ, qblock, 0, unroll=2)


def mla_attention(qall, kv, kr, cos, sin, q_norm, k_norm):
    b, s, _ = qall.shape
    nh = MLA_HEADS
    qr_off = nh * MLA_NOPE // (2 * MLA_ROPE)
    row = lambda t: t.reshape(1, -1).astype(F32)
    twice = lambda t: jnp.concatenate([t, t], axis=-1)
    vec = pl.BlockSpec((1, LANES), lambda i, h: (0, 0))
    cos = jnp.tile(cos, (1, LANES // cos.shape[1]))
    sin = jnp.tile(sin, (1, LANES // sin.shape[1]))
    return pl.pallas_call(
        _mla_kernel,
        grid=(b, nh // 2),
        in_specs=[
            pl.BlockSpec((1, s, 2 * MLA_NOPE), lambda i, h: (i, 0, h)),
            pl.BlockSpec((1, s, 2 * MLA_ROPE), lambda i, h: (i, 0, qr_off + h)),
            pl.BlockSpec((1, s, 4 * LANES), lambda i, h: (i, 0, h)),
            pl.BlockSpec((1, s, LANES), lambda i, h: (i, 0, 0)),
            pl.BlockSpec((s, LANES), lambda i, h: (0, 0)),
            pl.BlockSpec((s, LANES), lambda i, h: (0, 0)),
            vec, vec, vec, vec,
        ],
        out_specs=pl.BlockSpec((1, s, 2 * MLA_DV), lambda i, h: (i, 0, h)),
        out_shape=jax.ShapeDtypeStruct((b, s, nh * MLA_DV), BF16),
        scratch_shapes=[
            pltpu.VMEM((s, 2 * LANES), BF16),
            pltpu.VMEM((s, 2 * LANES), BF16),
            pltpu.VMEM((s, MLA_DV), BF16),
        ],
        compiler_params=_params("parallel", "parallel"),
        name="mla_attention",
    )(qall, qall, kv, twice(kr), cos, sin,
      row(q_norm[:MLA_NOPE]), twice(row(q_norm[MLA_NOPE:])), row(k_norm[:MLA_NOPE]), twice(row(k_norm[MLA_NOPE:])))


def mla_mixer(x2d, norm_g, w_in, kv_norm, w_ukv, q_norm, k_norm, w_out, bsz, s):
    nh = MLA_HEADS
    dqk = MLA_NOPE + MLA_ROPE
    d = x2d.shape[1]
    w_q = w_in[:, :nh * dqk].reshape(d, nh, dqk)
    w_qcat = jnp.concatenate([w_q[:, :, :MLA_NOPE].reshape(d, nh * MLA_NOPE),
                              w_q[:, :, MLA_NOPE:].reshape(d, nh * MLA_ROPE)], axis=1).astype(BF16)
    qall = norm_matmul(x2d, norm_g, w_qcat, kdim=d, tm=MM_TM, tn=MM_TN)
    lat = norm_matmul(x2d, norm_g, w_in[:, nh * dqk:].astype(BF16), kdim=d, tm=MM_TM, tn=MLA_KV_RANK + MLA_ROPE)
    kv = norm_matmul(lat, kv_norm, w_ukv.astype(BF16), kdim=MLA_KV_RANK, tm=MM_TM, tn=MM_TN)
    kr = lat[:, MLA_KV_RANK:]
    half = MLA_ROPE // 2
    inv = ROPE_BASE ** (-jnp.arange(half, dtype=F32) / half)
    ang = jnp.arange(s).astype(F32)[:, None] * inv[None, :]
    o = mla_attention(qall.reshape(bsz, s, -1), kv.reshape(bsz, s, -1), kr.reshape(bsz, s, MLA_ROPE),
                      jnp.cos(ang), jnp.sin(ang), q_norm, k_norm)
    return matmul_res(o.reshape(bsz * s, nh * MLA_DV), w_out.astype(BF16), x2d, tm=MM_TM, tn=MM_TN)


DIL_QB = 128
DIL_KW = 256
DIL_UNROLL = 8


def _dilated_kernel(*refs):
    n_g = len(DIL_GROUPS)
    qkv_refs = refs[:3 * n_g]
    gq_ref, gk_ref = refs[3 * n_g], refs[3 * n_g + 1]
    out_ref = refs[3 * n_g + 2]
    qs_ref, ks_ref, vs_ref = refs[3 * n_g + 3:3 * n_g + 6]
    acc_refs = refs[3 * n_g + 6:3 * n_g + 6 + n_g]
    m_refs = refs[3 * n_g + 6 + n_g:3 * n_g + 6 + 2 * n_g]
    l_refs = refs[3 * n_g + 6 + 2 * n_g:3 * n_g + 6 + 3 * n_g]
    s = out_ref.shape[1]
    head = pl.program_id(1)
    scale = DIL_DH ** -0.5

    for gi, (_, dil) in enumerate(DIL_GROUPS):
        q_ref, k_ref, v_ref = qkv_refs[3 * gi:3 * gi + 3]
        n = s // dil
        kw = min(DIL_KW, n)
        slope_arg = jnp.full((1, 1), -8.0 / (n_g * DIL_HEADS), F32) * (gi * DIL_HEADS + head + 1).astype(F32)
        bias_step = jnp.exp2(slope_arg) * float(dil)
        gq = gq_ref[gi:gi + 1, :]
        gk = gk_ref[gi:gi + 1, :]

        def residue(r, _, q_ref=q_ref, k_ref=k_ref, v_ref=v_ref, n=n, kw=kw, dil=dil, gi=gi,
                    bias_step=bias_step, gq=gq, gk=gk):
            def sub(ref):
                if dil == 1:
                    return ref[0]
                return ref[0, pl.ds(r, n, stride=dil), :]

            q = sub(q_ref)
            k = sub(k_ref)
            base = 0 if dil == 1 else pl.multiple_of(r * n, n)
            qs_ref[pl.ds(base, n), :] = (q * _rms_scale(q, DIL_DH) * gq * scale).astype(BF16)
            ks_ref[pl.ds(base, n), :] = (k * _rms_scale(k, DIL_DH) * gk).astype(BF16)
            vs_ref[pl.ds(base, n), :] = sub(v_ref).astype(BF16)
            for qb in range(n // DIL_QB):
                q0 = qb * DIL_QB
                k0 = min(max(q0 - DIL_SIDE, 0), n - kw)
                sc = lax.dot_general(qs_ref[pl.ds(base + q0, DIL_QB), :], ks_ref[pl.ds(base + k0, kw), :],
                                     NT_DIMS, preferred_element_type=F32)
                dist = jnp.abs((k0 + lax.broadcasted_iota(jnp.int32, (DIL_QB, kw), 1))
                               - (q0 + lax.broadcasted_iota(jnp.int32, (DIL_QB, kw), 0)))
                sc = sc - bias_step * dist.astype(F32)
                sc = jnp.where(dist <= DIL_SIDE, sc, -jnp.inf)
                mx = jnp.max(sc, axis=-1, keepdims=True)
                e = jnp.exp(sc - mx)
                den = jnp.sum(e, axis=-1, keepdims=True)
                acc = jnp.dot(e.astype(BF16), vs_ref[pl.ds(base + k0, kw), :], preferred_element_type=F32)
                if dil == 1:
                    rows = pl.ds(q0, DIL_QB)
                else:
                    rows = pl.ds(r + dil * q0, DIL_QB, stride=dil)
                acc_refs[gi][rows, :] = acc
                m_refs[gi][rows, :] = jnp.broadcast_to(mx, (DIL_QB, DIL_DH))
                l_refs[gi][rows, :] = jnp.broadcast_to(den, (DIL_QB, DIL_DH))
            return 0

        if dil == 1:
            residue(0, 0)
        else:
            lax.fori_loop(0, dil, residue, 0, unroll=min(dil, DIL_UNROLL))

    m_all = m_refs[0][...]
    for gi in range(1, n_g):
        m_all = jnp.maximum(m_all, m_refs[gi][...])
    num = jnp.zeros((s, DIL_DH), F32)
    den = jnp.zeros((s, DIL_DH), F32)
    for gi in range(n_g):
        w = jnp.exp(m_refs[gi][...] - m_all)
        num = num + acc_refs[gi][...] * w
        den = den + l_refs[gi][...] * w
    out_ref[0] = (num / den).astype(out_ref.dtype)


def dilated_attention(p, q_norm, k_norm):
    b, s, _ = p.shape
    n_g = len(DIL_GROUPS)
    nh = DIL_HEADS

    def spec(col0):
        return pl.BlockSpec((1, s, DIL_DH), lambda i, h: (i, 0, col0 + h))

    in_specs = [spec((gi * 3 + which) * nh) for gi in range(n_g) for which in range(3)]
    in_specs += [pl.BlockSpec((n_g, DIL_DH), lambda i, h: (0, 0))] * 2
    scratch = [pltpu.VMEM((s, DIL_DH), BF16)] * 3 + [pltpu.VMEM((s, DIL_DH), F32)] * (3 * n_g)
    return pl.pallas_call(
        _dilated_kernel,
        grid=(b, nh),
        in_specs=in_specs,
        out_specs=pl.BlockSpec((1, s, DIL_DH), lambda i, h: (i, 0, h)),
        out_shape=jax.ShapeDtypeStruct((b, s, nh * DIL_DH), BF16),
        scratch_shapes=scratch,
        compiler_params=_params("parallel", "parallel"),
        name="dilated_attention",
    )(*([p] * (3 * n_g)), q_norm.astype(F32), k_norm.astype(F32))


def dilated_mixer(x2d, norm_g, w_in, q_norm, k_norm, w_out, bsz, s):
    p = norm_matmul(x2d, norm_g, w_in.astype(BF16), kdim=x2d.shape[1], tm=MM_TM, tn=MM_TN)
    o = dilated_attention(p.reshape(bsz, s, -1), q_norm, k_norm)
    return matmul_res(o.reshape(bsz * s, -1), w_out.astype(BF16), x2d, tm=MM_TM, tn=MM_TN)


def _router_kernel(x_ref, g_ref, rw_ref, rb_ref, grp_ref, xrows_ref):
    x = x_ref[...]
    tm, d = x.shape
    xn = x * _rms_scale(x, d) * g_ref[...]
    n_sub = d // LANES
    n_rec = xrows_ref.shape[0] // tm
    for a in range(n_sub):
        xrows_ref[pl.ds(a, tm, stride=n_rec), :] = xn[:, a * LANES:(a + 1) * LANES]
    for a in range(n_sub + 1, n_rec):
        xrows_ref[pl.ds(a, tm, stride=n_rec), :] = jnp.zeros((tm, LANES), F32)
    logits = lax.dot_general(rw_ref[...], xn, NT_DIMS, precision=lax.Precision.HIGHEST,
                             preferred_element_type=F32)
    scores = jax.nn.sigmoid(logits)
    sel = scores + rb_ref[...]
    ng = N_GROUPS
    slots = range(EXPERTS_PER_GROUP)
    v = [sel[j * ng:(j + 1) * ng, :] for j in slots]
    sc = [scores[j * ng:(j + 1) * ng, :] for j in slots]

    def first_argmax(vals):
        best = vals[0]
        for t in vals[1:]:
            best = jnp.maximum(best, t)
        idx = jnp.full(best.shape, len(vals) - 1, jnp.int32)
        for j in reversed(range(len(vals) - 1)):
            idx = jnp.where(vals[j] == best, j, idx)
        return best, idx

    m1, i1 = first_argmax(v)
    m2, i2 = first_argmax([jnp.where(i1 == j, -jnp.inf, v[j]) for j in slots])
    grp_score = m1 + m2
    grp_iota = lax.broadcasted_iota(jnp.int32, grp_score.shape, 0)
    best = jnp.max(grp_score, axis=0, keepdims=True)
    grp = jnp.min(jnp.where(grp_score == best, grp_iota, ng), axis=0, keepdims=True)
    w1 = sum(jnp.where(i1 == j, sc[j], 0.0) for j in slots)
    w2 = sum(jnp.where(i2 == j, sc[j], 0.0) for j in slots)
    tot = w1 + w2
    chosen = grp_iota == grp

    def pick(v):
        return jnp.sum(jnp.where(chosen, v, jnp.zeros_like(v)), axis=0, keepdims=True)

    grp_ref[...] = grp
    gates = [pick(jnp.where(i1 == j, w1 / tot, 0.0) + jnp.where(i2 == j, w2 / tot, 0.0)) for j in slots]
    gates_t = jnp.concatenate(gates + [jnp.zeros((SUBLANES - len(gates), tm), F32)], axis=0)
    sel_row = lax.broadcasted_iota(jnp.int32, (SUBLANES, LANES), 0)
    sel_col = lax.broadcasted_iota(jnp.int32, (SUBLANES, LANES), 1)
    select = (sel_row == sel_col).astype(F32)
    xrows_ref[pl.ds(n_sub, tm, stride=n_rec), :] = lax.dot_general(
        gates_t, select, TN_DIMS, precision=lax.Precision.HIGHEST, preferred_element_type=F32)


def router_topk(x2d, norm_g, router_w, router_b, *, tm):
    t, d = x2d.shape
    ne = N_EXPERTS
    perm = jnp.arange(ne).reshape(N_GROUPS, EXPERTS_PER_GROUP).T.reshape(-1)
    rw_t = router_w.astype(F32).T[perm]
    rb = router_b.astype(F32)[perm].reshape(ne, 1)
    return pl.pallas_call(
        _router_kernel,
        grid=(t // tm,),
        in_specs=[
            pl.BlockSpec((tm, d), lambda i: (i, 0)),
            pl.BlockSpec((1, d), lambda i: (0, 0)),
            pl.BlockSpec((ne, d), lambda i: (0, 0)),
            pl.BlockSpec((ne, 1), lambda i: (0, 0)),
        ],
        out_specs=[pl.BlockSpec((1, tm), lambda i: (0, i)),
                   pl.BlockSpec((tm * MOE_REC, LANES), lambda i: (i, 0))],
        out_shape=[jax.ShapeDtypeStruct((1, t), jnp.int32),
                   jax.ShapeDtypeStruct((t * MOE_REC, LANES), F32)],
        compiler_params=_params("parallel"),
        name="router",
    )(x2d, norm_g.reshape(1, d), rw_t, rb)


def moe_plan(grp, tm):
    ne = N_GROUPS
    e_flat = grp.reshape(-1)
    n_tiles = -(-e_flat.shape[0] // tm) + ne + 1
    onehot = (e_flat[:, None] == jnp.arange(ne, dtype=jnp.int32)[None, :]).astype(jnp.int32)
    csum = jnp.cumsum(onehot, axis=0)
    counts = csum[-1]
    padded = (counts + tm - 1) // tm * tm
    ends = jnp.cumsum(padded)
    starts = ends - padded
    pos = jnp.sum(onehot * (csum - 1 + starts[None, :]), axis=1)
    tile_row0 = jnp.arange(n_tiles, dtype=jnp.int32) * tm
    tile_expert = jnp.minimum(jnp.sum((tile_row0[:, None] >= ends[None, :]).astype(jnp.int32), axis=1), ne - 1)
    n_used = (ends[-1] // tm).reshape(1)
    return pos.astype(jnp.int32), tile_expert.astype(jnp.int32), n_used.astype(jnp.int32), n_tiles


def _moe_group_kernel(pos_ref, tg_ref, nused_ref, xrows_hbm, wg_ref, wu_ref, wd_ref, ys_ref,
                      src_ref, xbuf, sem, xn_bf, gate_ref):
    i = pl.program_id(0)
    j = pl.program_id(1)
    n_slots = pl.num_programs(1)
    tm, d = xn_bf.shape
    n_sub = d // LANES
    n_rec = xbuf.shape[1] // tm
    n_used = nused_ref[0]
    n_tok = pos_ref.shape[0]
    first = j == 0

    def row_copy(tok, slot, r):
        return pltpu.make_async_copy(xrows_hbm.at[pl.ds(pl.multiple_of(tok * n_rec, SUBLANES), n_rec), :],
                                     xbuf.at[slot, pl.ds(pl.multiple_of(r * n_rec, SUBLANES), n_rec), :],
                                     sem.at[slot])

    def wait(slot):
        pltpu.make_async_copy(xrows_hbm.at[pl.ds(0, tm * n_rec), :], xbuf.at[slot], sem.at[slot]).wait()

    @pl.when((i == 0) & first)
    def _():
        def clear(r, _):
            src_ref[r] = 0
            return 0
        lax.fori_loop(0, src_ref.shape[0], clear, 0, unroll=DMA_UNROLL)

        def scatter(n, _):
            src_ref[pos_ref[n]] = n
            return 0
        lax.fori_loop(0, n_tok, scatter, 0, unroll=DMA_UNROLL)

        def head(r, _):
            row_copy(src_ref[r], 0, r).start()
            return 0
        lax.fori_loop(0, tm, head, 0, unroll=DMA_UNROLL)

    slot = i % 2

    @pl.when(first & (i <= n_used))
    def _():
        wait(slot)

    @pl.when(i < n_used)
    def _():
        @pl.when(first)
        def _():
            for a in range(n_sub):
                xn_bf[:, a * LANES:(a + 1) * LANES] = xbuf[slot, pl.ds(a, tm, stride=n_rec), :].astype(BF16)
            gate_ref[...] = xbuf[slot, pl.ds(n_sub, tm, stride=n_rec), :]

        share = tm // MOE_SLOTS

        def issue(r, _):
            rr = j * share + r
            row_copy(src_ref[(i + 1) * tm + rr], 1 - slot, rr).start()
            return 0
        lax.fori_loop(0, share, issue, 0, unroll=DMA_UNROLL)

        xn = xn_bf[...]
        hg = jnp.dot(xn, wg_ref[0, 0].astype(BF16), preferred_element_type=F32)
        hu = jnp.dot(xn, wu_ref[0, 0].astype(BF16), preferred_element_type=F32)
        gates = gate_ref[...]
        lane = lax.broadcasted_iota(jnp.int32, gates.shape, 1)
        gate = jnp.sum(jnp.where(lane == j, gates, 0.0), axis=1, keepdims=True)
        a = hg * jax.nn.sigmoid(hg) * hu * gate
        y = jnp.dot(a.astype(BF16), wd_ref[0, 0].astype(BF16), preferred_element_type=F32)

        @pl.when(first)
        def _():
            ys_ref[...] = y

        @pl.when(j > 0)
        def _():
            ys_ref[...] += y

    @pl.when(first & (i >= n_used))
    def _():
        ys_ref[...] = jnp.zeros_like(ys_ref)


def moe_experts(xrows, pos, tile_group, n_used, n_tiles, layer, w_gate, w_up, w_down, *, tm):
    d, f = w_gate.shape[-2:]
    ns = MOE_SLOTS
    wmap = lambda i, j, pos, tg, nu: (layer, tg[i] * ns + j, 0, 0)
    grid_spec = pltpu.PrefetchScalarGridSpec(
        num_scalar_prefetch=3,
        grid=(n_tiles, ns),
        in_specs=[
            pl.BlockSpec(memory_space=pl.ANY),
            pl.BlockSpec((1, 1, d, f), wmap),
            pl.BlockSpec((1, 1, d, f), wmap),
            pl.BlockSpec((1, 1, f, d), wmap),
        ],
        out_specs=pl.BlockSpec((tm, d), lambda i, j, pos, tg, nu: (i, 0)),
        scratch_shapes=[
            pltpu.SMEM((n_tiles * tm,), jnp.int32),
            pltpu.VMEM((2, tm * MOE_REC, LANES), F32),
            pltpu.SemaphoreType.DMA((2,)),
            pltpu.VMEM((tm, d), BF16),
            pltpu.VMEM((tm, LANES), F32),
        ],
    )
    return pl.pallas_call(
        _moe_group_kernel,
        grid_spec=grid_spec,
        out_shape=jax.ShapeDtypeStruct((n_tiles * tm, d), F32),
        compiler_params=pltpu.CompilerParams(dimension_semantics=("arbitrary", "arbitrary"),
                                             vmem_limit_bytes=MOE_VMEM_LIMIT_BYTES),
        name="moe_experts",
    )(pos, tile_group, n_used, xrows, w_gate, w_up, w_down)


def _moe_combine_kernel(pos_ref, x_ref, ys_hbm, o_ref, buf, sem):
    i = pl.program_id(0)
    tc = x_ref.shape[0]

    def row_copy(p, slot, r):
        return pltpu.make_async_copy(ys_hbm.at[pl.ds(p, 1), :], buf.at[slot, pl.ds(r, 1), :], sem.at[slot])

    def issue(tile, slot):
        def body(r, _):
            row_copy(pos_ref[tile * tc + r], slot, r).start()
            return 0
        lax.fori_loop(0, tc, body, 0, unroll=DMA_UNROLL)

    def wait(slot):
        pltpu.make_async_copy(ys_hbm.at[pl.ds(0, tc), :], buf.at[slot], sem.at[slot]).wait()

    @pl.when(i == 0)
    def _():
        issue(0, 0)

    slot = i % 2
    wait(slot)

    @pl.when(i + 1 < pl.num_programs(0))
    def _():
        issue(i + 1, 1 - slot)

    o_ref[...] = x_ref[...] + buf[slot]


def moe_combine(x2d, pos, ys, *, tc):
    t, d = x2d.shape
    grid_spec = pltpu.PrefetchScalarGridSpec(
        num_scalar_prefetch=1,
        grid=(t // tc,),
        in_specs=[
            pl.BlockSpec((tc, d), lambda i, pos: (i, 0)),
            pl.BlockSpec(memory_space=pl.ANY),
        ],
        out_specs=pl.BlockSpec((tc, d), lambda i, pos: (i, 0)),
        scratch_shapes=[pltpu.VMEM((2, tc, d), F32), pltpu.SemaphoreType.DMA((2,))],
    )
    return pl.pallas_call(
        _moe_combine_kernel,
        grid_spec=grid_spec,
        out_shape=jax.ShapeDtypeStruct((t, d), F32),
        compiler_params=_params("arbitrary"),
        name="moe_combine",
    )(pos, x2d, ys)


def moe_ffn(x2d, norm_g, router_w, router_b, layer, w_gate, w_up, w_down):
    grp, xrows = router_topk(x2d, norm_g, router_w, router_b, tm=512)
    pos, tile_group, n_used, n_tiles = moe_plan(grp, MOE_TM)
    ys = moe_experts(xrows, pos, tile_group, n_used, n_tiles, layer, w_gate, w_up, w_down, tm=MOE_TM)
    return moe_combine(x2d, pos, ys, tc=MOE_TC)


def kernel(x, norm_mix, norm_ffn, a_w_in, a_conv_w, a_gate_b, a_out_norm, a_w_out, b_w_in, b_kv_norm, b_w_ukv, b_q_norm, b_k_norm, b_w_out, c_w_in, c_q_norm, c_k_norm, c_w_out, router_w, router_b, moe_w_gate, moe_w_up, moe_w_down):
    bsz, s, d = x.shape
    depth = norm_mix.shape[0]
    x2d = x.reshape(bsz * s, d)
    for i in range(depth):
        kind = i % 3
        j = i // 3
        if kind == 0:
            x2d = mlstm_mixer(x2d, norm_mix[i], a_w_in[j], a_conv_w[j], a_gate_b[j], a_out_norm[j], a_w_out[j],
                              bsz, s)
        elif kind == 1:
            x2d = mla_mixer(x2d, norm_mix[i], b_w_in[j], b_kv_norm[j], b_w_ukv[j], b_q_norm[j], b_k_norm[j],
                            b_w_out[j], bsz, s)
        else:
            x2d = dilated_mixer(x2d, norm_mix[i], c_w_in[j], c_q_norm[j], c_k_norm[j], c_w_out[j], bsz, s)
        x2d = moe_ffn(x2d, norm_ffn[i], router_w, router_b, i, moe_w_gate, moe_w_up, moe_w_down)
    return x2d.reshape(bsz, s, d)
```

```python
import functools
import math

import jax
import jax.numpy as jnp
from jax import lax
from jax.experimental import pallas as pl
from jax.experimental.pallas import tpu as pltpu

F32 = jnp.float32
BF16 = jnp.bfloat16

EPS = 1e-6
LANES = 128
SUBLANES = 8
VMEM_LIMIT_BYTES = 56 * 1024 * 1024
MM_TM = 1024
MM_TN = 1024

MLSTM_HEADS = 8
MLSTM_DQK = 128
MLSTM_DV = 256
MLSTM_CHUNK = 128
MLSTM_HEADS_PER_STEP = 1

MLA_HEADS = 16
MLA_NOPE = 128
MLA_ROPE = 64
MLA_DV = 128
MLA_KV_RANK = 512
ROPE_BASE = 10000.0

DIL_GROUPS = ((128, 1), (512, 4), (2048, 16))
DIL_HEADS = 8
DIL_DH = 128
DIL_SIDE = 64

N_EXPERTS = 16
N_GROUPS = 4
EXPERTS_PER_GROUP = 4
MOE_SLOTS = EXPERTS_PER_GROUP
MOE_TM = 512
MOE_TC = 256
MOE_REC = 24
MOE_VMEM_LIMIT_BYTES = 60 * 1024 * 1024
DMA_UNROLL = 8

NT_DIMS = (((1,), (1,)), ((), ()))
TN_DIMS = (((0,), (0,)), ((), ()))


def _params(*sem):
    return pltpu.CompilerParams(dimension_semantics=sem, vmem_limit_bytes=VMEM_LIMIT_BYTES)


def _rms_scale(x, width):
    return lax.rsqrt(jnp.sum(x * x, axis=-1, keepdims=True) * (1.0 / width) + EPS)


def _norm_matmul_kernel(x_ref, g_ref, w_ref, o_ref, xn_ref):
    @pl.when(pl.program_id(1) == 0)
    def _():
        x = x_ref[...].astype(F32)
        xn_ref[...] = (x * _rms_scale(x, x.shape[-1]) * g_ref[...]).astype(BF16)

    o_ref[...] = jnp.dot(xn_ref[...], w_ref[...], preferred_element_type=F32).astype(o_ref.dtype)


def norm_matmul(x, g, w, *, kdim, tm, tn, out_dtype=F32):
    t = x.shape[0]
    n = w.shape[1]
    return pl.pallas_call(
        _norm_matmul_kernel,
        grid=(t // tm, n // tn),
        in_specs=[
            pl.BlockSpec((tm, kdim), lambda i, j: (i, 0)),
            pl.BlockSpec((1, kdim), lambda i, j: (0, 0)),
            pl.BlockSpec((kdim, tn), lambda i, j: (0, j)),
        ],
        out_specs=pl.BlockSpec((tm, tn), lambda i, j: (i, j)),
        out_shape=jax.ShapeDtypeStruct((t, n), out_dtype),
        scratch_shapes=[pltpu.VMEM((tm, kdim), BF16)],
        compiler_params=_params("parallel", "arbitrary"),
        name="norm_matmul",
    )(x, g.reshape(1, kdim), w)


def _matmul_res_kernel(a_ref, w_ref, r_ref, o_ref):
    o_ref[...] = r_ref[...] + jnp.dot(a_ref[...].astype(BF16), w_ref[...], preferred_element_type=F32)


def matmul_res(a, w, res, *, tm, tn):
    t, k = a.shape
    n = w.shape[1]
    return pl.pallas_call(
        _matmul_res_kernel,
        grid=(t // tm, n // tn),
        in_specs=[
            pl.BlockSpec((tm, k), lambda i, j: (i, 0)),
            pl.BlockSpec((k, tn), lambda i, j: (0, j)),
            pl.BlockSpec((tm, tn), lambda i, j: (i, j)),
        ],
        out_specs=pl.BlockSpec((tm, tn), lambda i, j: (i, j)),
        out_shape=jax.ShapeDtypeStruct((t, n), F32),
        compiler_params=_params("parallel", "arbitrary"),
        name="matmul_res",
    )(a, w, res)


def _log_sigmoid(x):
    return jnp.minimum(x, 0.0) - jnp.log(1.0 + jnp.exp(-jnp.abs(x)))


def _mlstm_kernel(q_ref, k_ref, v_ref, o_ref, cwq_ref, cwk_ref, gates_ref, onorm_ref, out_ref,
                  qs_ref, ks_ref, hf_ref, hb_ref):
    s = q_ref.shape[1]
    chunk = MLSTM_CHUNK
    n_chunks = s // chunk
    row = lax.broadcasted_iota(jnp.int32, (s, 1), 0)

    def conv_silu(p, w):
        prev = jnp.where(row == 0, 0.0, pltpu.roll(p, 1, axis=0))
        nxt = jnp.where(row == s - 1, 0.0, pltpu.roll(p, s - 1, axis=0))
        y = w[0:1, :] * prev + w[1:2, :] * p + w[2:3, :] * nxt
        return y * jax.nn.sigmoid(y)

    qs_ref[...] = (conv_silu(q_ref[0], cwq_ref[...]) * (MLSTM_DQK ** -0.5)).astype(BF16)
    ks_ref[...] = conv_silu(k_ref[0], cwk_ref[...]).astype(BF16)

    ti = lax.broadcasted_iota(jnp.int32, (chunk, chunk), 0)
    si = lax.broadcasted_iota(jnp.int32, (chunk, chunk), 1)
    eye = ti == si

    def to_col(r):
        return jnp.sum(jnp.where(eye, jnp.broadcast_to(r, (chunk, chunk)), 0.0), axis=1, keepdims=True)

    def to_row(c):
        return jnp.sum(jnp.where(eye, jnp.broadcast_to(c, (chunk, chunk)), 0.0), axis=0, keepdims=True)

    def chunk_step(c, hh, i_idx, f_idx, reverse, carry, hdir_ref):
        mask = (si >= ti) if reverse else (si <= ti)
        c_st, n_st, m_st = carry
        r0 = pl.multiple_of(c * chunk, chunk)
        q = qs_ref[pl.ds(r0, chunk), hh * MLSTM_DQK:(hh + 1) * MLSTM_DQK]
        k = ks_ref[pl.ds(r0, chunk), hh * MLSTM_DQK:(hh + 1) * MLSTM_DQK]
        v = v_ref[0, pl.ds(r0, chunk), hh * MLSTM_DV:(hh + 1) * MLSTM_DV].astype(BF16)
        li = gates_ref[0, hh, i_idx:i_idx + 1, pl.ds(r0, chunk)]
        lf = _log_sigmoid(gates_ref[0, hh, f_idx:f_idx + 1, pl.ds(r0, chunk)])
        b_col = jnp.sum(jnp.where(mask, jnp.broadcast_to(lf, (chunk, chunk)), 0.0), axis=1, keepdims=True)
        b_row = to_row(b_col)
        g = jnp.sum(lf, axis=1, keepdims=True)
        dlog = jnp.where(mask, b_col - b_row + li, -jnp.inf)
        m_intra = jnp.max(dlog, axis=1, keepdims=True)
        a_inter = b_col + m_st
        m = jnp.maximum(a_inter, m_intra)
        e_inter = jnp.exp(a_inter - m)
        s_qk = lax.dot_general(q, k, NT_DIMS, preferred_element_type=F32)
        dmat = jnp.exp(dlog - m) * s_qk
        num = (e_inter * jnp.dot(q, c_st.astype(BF16), preferred_element_type=F32)
               + jnp.dot(dmat.astype(BF16), v, preferred_element_type=F32))
        den = (e_inter * jnp.sum(q.astype(F32) * n_st, axis=1, keepdims=True)
               + jnp.sum(dmat, axis=1, keepdims=True))
        hdir_ref[pl.ds(r0, chunk), hh * MLSTM_DV:(hh + 1) * MLSTM_DV] = (
            num / jnp.maximum(jnp.abs(den), jnp.exp(-m)))
        w_row = g - b_row + li
        m_loc = jnp.max(w_row, axis=1, keepdims=True)
        ew_col = to_col(jnp.exp(w_row - m_loc))
        kw = k.astype(F32) * ew_col
        c_loc = lax.dot_general(kw.astype(BF16), v, TN_DIMS, preferred_element_type=F32)
        n_loc = jnp.sum(kw, axis=0, keepdims=True)
        m_new = jnp.maximum(g + m_st, m_loc)
        a = jnp.exp(g + m_st - m_new)
        e = jnp.exp(m_loc - m_new)
        return a * c_st + e * c_loc, a * n_st + e * n_loc, m_new

    def body(ci, carry):
        out = []
        for hh in range(MLSTM_HEADS_PER_STEP):
            out.append(chunk_step(ci, hh, 0, 1, False, carry[2 * hh], hf_ref))
            out.append(chunk_step(n_chunks - 1 - ci, hh, 2, 3, True, carry[2 * hh + 1], hb_ref))
        return tuple(out)

    init = (jnp.zeros((MLSTM_DQK, MLSTM_DV), F32), jnp.zeros((1, MLSTM_DQK), F32), jnp.zeros((1, 1), F32))
    lax.fori_loop(0, n_chunks, body, (init,) * (2 * MLSTM_HEADS_PER_STEP), unroll=2)

    for hh in range(MLSTM_HEADS_PER_STEP):
        cols = slice(hh * MLSTM_DV, (hh + 1) * MLSTM_DV)
        h = hf_ref[:, cols] + hb_ref[:, cols]
        y = h * _rms_scale(h, MLSTM_DV) * onorm_ref[:, cols]
        out_ref[0, :, cols] = (jax.nn.sigmoid(o_ref[0, :, cols]) * y).astype(out_ref.dtype)


def mlstm_core(p, conv_w, gates_rows, out_norm):
    b, s, _ = p.shape
    nh = MLSTM_HEADS
    hps = MLSTM_HEADS_PER_STEP
    wqk, wv = hps * MLSTM_DQK, hps * MLSTM_DV
    k_off = nh * MLSTM_DQK // wqk
    v_off = 2 * nh * MLSTM_DQK // wv
    o_off = v_off + nh // hps
    return pl.pallas_call(
        _mlstm_kernel,
        grid=(b, nh // hps),
        in_specs=[
            pl.BlockSpec((1, s, wqk), lambda i, h: (i, 0, h)),
            pl.BlockSpec((1, s, wqk), lambda i, h: (i, 0, k_off + h)),
            pl.BlockSpec((1, s, wv), lambda i, h: (i, 0, v_off + h)),
            pl.BlockSpec((1, s, wv), lambda i, h: (i, 0, o_off + h)),
            pl.BlockSpec((3, wqk), lambda i, h: (0, h)),
            pl.BlockSpec((3, wqk), lambda i, h: (0, k_off + h)),
            pl.BlockSpec((1, hps, 4, s), lambda i, h: (i, h, 0, 0)),
            pl.BlockSpec((1, wv), lambda i, h: (0, h)),
        ],
        out_specs=pl.BlockSpec((1, s, wv), lambda i, h: (i, 0, h)),
        out_shape=jax.ShapeDtypeStruct((b, s, nh * MLSTM_DV), BF16),
        scratch_shapes=[
            pltpu.VMEM((s, wqk), BF16),
            pltpu.VMEM((s, wqk), BF16),
            pltpu.VMEM((s, wv), F32),
            pltpu.VMEM((s, wv), F32),
        ],
        compiler_params=_params("parallel", "parallel"),
        name="mlstm_core",
    )(p, p, p, p, conv_w, conv_w, gates_rows, out_norm.reshape(1, -1))


def mlstm_mixer(x2d, norm_g, w_in, conv_w, gate_b, out_norm, w_out, bsz, s):
    nh = MLSTM_HEADS
    main_cols = 2 * nh * MLSTM_DQK + 2 * nh * MLSTM_DV
    p = norm_matmul(x2d, norm_g, w_in[:, :main_cols].astype(BF16), kdim=x2d.shape[1], tm=MM_TM, tn=MM_TN)
    gates = norm_matmul(x2d, norm_g, w_in[:, main_cols:].astype(BF16), kdim=x2d.shape[1], tm=MM_TM, tn=4 * nh)
    gates = gates + gate_b.astype(F32)
    gates_rows = gates.reshape(bsz, s, 4, nh).transpose(0, 3, 2, 1)
    hg = mlstm_core(p.reshape(bsz, s, main_cols), conv_w, gates_rows, out_norm)
    return matmul_res(hg.reshape(bsz * s, nh * MLSTM_DV), w_out.astype(BF16), x2d, tm=MM_TM, tn=MM_TN)


MLA_TQ = 256


def _mla_kernel(qn_ref, qr_ref, kv_ref, kr_ref, cos_ref, sin_ref, gqn_ref, gqr_ref, gkn_ref, gkr_ref,
                out_ref, qs_ref, ks_ref, vs_ref):
    s = qn_ref.shape[1]
    dqk = MLA_NOPE + MLA_ROPE
    half = MLA_ROPE // 2
    cos = cos_ref[...]
    sin = sin_ref[...]
    lane = lax.broadcasted_iota(jnp.int32, (1, LANES), 1)
    first_half = (lane // half) % 2 == 0

    def rope(x):
        rot = jnp.where(first_half, -pltpu.roll(x, LANES - half, axis=1), pltpu.roll(x, half, axis=1))
        return x * cos + rot * sin

    qr_raw = qr_ref[0]
    kr_raw = kr_ref[0]
    q_rope = rope(qr_raw * gqr_ref[...])
    k_rope = rope(kr_raw * gkr_ref[...])
    kr_ss = 0.5 * jnp.sum(kr_raw * kr_raw, axis=-1, keepdims=True)
    for hh in range(2):
        own = (lane // MLA_ROPE) == hh
        qn = qn_ref[0, :, hh * MLA_NOPE:(hh + 1) * MLA_NOPE]
        kn = kv_ref[0, :, hh * 2 * LANES:hh * 2 * LANES + MLA_NOPE]
        vs_ref[...] = kv_ref[0, :, hh * 2 * LANES + MLA_NOPE:(hh + 1) * 2 * LANES].astype(BF16)
        q_ss = (jnp.sum(qn * qn, axis=-1, keepdims=True)
                + jnp.sum(jnp.where(own, qr_raw * qr_raw, 0.0), axis=-1, keepdims=True))
        rq = lax.rsqrt(q_ss * (1.0 / dqk) + EPS) * (dqk ** -0.5)
        rk = lax.rsqrt((jnp.sum(kn * kn, axis=-1, keepdims=True) + kr_ss) * (1.0 / dqk) + EPS)
        qs_ref[:, 0:LANES] = (qn * rq * gqn_ref[...]).astype(BF16)
        qs_ref[:, LANES:2 * LANES] = (jnp.where(own, q_rope, 0.0) * rq).astype(BF16)
        ks_ref[:, 0:LANES] = (kn * rk * gkn_ref[...]).astype(BF16)
        ks_ref[:, LANES:2 * LANES] = (k_rope * rk).astype(BF16)

        def qblock(i, _):
            r0 = pl.multiple_of(i * MLA_TQ, MLA_TQ)
            sc = lax.dot_general(qs_ref[pl.ds(r0, MLA_TQ), :], ks_ref[...], NT_DIMS, preferred_element_type=F32)
            mx = jnp.max(sc, axis=-1, keepdims=True)
            e = jnp.exp(sc - mx)
            den = jnp.sum(e, axis=-1, keepdims=True)
            o = jnp.dot(e.astype(BF16), vs_ref[...], preferred_element_type=F32) / den
            out_ref[0, pl.ds(r0, MLA_TQ), hh * MLA_DV:(hh + 1) * MLA_DV] = o.astype(out_ref.dtype)
            return 0

        lax.fori_loop(0, s // MLA_TQ, qblock, 0, unroll=2)


def mla_attention(qall, kv, kr, cos, sin, q_norm, k_norm):
    b, s, _ = qall.shape
    nh = MLA_HEADS
    qr_off = nh * MLA_NOPE // (2 * MLA_ROPE)
    row = lambda t: t.reshape(1, -1).astype(F32)
    twice = lambda t: jnp.concatenate([t, t], axis=-1)
    vec = pl.BlockSpec((1, LANES), lambda i, h: (0, 0))
    cos = jnp.tile(cos, (1, LANES // cos.shape[1]))
    sin = jnp.tile(sin, (1, LANES // sin.shape[1]))
    return pl.pallas_call(
        _mla_kernel,
        grid=(b, nh // 2),
        in_specs=[
            pl.BlockSpec((1, s, 2 * MLA_NOPE), lambda i, h: (i, 0, h)),
            pl.BlockSpec((1, s, 2 * MLA_ROPE), lambda i, h: (i, 0, qr_off + h)),
            pl.BlockSpec((1, s, 4 * LANES), lambda i, h: (i, 0, h)),
            pl.BlockSpec((1, s, LANES), lambda i, h: (i, 0, 0)),
            pl.BlockSpec((s, LANES), lambda i, h: (0, 0)),
            pl.BlockSpec((s, LANES), lambda i, h: (0, 0)),
            vec, vec, vec, vec,
        ],
        out_specs=pl.BlockSpec((1, s, 2 * MLA_DV), lambda i, h: (i, 0, h)),
        out_shape=jax.ShapeDtypeStruct((b, s, nh * MLA_DV), BF16),
        scratch_shapes=[
            pltpu.VMEM((s, 2 * LANES), BF16),
            pltpu.VMEM((s, 2 * LANES), BF16),
            pltpu.VMEM((s, MLA_DV), BF16),
        ],
        compiler_params=_params("parallel", "parallel"),
        name="mla_attention",
    )(qall, qall, kv, twice(kr), cos, sin,
      row(q_norm[:MLA_NOPE]), twice(row(q_norm[MLA_NOPE:])), row(k_norm[:MLA_NOPE]), twice(row(k_norm[MLA_NOPE:])))


def mla_mixer(x2d, norm_g, w_in, kv_norm, w_ukv, q_norm, k_norm, w_out, bsz, s):
    nh = MLA_HEADS
    dqk = MLA_NOPE + MLA_ROPE
    d = x2d.shape[1]
    w_q = w_in[:, :nh * dqk].reshape(d, nh, dqk)
    w_qcat = jnp.concatenate([w_q[:, :, :MLA_NOPE].reshape(d, nh * MLA_NOPE),
                              w_q[:, :, MLA_NOPE:].reshape(d, nh * MLA_ROPE)], axis=1).astype(BF16)
    qall = norm_matmul(x2d, norm_g, w_qcat, kdim=d, tm=MM_TM, tn=MM_TN)
    lat = norm_matmul(x2d, norm_g, w_in[:, nh * dqk:].astype(BF16), kdim=d, tm=MM_TM, tn=MLA_KV_RANK + MLA_ROPE)
    kv = norm_matmul(lat, kv_norm, w_ukv.astype(BF16), kdim=MLA_KV_RANK, tm=MM_TM, tn=MM_TN)
    kr = lat[:, MLA_KV_RANK:]
    half = MLA_ROPE // 2
    inv = ROPE_BASE ** (-jnp.arange(half, dtype=F32) / half)
    ang = jnp.arange(s).astype(F32)[:, None] * inv[None, :]
    o = mla_attention(qall.reshape(bsz, s, -1), kv.reshape(bsz, s, -1), kr.reshape(bsz, s, MLA_ROPE),
                      jnp.cos(ang), jnp.sin(ang), q_norm, k_norm)
    return matmul_res(o.reshape(bsz * s, nh * MLA_DV), w_out.astype(BF16), x2d, tm=MM_TM, tn=MM_TN)


DIL_QB = 128
DIL_KW = 256
DIL_UNROLL = 8


def _dilated_kernel(*refs):
    n_g = len(DIL_GROUPS)
    qkv_refs = refs[:3 * n_g]
    gq_ref, gk_ref = refs[3 * n_g], refs[3 * n_g + 1]
    out_ref = refs[3 * n_g + 2]
    qs_ref, ks_ref, vs_ref = refs[3 * n_g + 3:3 * n_g + 6]
    acc_refs = refs[3 * n_g + 6:3 * n_g + 6 + n_g]
    m_refs = refs[3 * n_g + 6 + n_g:3 * n_g + 6 + 2 * n_g]
    l_refs = refs[3 * n_g + 6 + 2 * n_g:3 * n_g + 6 + 3 * n_g]
    s = out_ref.shape[1]
    head = pl.program_id(1)
    scale = DIL_DH ** -0.5

    for gi, (_, dil) in enumerate(DIL_GROUPS):
        q_ref, k_ref, v_ref = qkv_refs[3 * gi:3 * gi + 3]
        n = s // dil
        kw = min(DIL_KW, n)
        slope_arg = jnp.full((1, 1), -8.0 / (n_g * DIL_HEADS), F32) * (gi * DIL_HEADS + head + 1).astype(F32)
        bias_step = jnp.exp2(slope_arg) * float(dil)
        gq = gq_ref[gi:gi + 1, :]
        gk = gk_ref[gi:gi + 1, :]

        def residue(r, _, q_ref=q_ref, k_ref=k_ref, v_ref=v_ref, n=n, kw=kw, dil=dil, gi=gi,
                    bias_step=bias_step, gq=gq, gk=gk):
            def sub(ref):
                if dil == 1:
                    return ref[0]
                return ref[0, pl.ds(r, n, stride=dil), :]

            q = sub(q_ref)
            k = sub(k_ref)
            base = 0 if dil == 1 else pl.multiple_of(r * n, n)
            qs_ref[pl.ds(base, n), :] = (q * _rms_scale(q, DIL_DH) * gq * scale).astype(BF16)
            ks_ref[pl.ds(base, n), :] = (k * _rms_scale(k, DIL_DH) * gk).astype(BF16)
            vs_ref[pl.ds(base, n), :] = sub(v_ref).astype(BF16)
            for qb in range(n // DIL_QB):
                q0 = qb * DIL_QB
                k0 = min(max(q0 - DIL_SIDE, 0), n - kw)
                sc = lax.dot_general(qs_ref[pl.ds(base + q0, DIL_QB), :], ks_ref[pl.ds(base + k0, kw), :],
                                     NT_DIMS, preferred_element_type=F32)
                dist = jnp.abs((k0 + lax.broadcasted_iota(jnp.int32, (DIL_QB, kw), 1))
                               - (q0 + lax.broadcasted_iota(jnp.int32, (DIL_QB, kw), 0)))
                sc = sc - bias_step * dist.astype(F32)
                sc = jnp.where(dist <= DIL_SIDE, sc, -jnp.inf)
                mx = jnp.max(sc, axis=-1, keepdims=True)
                e = jnp.exp(sc - mx)
                den = jnp.sum(e, axis=-1, keepdims=True)
                acc = jnp.dot(e.astype(BF16), vs_ref[pl.ds(base + k0, kw), :], preferred_element_type=F32)
                if dil == 1:
                    rows = pl.ds(q0, DIL_QB)
                else:
                    rows = pl.ds(r + dil * q0, DIL_QB, stride=dil)
                acc_refs[gi][rows, :] = acc
                m_refs[gi][rows, :] = jnp.broadcast_to(mx, (DIL_QB, DIL_DH))
                l_refs[gi][rows, :] = jnp.broadcast_to(den, (DIL_QB, DIL_DH))
            return 0

        if dil == 1:
            residue(0, 0)
        else:
            lax.fori_loop(0, dil, residue, 0, unroll=min(dil, DIL_UNROLL))

    m_all = m_refs[0][...]
    for gi in range(1, n_g):
        m_all = jnp.maximum(m_all, m_refs[gi][...])
    num = jnp.zeros((s, DIL_DH), F32)
    den = jnp.zeros((s, DIL_DH), F32)
    for gi in range(n_g):
        w = jnp.exp(m_refs[gi][...] - m_all)
        num = num + acc_refs[gi][...] * w
        den = den + l_refs[gi][...] * w
    out_ref[0] = (num / den).astype(out_ref.dtype)


def dilated_attention(p, q_norm, k_norm):
    b, s, _ = p.shape
    n_g = len(DIL_GROUPS)
    nh = DIL_HEADS

    def spec(col0):
        return pl.BlockSpec((1, s, DIL_DH), lambda i, h: (i, 0, col0 + h))

    in_specs = [spec((gi * 3 + which) * nh) for gi in range(n_g) for which in range(3)]
    in_specs += [pl.BlockSpec((n_g, DIL_DH), lambda i, h: (0, 0))] * 2
    scratch = [pltpu.VMEM((s, DIL_DH), BF16)] * 3 + [pltpu.VMEM((s, DIL_DH), F32)] * (3 * n_g)
    return pl.pallas_call(
        _dilated_kernel,
        grid=(b, nh),
        in_specs=in_specs,
        out_specs=pl.BlockSpec((1, s, DIL_DH), lambda i, h: (i, 0, h)),
        out_shape=jax.ShapeDtypeStruct((b, s, nh * DIL_DH), BF16),
        scratch_shapes=scratch,
        compiler_params=_params("parallel", "parallel"),
        name="dilated_attention",
    )(*([p] * (3 * n_g)), q_norm.astype(F32), k_norm.astype(F32))


def dilated_mixer(x2d, norm_g, w_in, q_norm, k_norm, w_out, bsz, s):
    p = norm_matmul(x2d, norm_g, w_in.astype(BF16), kdim=x2d.shape[1], tm=MM_TM, tn=MM_TN)
    o = dilated_attention(p.reshape(bsz, s, -1), q_norm, k_norm)
    return matmul_res(o.reshape(bsz * s, -1), w_out.astype(BF16), x2d, tm=MM_TM, tn=MM_TN)


def _router_kernel(x_ref, g_ref, rw_ref, rb_ref, grp_ref, xrows_ref):
    x = x_ref[...]
    tm, d = x.shape
    xn = x * _rms_scale(x, d) * g_ref[...]
    n_sub = d // LANES
    n_rec = xrows_ref.shape[0] // tm
    for a in range(n_sub):
        xrows_ref[pl.ds(a, tm, stride=n_rec), :] = xn[:, a * LANES:(a + 1) * LANES]
    for a in range(n_sub + 1, n_rec):
        xrows_ref[pl.ds(a, tm, stride=n_rec), :] = jnp.zeros((tm, LANES), F32)
    def split(v):
        hi = v.astype(BF16)
        return hi, (v - hi.astype(F32)).astype(BF16)

    def nt(a, b):
        return lax.dot_general(a, b, NT_DIMS, preferred_element_type=F32)

    x_hi, x_lo = split(xn)
    w_hi, w_lo = split(rw_ref[...])
    logits = nt(w_hi, x_hi) + (nt(w_hi, x_lo) + nt(w_lo, x_hi))
    scores = jax.nn.sigmoid(logits)
    sel = scores + rb_ref[...]
    ng = N_GROUPS
    slots = range(EXPERTS_PER_GROUP)
    v = [sel[j * ng:(j + 1) * ng, :] for j in slots]
    sc = [scores[j * ng:(j + 1) * ng, :] for j in slots]

    def first_argmax(vals):
        best = vals[0]
        for t in vals[1:]:
            best = jnp.maximum(best, t)
        idx = jnp.full(best.shape, len(vals) - 1, jnp.int32)
        for j in reversed(range(len(vals) - 1)):
            idx = jnp.where(vals[j] == best, j, idx)
        return best, idx

    m1, i1 = first_argmax(v)
    m2, i2 = first_argmax([jnp.where(i1 == j, -jnp.inf, v[j]) for j in slots])
    grp_score = m1 + m2
    grp_iota = lax.broadcasted_iota(jnp.int32, grp_score.shape, 0)
    best = jnp.max(grp_score, axis=0, keepdims=True)
    grp = jnp.min(jnp.where(grp_score == best, grp_iota, ng), axis=0, keepdims=True)
    w1 = sum(jnp.where(i1 == j, sc[j], 0.0) for j in slots)
    w2 = sum(jnp.where(i2 == j, sc[j], 0.0) for j in slots)
    tot = w1 + w2
    chosen = grp_iota == grp

    def pick(v):
        return jnp.sum(jnp.where(chosen, v, jnp.zeros_like(v)), axis=0, keepdims=True)

    grp_ref[...] = grp
    gates = [pick(jnp.where(i1 == j, w1 / tot, 0.0) + jnp.where(i2 == j, w2 / tot, 0.0)) for j in slots]
    gates_t = jnp.concatenate(gates + [jnp.zeros((SUBLANES - len(gates), tm), F32)], axis=0)
    sel_row = lax.broadcasted_iota(jnp.int32, (SUBLANES, LANES), 0)
    sel_col = lax.broadcasted_iota(jnp.int32, (SUBLANES, LANES), 1)
    select = (sel_row == sel_col).astype(F32)
    xrows_ref[pl.ds(n_sub, tm, stride=n_rec), :] = lax.dot_general(
        gates_t, select, TN_DIMS, precision=lax.Precision.HIGHEST, preferred_element_type=F32)


def router_topk(x2d, norm_g, router_w, router_b, *, tm):
    t, d = x2d.shape
    ne = N_EXPERTS
    perm = jnp.arange(ne).reshape(N_GROUPS, EXPERTS_PER_GROUP).T.reshape(-1)
    rw_t = router_w.astype(F32).T[perm]
    rb = router_b.astype(F32)[perm].reshape(ne, 1)
    return pl.pallas_call(
        _router_kernel,
        grid=(t // tm,),
        in_specs=[
            pl.BlockSpec((tm, d), lambda i: (i, 0)),
            pl.BlockSpec((1, d), lambda i: (0, 0)),
            pl.BlockSpec((ne, d), lambda i: (0, 0)),
            pl.BlockSpec((ne, 1), lambda i: (0, 0)),
        ],
        out_specs=[pl.BlockSpec((1, tm), lambda i: (0, i)),
                   pl.BlockSpec((tm * MOE_REC, LANES), lambda i: (i, 0))],
        out_shape=[jax.ShapeDtypeStruct((1, t), jnp.int32),
                   jax.ShapeDtypeStruct((t * MOE_REC, LANES), F32)],
        compiler_params=_params("parallel"),
        name="router",
    )(x2d, norm_g.reshape(1, d), rw_t, rb)


def moe_plan(grp, tm):
    ne = N_GROUPS
    e_flat = grp.reshape(-1)
    n_tiles = -(-e_flat.shape[0] // tm) + ne + 1
    onehot = (e_flat[:, None] == jnp.arange(ne, dtype=jnp.int32)[None, :]).astype(jnp.int32)
    csum = jnp.cumsum(onehot, axis=0)
    counts = csum[-1]
    padded = (counts + tm - 1) // tm * tm
    ends = jnp.cumsum(padded)
    starts = ends - padded
    pos = jnp.sum(onehot * (csum - 1 + starts[None, :]), axis=1)
    tile_row0 = jnp.arange(n_tiles, dtype=jnp.int32) * tm
    tile_expert = jnp.minimum(jnp.sum((tile_row0[:, None] >= ends[None, :]).astype(jnp.int32), axis=1), ne - 1)
    n_used = (ends[-1] // tm).reshape(1)
    return pos.astype(jnp.int32), tile_expert.astype(jnp.int32), n_used.astype(jnp.int32), n_tiles


def _moe_group_kernel(pos_ref, tg_ref, nused_ref, xrows_hbm, wg_ref, wu_ref, wd_ref, ys_ref,
                      src_ref, xbuf, sem, xn_bf, gate_ref):
    i = pl.program_id(0)
    j = pl.program_id(1)
    n_slots = pl.num_programs(1)
    tm, d = xn_bf.shape
    n_sub = d // LANES
    n_rec = xbuf.shape[1] // tm
    n_used = nused_ref[0]
    n_tok = pos_ref.shape[0]
    first = j == 0

    def row_copy(tok, slot, r):
        return pltpu.make_async_copy(xrows_hbm.at[pl.ds(pl.multiple_of(tok * n_rec, SUBLANES), n_rec), :],
                                     xbuf.at[slot, pl.ds(pl.multiple_of(r * n_rec, SUBLANES), n_rec), :],
                                     sem.at[slot])

    def wait(slot):
        pltpu.make_async_copy(xrows_hbm.at[pl.ds(0, tm * n_rec), :], xbuf.at[slot], sem.at[slot]).wait()

    @pl.when((i == 0) & first)
    def _():
        def clear(r, _):
            src_ref[r] = 0
            return 0
        lax.fori_loop(0, src_ref.shape[0], clear, 0, unroll=DMA_UNROLL)

        def scatter(n, _):
            src_ref[pos_ref[n]] = n
            return 0
        lax.fori_loop(0, n_tok, scatter, 0, unroll=DMA_UNROLL)

        def head(r, _):
            row_copy(src_ref[r], 0, r).start()
            return 0
        lax.fori_loop(0, tm, head, 0, unroll=DMA_UNROLL)

    slot = i % 2

    @pl.when(first & (i <= n_used))
    def _():
        wait(slot)

    @pl.when(i < n_used)
    def _():
        @pl.when(first)
        def _():
            for a in range(n_sub):
                xn_bf[:, a * LANES:(a + 1) * LANES] = xbuf[slot, pl.ds(a, tm, stride=n_rec), :].astype(BF16)
            gate_ref[...] = xbuf[slot, pl.ds(n_sub, tm, stride=n_rec), :]

        share = tm // MOE_SLOTS
        parts = 4
        next_base = (i + 1) * tm + j * share

        def issue(q):
            for r in range(share * q // parts, share * (q + 1) // parts):
                row_copy(src_ref[next_base + r], 1 - slot, j * share + r).start()

        xn = xn_bf[...]
        issue(0)
        hg = jnp.dot(xn, wg_ref[0, 0].astype(BF16), preferred_element_type=F32)
        issue(1)
        hu = jnp.dot(xn, wu_ref[0, 0].astype(BF16), preferred_element_type=F32)
        issue(2)
        gates = gate_ref[...]
        lane = lax.broadcasted_iota(jnp.int32, gates.shape, 1)
        gate = jnp.sum(jnp.where(lane == j, gates, 0.0), axis=1, keepdims=True)
        a = hg * jax.nn.sigmoid(hg) * hu * gate
        issue(3)
        y = jnp.dot(a.astype(BF16), wd_ref[0, 0].astype(BF16), preferred_element_type=F32)

        @pl.when(first)
        def _():
            ys_ref[...] = y

        @pl.when(j > 0)
        def _():
            ys_ref[...] += y

    @pl.when(first & (i >= n_used))
    def _():
        ys_ref[...] = jnp.zeros_like(ys_ref)


def moe_experts(xrows, pos, tile_group, n_used, n_tiles, layer, w_gate, w_up, w_down, *, tm):
    d, f = w_gate.shape[-2:]
    ns = MOE_SLOTS
    wmap = lambda i, j, pos, tg, nu: (layer, tg[i] * ns + j, 0, 0)
    grid_spec = pltpu.PrefetchScalarGridSpec(
        num_scalar_prefetch=3,
        grid=(n_tiles, ns),
        in_specs=[
            pl.BlockSpec(memory_space=pl.ANY),
            pl.BlockSpec((1, 1, d, f), wmap),
            pl.BlockSpec((1, 1, d, f), wmap),
            pl.BlockSpec((1, 1, f, d), wmap),
        ],
        out_specs=pl.BlockSpec((tm, d), lambda i, j, pos, tg, nu: (i, 0)),
        scratch_shapes=[
            pltpu.SMEM((n_tiles * tm,), jnp.int32),
            pltpu.VMEM((2, tm * MOE_REC, LANES), F32),
            pltpu.SemaphoreType.DMA((2,)),
            pltpu.VMEM((tm, d), BF16),
            pltpu.VMEM((tm, LANES), F32),
        ],
    )
    return pl.pallas_call(
        _moe_group_kernel,
        grid_spec=grid_spec,
        out_shape=jax.ShapeDtypeStruct((n_tiles * tm, d), F32),
        compiler_params=pltpu.CompilerParams(dimension_semantics=("arbitrary", "arbitrary"),
                                             vmem_limit_bytes=MOE_VMEM_LIMIT_BYTES),
        name="moe_experts",
    )(pos, tile_group, n_used, xrows, w_gate, w_up, w_down)


def _moe_combine_kernel(pos_ref, x_ref, ys_hbm, o_ref, buf, sem):
    i = pl.program_id(0)
    tc = x_ref.shape[0]

    def row_copy(p, slot, r):
        return pltpu.make_async_copy(ys_hbm.at[pl.ds(p, 1), :], buf.at[slot, pl.ds(r, 1), :], sem.at[slot])

    def issue(tile, slot):
        def body(r, _):
            row_copy(pos_ref[tile * tc + r], slot, r).start()
            return 0
        lax.fori_loop(0, tc, body, 0, unroll=DMA_UNROLL)

    def wait(slot):
        pltpu.make_async_copy(ys_hbm.at[pl.ds(0, tc), :], buf.at[slot], sem.at[slot]).wait()

    @pl.when(i == 0)
    def _():
        issue(0, 0)

    slot = i % 2
    wait(slot)

    @pl.when(i + 1 < pl.num_programs(0))
    def _():
        issue(i + 1, 1 - slot)

    o_ref[...] = x_ref[...] + buf[slot]


def moe_combine(x2d, pos, ys, *, tc):
    t, d = x2d.shape
    grid_spec = pltpu.PrefetchScalarGridSpec(
        num_scalar_prefetch=1,
        grid=(t // tc,),
        in_specs=[
            pl.BlockSpec((tc, d), lambda i, pos: (i, 0)),
            pl.BlockSpec(memory_space=pl.ANY),
        ],
        out_specs=pl.BlockSpec((tc, d), lambda i, pos: (i, 0)),
        scratch_shapes=[pltpu.VMEM((2, tc, d), F32), pltpu.SemaphoreType.DMA((2,))],
    )
    return pl.pallas_call(
        _moe_combine_kernel,
        grid_spec=grid_spec,
        out_shape=jax.ShapeDtypeStruct((t, d), F32),
        compiler_params=_params("arbitrary"),
        name="moe_combine",
    )(pos, x2d, ys)


def moe_ffn(x2d, norm_g, router_w, router_b, layer, w_gate, w_up, w_down):
    grp, xrows = router_topk(x2d, norm_g, router_w, router_b, tm=512)
    pos, tile_group, n_used, n_tiles = moe_plan(grp, MOE_TM)
    ys = moe_experts(xrows, pos, tile_group, n_used, n_tiles, layer, w_gate, w_up, w_down, tm=MOE_TM)
    return moe_combine(x2d, pos, ys, tc=MOE_TC)


def kernel(x, norm_mix, norm_ffn, a_w_in, a_conv_w, a_gate_b, a_out_norm, a_w_out, b_w_in, b_kv_norm, b_w_ukv, b_q_norm, b_k_norm, b_w_out, c_w_in, c_q_norm, c_k_norm, c_w_out, router_w, router_b, moe_w_gate, moe_w_up, moe_w_down):
    bsz, s, d = x.shape
    depth = norm_mix.shape[0]
    x2d = x.reshape(bsz * s, d)
    for i in range(depth):
        kind = i % 3
        j = i // 3
        if kind == 0:
            x2d = mlstm_mixer(x2d, norm_mix[i], a_w_in[j], a_conv_w[j], a_gate_b[j], a_out_norm[j], a_w_out[j],
                              bsz, s)
        elif kind == 1:
            x2d = mla_mixer(x2d, norm_mix[i], b_w_in[j], b_kv_norm[j], b_w_ukv[j], b_q_norm[j], b_k_norm[j],
                            b_w_out[j], bsz, s)
        else:
            x2d = dilated_mixer(x2d, norm_mix[i], c_w_in[j], c_q_norm[j], c_k_norm[j], c_w_out[j], bsz, s)
        x2d = moe_ffn(x2d, norm_ffn[i], router_w, router_b, i, moe_w_gate, moe_w_up, moe_w_down)
    return x2d.reshape(bsz, s, d)
```

```python
import functools
import math

import jax
import jax.numpy as jnp
from jax import lax
from jax.experimental import pallas as pl
from jax.experimental.pallas import tpu as pltpu

F32 = jnp.float32
BF16 = jnp.bfloat16

EPS = 1e-6
LANES = 128
SUBLANES = 8
VMEM_LIMIT_BYTES = 56 * 1024 * 1024
MM_TM = 1024
MM_TN = 1024

MLSTM_HEADS = 8
MLSTM_DQK = 128
MLSTM_DV = 256
MLSTM_CHUNK = 128
MLSTM_HEADS_PER_STEP = 1

MLA_HEADS = 16
MLA_NOPE = 128
MLA_ROPE = 64
MLA_DV = 128
MLA_KV_RANK = 512
ROPE_BASE = 10000.0

DIL_GROUPS = ((128, 1), (512, 4), (2048, 16))
DIL_HEADS = 8
DIL_DH = 128
DIL_SIDE = 64

N_EXPERTS = 16
N_GROUPS = 4
EXPERTS_PER_GROUP = 4
MOE_SLOTS = EXPERTS_PER_GROUP
MOE_TM = 512
MOE_TC = 256
MOE_REC = 24
MOE_VMEM_LIMIT_BYTES = 60 * 1024 * 1024
DMA_UNROLL = 8

NT_DIMS = (((1,), (1,)), ((), ()))
TN_DIMS = (((0,), (0,)), ((), ()))


def _params(*sem):
    return pltpu.CompilerParams(dimension_semantics=sem, vmem_limit_bytes=VMEM_LIMIT_BYTES)


def _rms_scale(x, width):
    return lax.rsqrt(jnp.sum(x * x, axis=-1, keepdims=True) * (1.0 / width) + EPS)


def _norm_matmul_kernel(x_ref, g_ref, w_ref, o_ref, xn_ref):
    @pl.when(pl.program_id(1) == 0)
    def _():
        x = x_ref[...].astype(F32)
        xn_ref[...] = (x * _rms_scale(x, x.shape[-1]) * g_ref[...]).astype(BF16)

    res = jnp.dot(xn_ref[...], w_ref[...], preferred_element_type=F32).astype(o_ref.dtype)
    if len(o_ref.shape) == 2:
        o_ref[...] = res
    else:
        for c in range(o_ref.shape[0]):
            o_ref[c] = res[:, c * LANES:(c + 1) * LANES]


def norm_matmul(x, g, w, *, kdim, tm, tn, out_dtype=F32, column_blocks=False):
    t = x.shape[0]
    n = w.shape[1]
    if column_blocks:
        out_specs = pl.BlockSpec((tn // LANES, tm, LANES), lambda i, j: (j, i, 0))
        out_shape = jax.ShapeDtypeStruct((n // LANES, t, LANES), out_dtype)
    else:
        out_specs = pl.BlockSpec((tm, tn), lambda i, j: (i, j))
        out_shape = jax.ShapeDtypeStruct((t, n), out_dtype)
    return pl.pallas_call(
        _norm_matmul_kernel,
        grid=(t // tm, n // tn),
        in_specs=[
            pl.BlockSpec((tm, kdim), lambda i, j: (i, 0)),
            pl.BlockSpec((1, kdim), lambda i, j: (0, 0)),
            pl.BlockSpec((kdim, tn), lambda i, j: (0, j)),
        ],
        out_specs=out_specs,
        out_shape=out_shape,
        scratch_shapes=[pltpu.VMEM((tm, kdim), BF16)],
        compiler_params=_params("parallel", "arbitrary"),
        name="norm_matmul",
    )(x, g.reshape(1, kdim), w)


def _matmul_res_kernel(a_ref, w_ref, r_ref, o_ref):
    o_ref[...] = r_ref[...] + jnp.dot(a_ref[...].astype(BF16), w_ref[...], preferred_element_type=F32)


def matmul_res(a, w, res, *, tm, tn):
    t, k = a.shape
    n = w.shape[1]
    return pl.pallas_call(
        _matmul_res_kernel,
        grid=(t // tm, n // tn),
        in_specs=[
            pl.BlockSpec((tm, k), lambda i, j: (i, 0)),
            pl.BlockSpec((k, tn), lambda i, j: (0, j)),
            pl.BlockSpec((tm, tn), lambda i, j: (i, j)),
        ],
        out_specs=pl.BlockSpec((tm, tn), lambda i, j: (i, j)),
        out_shape=jax.ShapeDtypeStruct((t, n), F32),
        compiler_params=_params("parallel", "arbitrary"),
        name="matmul_res",
    )(a, w, res)


def _log_sigmoid(x):
    return jnp.minimum(x, 0.0) - jnp.log(1.0 + jnp.exp(-jnp.abs(x)))


def _mlstm_kernel(q_ref, k_ref, v_ref, o_ref, cwq_ref, cwk_ref, gates_ref, onorm_ref, out_ref,
                  qs_ref, ks_ref, hf_ref, hb_ref):
    s = q_ref.shape[2]
    chunk = MLSTM_CHUNK
    n_chunks = s // chunk
    v_blocks = MLSTM_DV // LANES
    row = lax.broadcasted_iota(jnp.int32, (s, 1), 0)

    def conv_silu(p, w):
        prev = jnp.where(row == 0, 0.0, pltpu.roll(p, 1, axis=0))
        nxt = jnp.where(row == s - 1, 0.0, pltpu.roll(p, s - 1, axis=0))
        y = w[0:1, :] * prev + w[1:2, :] * p + w[2:3, :] * nxt
        return y * jax.nn.sigmoid(y)

    for hh in range(MLSTM_HEADS_PER_STEP):
        cols = slice(hh * MLSTM_DQK, (hh + 1) * MLSTM_DQK)
        qs_ref[:, cols] = (conv_silu(q_ref[hh, 0], cwq_ref[:, cols]) * (MLSTM_DQK ** -0.5)).astype(BF16)
        ks_ref[:, cols] = conv_silu(k_ref[hh, 0], cwk_ref[:, cols]).astype(BF16)

    ti = lax.broadcasted_iota(jnp.int32, (chunk, chunk), 0)
    si = lax.broadcasted_iota(jnp.int32, (chunk, chunk), 1)
    eye = ti == si

    def to_col(r):
        return jnp.sum(jnp.where(eye, jnp.broadcast_to(r, (chunk, chunk)), 0.0), axis=1, keepdims=True)

    def to_row(c):
        return jnp.sum(jnp.where(eye, jnp.broadcast_to(c, (chunk, chunk)), 0.0), axis=0, keepdims=True)

    def chunk_step(c, hh, i_idx, f_idx, reverse, carry, hdir_ref):
        mask = (si >= ti) if reverse else (si <= ti)
        c_st, n_st, m_st = carry
        r0 = pl.multiple_of(c * chunk, chunk)
        q = qs_ref[pl.ds(r0, chunk), hh * MLSTM_DQK:(hh + 1) * MLSTM_DQK]
        k = ks_ref[pl.ds(r0, chunk), hh * MLSTM_DQK:(hh + 1) * MLSTM_DQK]
        v = jnp.concatenate([v_ref[hh * v_blocks + c2, 0, pl.ds(r0, chunk), :] for c2 in range(v_blocks)],
                            axis=1).astype(BF16)
        li = gates_ref[0, hh, i_idx:i_idx + 1, pl.ds(r0, chunk)]
        lf = _log_sigmoid(gates_ref[0, hh, f_idx:f_idx + 1, pl.ds(r0, chunk)])
        b_col = jnp.sum(jnp.where(mask, jnp.broadcast_to(lf, (chunk, chunk)), 0.0), axis=1, keepdims=True)
        b_row = to_row(b_col)
        g = jnp.sum(lf, axis=1, keepdims=True)
        dlog = jnp.where(mask, b_col - b_row + li, -jnp.inf)
        m_intra = jnp.max(dlog, axis=1, keepdims=True)
        a_inter = b_col + m_st
        m = jnp.maximum(a_inter, m_intra)
        e_inter = jnp.exp(a_inter - m)
        s_qk = lax.dot_general(q, k, NT_DIMS, preferred_element_type=F32)
        dmat = jnp.exp(dlog - m) * s_qk
        num = (e_inter * jnp.dot(q, c_st.astype(BF16), preferred_element_type=F32)
               + jnp.dot(dmat.astype(BF16), v, preferred_element_type=F32))
        den = (e_inter * jnp.sum(q.astype(F32) * n_st, axis=1, keepdims=True)
               + jnp.sum(dmat, axis=1, keepdims=True))
        hdir_ref[pl.ds(r0, chunk), hh * MLSTM_DV:(hh + 1) * MLSTM_DV] = (
            num / jnp.maximum(jnp.abs(den), jnp.exp(-m)))
        w_row = g - b_row + li
        m_loc = jnp.max(w_row, axis=1, keepdims=True)
        ew_col = to_col(jnp.exp(w_row - m_loc))
        kw = k.astype(F32) * ew_col
        c_loc = lax.dot_general(kw.astype(BF16), v, TN_DIMS, preferred_element_type=F32)
        n_loc = jnp.sum(kw, axis=0, keepdims=True)
        m_new = jnp.maximum(g + m_st, m_loc)
        a = jnp.exp(g + m_st - m_new)
        e = jnp.exp(m_loc - m_new)
        return a * c_st + e * c_loc, a * n_st + e * n_loc, m_new

    def body(ci, carry):
        out = []
        for hh in range(MLSTM_HEADS_PER_STEP):
            out.append(chunk_step(ci, hh, 0, 1, False, carry[2 * hh], hf_ref))
            out.append(chunk_step(n_chunks - 1 - ci, hh, 2, 3, True, carry[2 * hh + 1], hb_ref))
        return tuple(out)

    init = (jnp.zeros((MLSTM_DQK, MLSTM_DV), F32), jnp.zeros((1, MLSTM_DQK), F32), jnp.zeros((1, 1), F32))
    lax.fori_loop(0, n_chunks, body, (init,) * (2 * MLSTM_HEADS_PER_STEP), unroll=2)

    for hh in range(MLSTM_HEADS_PER_STEP):
        cols = slice(hh * MLSTM_DV, (hh + 1) * MLSTM_DV)
        h = hf_ref[:, cols] + hb_ref[:, cols]
        y = h * _rms_scale(h, MLSTM_DV) * onorm_ref[:, cols]
        o_gate = jnp.concatenate([o_ref[hh * v_blocks + c2, 0] for c2 in range(v_blocks)], axis=1)
        out_ref[0, :, cols] = (jax.nn.sigmoid(o_gate) * y).astype(out_ref.dtype)


def mlstm_core(p, conv_w, gates_rows, out_norm):
    _, b, s, _ = p.shape
    nh = MLSTM_HEADS
    hps = MLSTM_HEADS_PER_STEP
    wqk, wv = hps * MLSTM_DQK, hps * MLSTM_DV
    bqk, bv = wqk // LANES, wv // LANES
    k_off = nh * MLSTM_DQK // wqk
    v_off = 2 * nh * MLSTM_DQK // wv
    o_off = v_off + nh // hps
    return pl.pallas_call(
        _mlstm_kernel,
        grid=(b, nh // hps),
        in_specs=[
            pl.BlockSpec((bqk, 1, s, LANES), lambda i, h: (h, i, 0, 0)),
            pl.BlockSpec((bqk, 1, s, LANES), lambda i, h: (k_off + h, i, 0, 0)),
            pl.BlockSpec((bv, 1, s, LANES), lambda i, h: (v_off + h, i, 0, 0)),
            pl.BlockSpec((bv, 1, s, LANES), lambda i, h: (o_off + h, i, 0, 0)),
            pl.BlockSpec((3, wqk), lambda i, h: (0, h)),
            pl.BlockSpec((3, wqk), lambda i, h: (0, k_off + h)),
            pl.BlockSpec((1, hps, 4, s), lambda i, h: (i, h, 0, 0)),
            pl.BlockSpec((1, wv), lambda i, h: (0, h)),
        ],
        out_specs=pl.BlockSpec((1, s, wv), lambda i, h: (i, 0, h)),
        out_shape=jax.ShapeDtypeStruct((b, s, nh * MLSTM_DV), BF16),
        scratch_shapes=[
            pltpu.VMEM((s, wqk), BF16),
            pltpu.VMEM((s, wqk), BF16),
            pltpu.VMEM((s, wv), F32),
            pltpu.VMEM((s, wv), F32),
        ],
        compiler_params=_params("parallel", "parallel"),
        name="mlstm_core",
    )(p, p, p, p, conv_w, conv_w, gates_rows, out_norm.reshape(1, -1))


def mlstm_mixer(x2d, norm_g, w_in, conv_w, gate_b, out_norm, w_out, bsz, s):
    nh = MLSTM_HEADS
    main_cols = 2 * nh * MLSTM_DQK + 2 * nh * MLSTM_DV
    p = norm_matmul(x2d, norm_g, w_in[:, :main_cols].astype(BF16), kdim=x2d.shape[1], tm=MM_TM, tn=MM_TN,
                    column_blocks=True)
    gates = norm_matmul(x2d, norm_g, w_in[:, main_cols:].astype(BF16), kdim=x2d.shape[1], tm=MM_TM, tn=4 * nh)
    gates = gates + gate_b.astype(F32)
    gates_rows = gates.reshape(bsz, s, 4, nh).transpose(0, 3, 2, 1)
    hg = mlstm_core(p.reshape(p.shape[0], bsz, s, LANES), conv_w, gates_rows, out_norm)
    return matmul_res(hg.reshape(bsz * s, nh * MLSTM_DV), w_out.astype(BF16), x2d, tm=MM_TM, tn=MM_TN)


MLA_TQ = 256


def _mla_kernel(qn_ref, qr_ref, kv_ref, kr_ref, cos_ref, sin_ref, gqn_ref, gqr_ref, gkn_ref, gkr_ref,
                out_ref, qs_ref, ks_ref, vs_ref):
    s = qn_ref.shape[2]
    dqk = MLA_NOPE + MLA_ROPE
    half = MLA_ROPE // 2
    cos = cos_ref[...]
    sin = sin_ref[...]
    lane = lax.broadcasted_iota(jnp.int32, (1, LANES), 1)
    first_half = (lane // half) % 2 == 0

    def rope(x):
        rot = jnp.where(first_half, -pltpu.roll(x, LANES - half, axis=1), pltpu.roll(x, half, axis=1))
        return x * cos + rot * sin

    qr_raw = qr_ref[0, 0]
    kr_raw = kr_ref[0]
    q_rope = rope(qr_raw * gqr_ref[...])
    k_rope = rope(kr_raw * gkr_ref[...])
    kr_ss = 0.5 * jnp.sum(kr_raw * kr_raw, axis=-1, keepdims=True)
    for hh in range(2):
        own = (lane // MLA_ROPE) == hh
        qn = qn_ref[hh, 0]
        kn = kv_ref[2 * hh, 0]
        vs_ref[...] = kv_ref[2 * hh + 1, 0].astype(BF16)
        q_ss = (jnp.sum(qn * qn, axis=-1, keepdims=True)
                + jnp.sum(jnp.where(own, qr_raw * qr_raw, 0.0), axis=-1, keepdims=True))
        rq = lax.rsqrt(q_ss * (1.0 / dqk) + EPS) * (dqk ** -0.5)
        rk = lax.rsqrt((jnp.sum(kn * kn, axis=-1, keepdims=True) + kr_ss) * (1.0 / dqk) + EPS)
        qs_ref[:, 0:LANES] = (qn * rq * gqn_ref[...]).astype(BF16)
        qs_ref[:, LANES:2 * LANES] = (jnp.where(own, q_rope, 0.0) * rq).astype(BF16)
        ks_ref[:, 0:LANES] = (kn * rk * gkn_ref[...]).astype(BF16)
        ks_ref[:, LANES:2 * LANES] = (k_rope * rk).astype(BF16)

        def qblock(i, _):
            r0 = pl.multiple_of(i * MLA_TQ, MLA_TQ)
            sc = lax.dot_general(qs_ref[pl.ds(r0, MLA_TQ), :], ks_ref[...], NT_DIMS, preferred_element_type=F32)
            mx = jnp.max(sc, axis=-1, keepdims=True)
            e = jnp.exp(sc - mx)
            den = jnp.sum(e, axis=-1, keepdims=True)
            o = jnp.dot(e.astype(BF16), vs_ref[...], preferred_element_type=F32) / den
            out_ref[0, pl.ds(r0, MLA_TQ), hh * MLA_DV:(hh + 1) * MLA_DV] = o.astype(out_ref.dtype)
            return 0

        lax.fori_loop(0, s // MLA_TQ, qblock, 0, unroll=2)


def mla_attention(qall, kv, kr, cos, sin, q_norm, k_norm):
    _, b, s, _ = qall.shape
    nh = MLA_HEADS
    qr_off = nh * MLA_NOPE // LANES
    row = lambda t: t.reshape(1, -1).astype(F32)
    twice = lambda t: jnp.concatenate([t, t], axis=-1)
    vec = pl.BlockSpec((1, LANES), lambda i, h: (0, 0))
    cos = jnp.tile(cos, (1, LANES // cos.shape[1]))
    sin = jnp.tile(sin, (1, LANES // sin.shape[1]))
    return pl.pallas_call(
        _mla_kernel,
        grid=(b, nh // 2),
        in_specs=[
            pl.BlockSpec((2, 1, s, LANES), lambda i, h: (h, i, 0, 0)),
            pl.BlockSpec((1, 1, s, LANES), lambda i, h: (qr_off + h, i, 0, 0)),
            pl.BlockSpec((4, 1, s, LANES), lambda i, h: (h, i, 0, 0)),
            pl.BlockSpec((1, s, LANES), lambda i, h: (i, 0, 0)),
            pl.BlockSpec((s, LANES), lambda i, h: (0, 0)),
            pl.BlockSpec((s, LANES), lambda i, h: (0, 0)),
            vec, vec, vec, vec,
        ],
        out_specs=pl.BlockSpec((1, s, 2 * MLA_DV), lambda i, h: (i, 0, h)),
        out_shape=jax.ShapeDtypeStruct((b, s, nh * MLA_DV), BF16),
        scratch_shapes=[
            pltpu.VMEM((s, 2 * LANES), BF16),
            pltpu.VMEM((s, 2 * LANES), BF16),
            pltpu.VMEM((s, MLA_DV), BF16),
        ],
        compiler_params=_params("parallel", "parallel"),
        name="mla_attention",
    )(qall, qall, kv, twice(kr), cos, sin,
      row(q_norm[:MLA_NOPE]), twice(row(q_norm[MLA_NOPE:])), row(k_norm[:MLA_NOPE]), twice(row(k_norm[MLA_NOPE:])))


def mla_mixer(x2d, norm_g, w_in, kv_norm, w_ukv, q_norm, k_norm, w_out, bsz, s):
    nh = MLA_HEADS
    dqk = MLA_NOPE + MLA_ROPE
    d = x2d.shape[1]
    w_q = w_in[:, :nh * dqk].reshape(d, nh, dqk)
    w_qcat = jnp.concatenate([w_q[:, :, :MLA_NOPE].reshape(d, nh * MLA_NOPE),
                              w_q[:, :, MLA_NOPE:].reshape(d, nh * MLA_ROPE)], axis=1).astype(BF16)
    qall = norm_matmul(x2d, norm_g, w_qcat, kdim=d, tm=MM_TM, tn=MM_TN, column_blocks=True)
    lat = norm_matmul(x2d, norm_g, w_in[:, nh * dqk:].astype(BF16), kdim=d, tm=MM_TM, tn=MLA_KV_RANK + MLA_ROPE)
    kv = norm_matmul(lat, kv_norm, w_ukv.astype(BF16), kdim=MLA_KV_RANK, tm=MM_TM, tn=MM_TN, column_blocks=True)
    kr = lat[:, MLA_KV_RANK:]
    half = MLA_ROPE // 2
    inv = ROPE_BASE ** (-jnp.arange(half, dtype=F32) / half)
    ang = jnp.arange(s).astype(F32)[:, None] * inv[None, :]
    o = mla_attention(qall.reshape(-1, bsz, s, LANES), kv.reshape(-1, bsz, s, LANES), kr.reshape(bsz, s, MLA_ROPE),
                      jnp.cos(ang), jnp.sin(ang), q_norm, k_norm)
    return matmul_res(o.reshape(bsz * s, nh * MLA_DV), w_out.astype(BF16), x2d, tm=MM_TM, tn=MM_TN)


DIL_QB = 128
DIL_KW = 256
DIL_UNROLL = 8


def _dilated_kernel(*refs):
    n_g = len(DIL_GROUPS)
    qkv_refs = refs[:3 * n_g]
    gq_ref, gk_ref = refs[3 * n_g], refs[3 * n_g + 1]
    out_ref = refs[3 * n_g + 2]
    qs_ref, ks_ref, vs_ref = refs[3 * n_g + 3:3 * n_g + 6]
    acc_refs = refs[3 * n_g + 6:3 * n_g + 6 + n_g]
    m_refs = refs[3 * n_g + 6 + n_g:3 * n_g + 6 + 2 * n_g]
    l_refs = refs[3 * n_g + 6 + 2 * n_g:3 * n_g + 6 + 3 * n_g]
    s = out_ref.shape[1]
    head = pl.program_id(1)
    scale = DIL_DH ** -0.5

    for gi, (_, dil) in enumerate(DIL_GROUPS):
        q_ref, k_ref, v_ref = qkv_refs[3 * gi:3 * gi + 3]
        n = s // dil
        kw = min(DIL_KW, n)
        slope_arg = jnp.full((1, 1), -8.0 / (n_g * DIL_HEADS), F32) * (gi * DIL_HEADS + head + 1).astype(F32)
        bias_step = jnp.exp2(slope_arg) * float(dil)
        gq = gq_ref[gi:gi + 1, :]
        gk = gk_ref[gi:gi + 1, :]

        def residue(r, _, q_ref=q_ref, k_ref=k_ref, v_ref=v_ref, n=n, kw=kw, dil=dil, gi=gi,
                    bias_step=bias_step, gq=gq, gk=gk):
            def sub(ref):
                if dil == 1:
                    return ref[0, 0]
                return ref[0, 0, pl.ds(r, n, stride=dil), :]

            q = sub(q_ref)
            k = sub(k_ref)
            base = 0 if dil == 1 else pl.multiple_of(r * n, n)
            qs_ref[pl.ds(base, n), :] = (q * _rms_scale(q, DIL_DH) * gq * scale).astype(BF16)
            ks_ref[pl.ds(base, n), :] = (k * _rms_scale(k, DIL_DH) * gk).astype(BF16)
            vs_ref[pl.ds(base, n), :] = sub(v_ref).astype(BF16)
            for qb in range(n // DIL_QB):
                q0 = qb * DIL_QB
                k0 = min(max(q0 - DIL_SIDE, 0), n - kw)
                sc = lax.dot_general(qs_ref[pl.ds(base + q0, DIL_QB), :], ks_ref[pl.ds(base + k0, kw), :],
                                     NT_DIMS, preferred_element_type=F32)
                dist = jnp.abs((k0 + lax.broadcasted_iota(jnp.int32, (DIL_QB, kw), 1))
                               - (q0 + lax.broadcasted_iota(jnp.int32, (DIL_QB, kw), 0)))
                sc = sc - bias_step * dist.astype(F32)
                sc = jnp.where(dist <= DIL_SIDE, sc, -jnp.inf)
                mx = jnp.max(sc, axis=-1, keepdims=True)
                e = jnp.exp(sc - mx)
                den = jnp.sum(e, axis=-1, keepdims=True)
                acc = jnp.dot(e.astype(BF16), vs_ref[pl.ds(base + k0, kw), :], preferred_element_type=F32)
                if dil == 1:
                    rows = pl.ds(q0, DIL_QB)
                else:
                    rows = pl.ds(r + dil * q0, DIL_QB, stride=dil)
                acc_refs[gi][rows, :] = acc
                m_refs[gi][rows, :] = jnp.broadcast_to(mx, (DIL_QB, DIL_DH))
                l_refs[gi][rows, :] = jnp.broadcast_to(den, (DIL_QB, DIL_DH))
            return 0

        if dil == 1:
            residue(0, 0)
        else:
            lax.fori_loop(0, dil, residue, 0, unroll=min(dil, DIL_UNROLL))

    m_all = m_refs[0][...]
    for gi in range(1, n_g):
        m_all = jnp.maximum(m_all, m_refs[gi][...])
    num = jnp.zeros((s, DIL_DH), F32)
    den = jnp.zeros((s, DIL_DH), F32)
    for gi in range(n_g):
        w = jnp.exp(m_refs[gi][...] - m_all)
        num = num + acc_refs[gi][...] * w
        den = den + l_refs[gi][...] * w
    out_ref[0] = (num / den).astype(out_ref.dtype)


def dilated_attention(p, q_norm, k_norm):
    _, b, s, _ = p.shape
    n_g = len(DIL_GROUPS)
    nh = DIL_HEADS

    def spec(col0):
        return pl.BlockSpec((1, 1, s, DIL_DH), lambda i, h: (col0 + h, i, 0, 0))

    in_specs = [spec((gi * 3 + which) * nh) for gi in range(n_g) for which in range(3)]
    in_specs += [pl.BlockSpec((n_g, DIL_DH), lambda i, h: (0, 0))] * 2
    scratch = [pltpu.VMEM((s, DIL_DH), BF16)] * 3 + [pltpu.VMEM((s, DIL_DH), F32)] * (3 * n_g)
    return pl.pallas_call(
        _dilated_kernel,
        grid=(b, nh),
        in_specs=in_specs,
        out_specs=pl.BlockSpec((1, s, DIL_DH), lambda i, h: (i, 0, h)),
        out_shape=jax.ShapeDtypeStruct((b, s, nh * DIL_DH), BF16),
        scratch_shapes=scratch,
        compiler_params=_params("parallel", "parallel"),
        name="dilated_attention",
    )(*([p] * (3 * n_g)), q_norm.astype(F32), k_norm.astype(F32))


def dilated_mixer(x2d, norm_g, w_in, q_norm, k_norm, w_out, bsz, s):
    p = norm_matmul(x2d, norm_g, w_in.astype(BF16), kdim=x2d.shape[1], tm=MM_TM, tn=MM_TN, column_blocks=True)
    o = dilated_attention(p.reshape(p.shape[0], bsz, s, DIL_DH), q_norm, k_norm)
    return matmul_res(o.reshape(bsz * s, -1), w_out.astype(BF16), x2d, tm=MM_TM, tn=MM_TN)


def _router_kernel(x_ref, g_ref, rw_ref, rb_ref, grp_ref, xrows_ref):
    x = x_ref[...]
    tm, d = x.shape
    xn = x * _rms_scale(x, d) * g_ref[...]
    n_sub = d // LANES
    n_rec = xrows_ref.shape[0] // tm
    for a in range(n_sub):
        xrows_ref[pl.ds(a, tm, stride=n_rec), :] = xn[:, a * LANES:(a + 1) * LANES]
    for a in range(n_sub + 1, n_rec):
        xrows_ref[pl.ds(a, tm, stride=n_rec), :] = jnp.zeros((tm, LANES), F32)
    def split(v):
        hi = v.astype(BF16)
        return hi, (v - hi.astype(F32)).astype(BF16)

    def nt(a, b):
        return lax.dot_general(a, b, NT_DIMS, preferred_element_type=F32)

    x_hi, x_lo = split(xn)
    w_hi, w_lo = split(rw_ref[...])
    logits = nt(w_hi, x_hi) + (nt(w_hi, x_lo) + nt(w_lo, x_hi))
    scores = jax.nn.sigmoid(logits)
    sel = scores + rb_ref[...]
    ng = N_GROUPS
    slots = range(EXPERTS_PER_GROUP)
    v = [sel[j * ng:(j + 1) * ng, :] for j in slots]
    sc = [scores[j * ng:(j + 1) * ng, :] for j in slots]

    def first_argmax(vals):
        best = vals[0]
        for t in vals[1:]:
            best = jnp.maximum(best, t)
        idx = jnp.full(best.shape, len(vals) - 1, jnp.int32)
        for j in reversed(range(len(vals) - 1)):
            idx = jnp.where(vals[j] == best, j, idx)
        return best, idx

    m1, i1 = first_argmax(v)
    m2, i2 = first_argmax([jnp.where(i1 == j, -jnp.inf, v[j]) for j in slots])
    grp_score = m1 + m2
    grp_iota = lax.broadcasted_iota(jnp.int32, grp_score.shape, 0)
    best = jnp.max(grp_score, axis=0, keepdims=True)
    grp = jnp.min(jnp.where(grp_score == best, grp_iota, ng), axis=0, keepdims=True)
    w1 = sum(jnp.where(i1 == j, sc[j], 0.0) for j in slots)
    w2 = sum(jnp.where(i2 == j, sc[j], 0.0) for j in slots)
    tot = w1 + w2
    chosen = grp_iota == grp

    def pick(v):
        return jnp.sum(jnp.where(chosen, v, jnp.zeros_like(v)), axis=0, keepdims=True)

    grp_ref[...] = grp
    gates = [pick(jnp.where(i1 == j, w1 / tot, 0.0) + jnp.where(i2 == j, w2 / tot, 0.0)) for j in slots]
    gates_t = jnp.concatenate(gates + [jnp.zeros((SUBLANES - len(gates), tm), F32)], axis=0)
    sel_row = lax.broadcasted_iota(jnp.int32, (SUBLANES, LANES), 0)
    sel_col = lax.broadcasted_iota(jnp.int32, (SUBLANES, LANES), 1)
    select = (sel_row == sel_col).astype(F32)
    xrows_ref[pl.ds(n_sub, tm, stride=n_rec), :] = lax.dot_general(
        gates_t, select, TN_DIMS, precision=lax.Precision.HIGHEST, preferred_element_type=F32)


def router_topk(x2d, norm_g, router_w, router_b, *, tm):
    t, d = x2d.shape
    ne = N_EXPERTS
    perm = jnp.arange(ne).reshape(N_GROUPS, EXPERTS_PER_GROUP).T.reshape(-1)
    rw_t = router_w.astype(F32).T[perm]
    rb = router_b.astype(F32)[perm].reshape(ne, 1)
    return pl.pallas_call(
        _router_kernel,
        grid=(t // tm,),
        in_specs=[
            pl.BlockSpec((tm, d), lambda i: (i, 0)),
            pl.BlockSpec((1, d), lambda i: (0, 0)),
            pl.BlockSpec((ne, d), lambda i: (0, 0)),
            pl.BlockSpec((ne, 1), lambda i: (0, 0)),
        ],
        out_specs=[pl.BlockSpec((1, tm), lambda i: (0, i)),
                   pl.BlockSpec((tm * MOE_REC, LANES), lambda i: (i, 0))],
        out_shape=[jax.ShapeDtypeStruct((1, t), jnp.int32),
                   jax.ShapeDtypeStruct((t * MOE_REC, LANES), F32)],
        compiler_params=_params("parallel"),
        name="router",
    )(x2d, norm_g.reshape(1, d), rw_t, rb)


def moe_plan(grp, tm):
    ne = N_GROUPS
    e_flat = grp.reshape(-1)
    n_tiles = -(-e_flat.shape[0] // tm) + ne + 1
    onehot = (e_flat[:, None] == jnp.arange(ne, dtype=jnp.int32)[None, :]).astype(jnp.int32)
    csum = jnp.cumsum(onehot, axis=0)
    counts = csum[-1]
    padded = (counts + tm - 1) // tm * tm
    ends = jnp.cumsum(padded)
    starts = ends - padded
    pos = jnp.sum(onehot * (csum - 1 + starts[None, :]), axis=1)
    tile_row0 = jnp.arange(n_tiles, dtype=jnp.int32) * tm
    tile_expert = jnp.minimum(jnp.sum((tile_row0[:, None] >= ends[None, :]).astype(jnp.int32), axis=1), ne - 1)
    n_used = (ends[-1] // tm).reshape(1)
    return pos.astype(jnp.int32), tile_expert.astype(jnp.int32), n_used.astype(jnp.int32), n_tiles


def _moe_group_kernel(pos_ref, tg_ref, nused_ref, xrows_hbm, wg_ref, wu_ref, wd_ref, ys_ref,
                      src_ref, xbuf, sem, xn_bf, gate_ref):
    i = pl.program_id(0)
    j = pl.program_id(1)
    n_slots = pl.num_programs(1)
    tm, d = xn_bf.shape
    n_sub = d // LANES
    n_rec = xbuf.shape[1] // tm
    n_used = nused_ref[0]
    n_tok = pos_ref.shape[0]
    first = j == 0

    def row_copy(tok, slot, r):
        return pltpu.make_async_copy(xrows_hbm.at[pl.ds(pl.multiple_of(tok * n_rec, SUBLANES), n_rec), :],
                                     xbuf.at[slot, pl.ds(pl.multiple_of(r * n_rec, SUBLANES), n_rec), :],
                                     sem.at[slot])

    def wait(slot):
        pltpu.make_async_copy(xrows_hbm.at[pl.ds(0, tm * n_rec), :], xbuf.at[slot], sem.at[slot]).wait()

    @pl.when((i == 0) & first)
    def _():
        def clear(r, _):
            src_ref[r] = 0
            return 0
        lax.fori_loop(0, src_ref.shape[0], clear, 0, unroll=DMA_UNROLL)

        def scatter(n, _):
            src_ref[pos_ref[n]] = n
            return 0
        lax.fori_loop(0, n_tok, scatter, 0, unroll=DMA_UNROLL)

        def head(r, _):
            row_copy(src_ref[r], 0, r).start()
            return 0
        lax.fori_loop(0, tm, head, 0, unroll=DMA_UNROLL)

    slot = i % 2

    @pl.when(first & (i <= n_used))
    def _():
        wait(slot)

    @pl.when(i < n_used)
    def _():
        @pl.when(first)
        def _():
            for a in range(n_sub):
                xn_bf[:, a * LANES:(a + 1) * LANES] = xbuf[slot, pl.ds(a, tm, stride=n_rec), :].astype(BF16)
            gate_ref[...] = xbuf[slot, pl.ds(n_sub, tm, stride=n_rec), :]

        share = tm // MOE_SLOTS
        parts = 4
        next_base = (i + 1) * tm + j * share

        def issue(q):
            for r in range(share * q // parts, share * (q + 1) // parts):
                row_copy(src_ref[next_base + r], 1 - slot, j * share + r).start()

        xn = xn_bf[...]
        issue(0)
        hg = jnp.dot(xn, wg_ref[0, 0].astype(BF16), preferred_element_type=F32)
        issue(1)
        hu = jnp.dot(xn, wu_ref[0, 0].astype(BF16), preferred_element_type=F32)
        issue(2)
        gates = gate_ref[...]
        lane = lax.broadcasted_iota(jnp.int32, gates.shape, 1)
        gate = jnp.sum(jnp.where(lane == j, gates, 0.0), axis=1, keepdims=True)
        a = hg * jax.nn.sigmoid(hg) * hu * gate
        issue(3)
        y = jnp.dot(a.astype(BF16), wd_ref[0, 0].astype(BF16), preferred_element_type=F32)

        @pl.when(first)
        def _():
            ys_ref[...] = y

        @pl.when(j > 0)
        def _():
            ys_ref[...] += y

    @pl.when(first & (i >= n_used))
    def _():
        ys_ref[...] = jnp.zeros_like(ys_ref)


def moe_experts(xrows, pos, tile_group, n_used, n_tiles, layer, w_gate, w_up, w_down, *, tm):
    d, f = w_gate.shape[-2:]
    ns = MOE_SLOTS
    wmap = lambda i, j, pos, tg, nu: (layer, tg[i] * ns + j, 0, 0)
    grid_spec = pltpu.PrefetchScalarGridSpec(
        num_scalar_prefetch=3,
        grid=(n_tiles, ns),
        in_specs=[
            pl.BlockSpec(memory_space=pl.ANY),
            pl.BlockSpec((1, 1, d, f), wmap),
            pl.BlockSpec((1, 1, d, f), wmap),
            pl.BlockSpec((1, 1, f, d), wmap),
        ],
        out_specs=pl.BlockSpec((tm, d), lambda i, j, pos, tg, nu: (i, 0)),
        scratch_shapes=[
            pltpu.SMEM((n_tiles * tm,), jnp.int32),
            pltpu.VMEM((2, tm * MOE_REC, LANES), F32),
            pltpu.SemaphoreType.DMA((2,)),
            pltpu.VMEM((tm, d), BF16),
            pltpu.VMEM((tm, LANES), F32),
        ],
    )
    return pl.pallas_call(
        _moe_group_kernel,
        grid_spec=grid_spec,
        out_shape=jax.ShapeDtypeStruct((n_tiles * tm, d), F32),
        compiler_params=pltpu.CompilerParams(dimension_semantics=("arbitrary", "arbitrary"),
                                             vmem_limit_bytes=MOE_VMEM_LIMIT_BYTES),
        name="moe_experts",
    )(pos, tile_group, n_used, xrows, w_gate, w_up, w_down)


def _moe_combine_kernel(pos_ref, x_ref, ys_hbm, o_ref, buf, sem):
    i = pl.program_id(0)
    tc = x_ref.shape[0]

    def row_copy(p, slot, r):
        return pltpu.make_async_copy(ys_hbm.at[pl.ds(p, 1), :], buf.at[slot, pl.ds(r, 1), :], sem.at[slot])

    def issue(tile, slot):
        def body(r, _):
            row_copy(pos_ref[tile * tc + r], slot, r).start()
            return 0
        lax.fori_loop(0, tc, body, 0, unroll=DMA_UNROLL)

    def wait(slot):
        pltpu.make_async_copy(ys_hbm.at[pl.ds(0, tc), :], buf.at[slot], sem.at[slot]).wait()

    @pl.when(i == 0)
    def _():
        issue(0, 0)

    slot = i % 2
    wait(slot)

    @pl.when(i + 1 < pl.num_programs(0))
    def _():
        issue(i + 1, 1 - slot)

    o_ref[...] = x_ref[...] + buf[slot]


def moe_combine(x2d, pos, ys, *, tc):
    t, d = x2d.shape
    grid_spec = pltpu.PrefetchScalarGridSpec(
        num_scalar_prefetch=1,
        grid=(t // tc,),
        in_specs=[
            pl.BlockSpec((tc, d), lambda i, pos: (i, 0)),
            pl.BlockSpec(memory_space=pl.ANY),
        ],
        out_specs=pl.BlockSpec((tc, d), lambda i, pos: (i, 0)),
        scratch_shapes=[pltpu.VMEM((2, tc, d), F32), pltpu.SemaphoreType.DMA((2,))],
    )
    return pl.pallas_call(
        _moe_combine_kernel,
        grid_spec=grid_spec,
        out_shape=jax.ShapeDtypeStruct((t, d), F32),
        compiler_params=_params("arbitrary"),
        name="moe_combine",
    )(pos, x2d, ys)


def moe_ffn(x2d, norm_g, router_w, router_b, layer, w_gate, w_up, w_down):
    grp, xrows = router_topk(x2d, norm_g, router_w, router_b, tm=512)
    pos, tile_group, n_used, n_tiles = moe_plan(grp, MOE_TM)
    ys = moe_experts(xrows, pos, tile_group, n_used, n_tiles, layer, w_gate, w_up, w_down, tm=MOE_TM)
    return moe_combine(x2d, pos, ys, tc=MOE_TC)


def kernel(x, norm_mix, norm_ffn, a_w_in, a_conv_w, a_gate_b, a_out_norm, a_w_out, b_w_in, b_kv_norm, b_w_ukv, b_q_norm, b_k_norm, b_w_out, c_w_in, c_q_norm, c_k_norm, c_w_out, router_w, router_b, moe_w_gate, moe_w_up, moe_w_down):
    bsz, s, d = x.shape
    depth = norm_mix.shape[0]
    x2d = x.reshape(bsz * s, d)
    for i in range(depth):
        kind = i % 3
        j = i // 3
        if kind == 0:
            x2d = mlstm_mixer(x2d, norm_mix[i], a_w_in[j], a_conv_w[j], a_gate_b[j], a_out_norm[j], a_w_out[j],
                              bsz, s)
        elif kind == 1:
            x2d = mla_mixer(x2d, norm_mix[i], b_w_in[j], b_kv_norm[j], b_w_ukv[j], b_q_norm[j], b_k_norm[j],
                            b_w_out[j], bsz, s)
        else:
            x2d = dilated_mixer(x2d, norm_mix[i], c_w_in[j], c_q_norm[j], c_k_norm[j], c_w_out[j], bsz, s)
        x2d = moe_ffn(x2d, norm_ffn[i], router_w, router_b, i, moe_w_gate, moe_w_up, moe_w_down)
    return x2d.reshape(bsz, s, d)
```

```python
import functools
import math

import jax
import jax.numpy as jnp
from jax import lax
from jax.experimental import pallas as pl
from jax.experimental.pallas import tpu as pltpu

F32 = jnp.float32
BF16 = jnp.bfloat16

EPS = 1e-6
LANES = 128
SUBLANES = 8
VMEM_LIMIT_BYTES = 56 * 1024 * 1024
MM_TM = 1024
MM_TN = 1024

MLSTM_HEADS = 8
MLSTM_DQK = 128
MLSTM_DV = 256
MLSTM_CHUNK = 128
MLSTM_HEADS_PER_STEP = 1
MLSTM_PROJ_SPLIT = 7

MLA_HEADS = 16
MLA_NOPE = 128
MLA_ROPE = 64
MLA_DV = 128
MLA_KV_RANK = 512
ROPE_BASE = 10000.0

DIL_GROUPS = ((128, 1), (512, 4), (2048, 16))
DIL_HEADS = 8
DIL_DH = 128
DIL_SIDE = 64

N_EXPERTS = 16
N_GROUPS = 4
EXPERTS_PER_GROUP = 4
MOE_SLOTS = EXPERTS_PER_GROUP
MOE_TM = 512
MOE_TC = 256
MOE_REC = 24
MOE_VMEM_LIMIT_BYTES = 60 * 1024 * 1024
DMA_UNROLL = 8

NT_DIMS = (((1,), (1,)), ((), ()))
TN_DIMS = (((0,), (0,)), ((), ()))


def _params(*sem):
    return pltpu.CompilerParams(dimension_semantics=sem, vmem_limit_bytes=VMEM_LIMIT_BYTES)


def _rms_scale(x, width):
    return lax.rsqrt(jnp.sum(x * x, axis=-1, keepdims=True) * (1.0 / width) + EPS)


def _norm_matmul_kernel(x_ref, g_ref, w_ref, o_ref, xn_ref):
    @pl.when(pl.program_id(1) == 0)
    def _():
        x = x_ref[...].astype(F32)
        xn_ref[...] = (x * _rms_scale(x, x.shape[-1]) * g_ref[...]).astype(BF16)

    o_ref[...] = jnp.dot(xn_ref[...], w_ref[...], preferred_element_type=F32).astype(o_ref.dtype)


def norm_matmul(x, g, w, *, kdim, tm, tn, out_dtype=F32):
    t = x.shape[0]
    n = w.shape[1]
    return pl.pallas_call(
        _norm_matmul_kernel,
        grid=(t // tm, n // tn),
        in_specs=[
            pl.BlockSpec((tm, kdim), lambda i, j: (i, 0)),
            pl.BlockSpec((1, kdim), lambda i, j: (0, 0)),
            pl.BlockSpec((kdim, tn), lambda i, j: (0, j)),
        ],
        out_specs=pl.BlockSpec((tm, tn), lambda i, j: (i, j)),
        out_shape=jax.ShapeDtypeStruct((t, n), out_dtype),
        scratch_shapes=[pltpu.VMEM((tm, kdim), BF16)],
        compiler_params=_params("parallel", "arbitrary"),
        name="norm_matmul",
    )(x, g.reshape(1, kdim), w)


def _matmul_res_kernel(a_ref, w_ref, r_ref, o_ref):
    o_ref[...] = r_ref[...] + jnp.dot(a_ref[...].astype(BF16), w_ref[...], preferred_element_type=F32)


def matmul_res(a, w, res, *, tm, tn):
    t, k = a.shape
    n = w.shape[1]
    return pl.pallas_call(
        _matmul_res_kernel,
        grid=(t // tm, n // tn),
        in_specs=[
            pl.BlockSpec((tm, k), lambda i, j: (i, 0)),
            pl.BlockSpec((k, tn), lambda i, j: (0, j)),
            pl.BlockSpec((tm, tn), lambda i, j: (i, j)),
        ],
        out_specs=pl.BlockSpec((tm, tn), lambda i, j: (i, j)),
        out_shape=jax.ShapeDtypeStruct((t, n), F32),
        compiler_params=_params("parallel", "arbitrary"),
        name="matmul_res",
    )(a, w, res)


def _log_sigmoid(x):
    return jnp.minimum(x, 0.0) - jnp.log(1.0 + jnp.exp(-jnp.abs(x)))


def _mlstm_kernel(q_ref, k_ref, v_ref, o_ref, cwq_ref, cwk_ref, gates_ref, onorm_ref, out_ref,
                  qs_ref, ks_ref, hf_ref, hb_ref):
    s = q_ref.shape[1]
    chunk = MLSTM_CHUNK
    n_chunks = s // chunk
    row = lax.broadcasted_iota(jnp.int32, (s, 1), 0)

    def conv_silu(p, w):
        prev = jnp.where(row == 0, 0.0, pltpu.roll(p, 1, axis=0))
        nxt = jnp.where(row == s - 1, 0.0, pltpu.roll(p, s - 1, axis=0))
        y = w[0:1, :] * prev + w[1:2, :] * p + w[2:3, :] * nxt
        return y * jax.nn.sigmoid(y)

    qs_ref[...] = (conv_silu(q_ref[0], cwq_ref[...]) * (MLSTM_DQK ** -0.5)).astype(BF16)
    ks_ref[...] = conv_silu(k_ref[0], cwk_ref[...]).astype(BF16)

    ti = lax.broadcasted_iota(jnp.int32, (chunk, chunk), 0)
    si = lax.broadcasted_iota(jnp.int32, (chunk, chunk), 1)
    eye = ti == si

    def to_col(r):
        return jnp.sum(jnp.where(eye, jnp.broadcast_to(r, (chunk, chunk)), 0.0), axis=1, keepdims=True)

    def to_row(c):
        return jnp.sum(jnp.where(eye, jnp.broadcast_to(c, (chunk, chunk)), 0.0), axis=0, keepdims=True)

    def chunk_step(c, hh, i_idx, f_idx, reverse, carry, hdir_ref):
        mask = (si >= ti) if reverse else (si <= ti)
        c_st, n_st, m_st = carry
        r0 = pl.multiple_of(c * chunk, chunk)
        q = qs_ref[pl.ds(r0, chunk), hh * MLSTM_DQK:(hh + 1) * MLSTM_DQK]
        k = ks_ref[pl.ds(r0, chunk), hh * MLSTM_DQK:(hh + 1) * MLSTM_DQK]
        v = v_ref[0, pl.ds(r0, chunk), hh * MLSTM_DV:(hh + 1) * MLSTM_DV].astype(BF16)
        li = gates_ref[0, hh, i_idx:i_idx + 1, pl.ds(r0, chunk)]
        lf = _log_sigmoid(gates_ref[0, hh, f_idx:f_idx + 1, pl.ds(r0, chunk)])
        b_col = jnp.sum(jnp.where(mask, jnp.broadcast_to(lf, (chunk, chunk)), 0.0), axis=1, keepdims=True)
        b_row = to_row(b_col)
        g = jnp.sum(lf, axis=1, keepdims=True)
        dlog = jnp.where(mask, b_col - b_row + li, -jnp.inf)
        m_intra = jnp.max(dlog, axis=1, keepdims=True)
        a_inter = b_col + m_st
        m = jnp.maximum(a_inter, m_intra)
        e_inter = jnp.exp(a_inter - m)
        s_qk = lax.dot_general(q, k, NT_DIMS, preferred_element_type=F32)
        dmat = jnp.exp(dlog - m) * s_qk
        num = (e_inter * jnp.dot(q, c_st.astype(BF16), preferred_element_type=F32)
               + jnp.dot(dmat.astype(BF16), v, preferred_element_type=F32))
        den = (e_inter * jnp.sum(q.astype(F32) * n_st, axis=1, keepdims=True)
               + jnp.sum(dmat, axis=1, keepdims=True))
        hdir_ref[pl.ds(r0, chunk), hh * MLSTM_DV:(hh + 1) * MLSTM_DV] = (
            num / jnp.maximum(jnp.abs(den), jnp.exp(-m)))
        w_row = g - b_row + li
        m_loc = jnp.max(w_row, axis=1, keepdims=True)
        ew_col = to_col(jnp.exp(w_row - m_loc))
        kw = k.astype(F32) * ew_col
        c_loc = lax.dot_general(kw.astype(BF16), v, TN_DIMS, preferred_element_type=F32)
        n_loc = jnp.sum(kw, axis=0, keepdims=True)
        m_new = jnp.maximum(g + m_st, m_loc)
        a = jnp.exp(g + m_st - m_new)
        e = jnp.exp(m_loc - m_new)
        return a * c_st + e * c_loc, a * n_st + e * n_loc, m_new

    def body(ci, carry):
        out = []
        for hh in range(MLSTM_HEADS_PER_STEP):
            out.append(chunk_step(ci, hh, 0, 1, False, carry[2 * hh], hf_ref))
            out.append(chunk_step(n_chunks - 1 - ci, hh, 2, 3, True, carry[2 * hh + 1], hb_ref))
        return tuple(out)

    init = (jnp.zeros((MLSTM_DQK, MLSTM_DV), F32), jnp.zeros((1, MLSTM_DQK), F32), jnp.zeros((1, 1), F32))
    lax.fori_loop(0, n_chunks, body, (init,) * (2 * MLSTM_HEADS_PER_STEP), unroll=2)

    for hh in range(MLSTM_HEADS_PER_STEP):
        cols = slice(hh * MLSTM_DV, (hh + 1) * MLSTM_DV)
        h = hf_ref[:, cols] + hb_ref[:, cols]
        y = h * _rms_scale(h, MLSTM_DV) * onorm_ref[:, cols]
        out_ref[0, :, cols] = (jax.nn.sigmoid(o_ref[0, :, cols]) * y).astype(out_ref.dtype)


def mlstm_core(p, conv_w, gates_rows, out_norm):
    b, s, _ = p.shape
    nh = MLSTM_HEADS
    hps = MLSTM_HEADS_PER_STEP
    wqk, wv = hps * MLSTM_DQK, hps * MLSTM_DV
    k_off = nh * MLSTM_DQK // wqk
    v_off = 2 * nh * MLSTM_DQK // wv
    o_off = v_off + nh // hps
    return pl.pallas_call(
        _mlstm_kernel,
        grid=(b, nh // hps),
        in_specs=[
            pl.BlockSpec((1, s, wqk), lambda i, h: (i, 0, h)),
            pl.BlockSpec((1, s, wqk), lambda i, h: (i, 0, k_off + h)),
            pl.BlockSpec((1, s, wv), lambda i, h: (i, 0, v_off + h)),
            pl.BlockSpec((1, s, wv), lambda i, h: (i, 0, o_off + h)),
            pl.BlockSpec((3, wqk), lambda i, h: (0, h)),
            pl.BlockSpec((3, wqk), lambda i, h: (0, k_off + h)),
            pl.BlockSpec((1, hps, 4, s), lambda i, h: (i, h, 0, 0)),
            pl.BlockSpec((1, wv), lambda i, h: (0, h)),
        ],
        out_specs=pl.BlockSpec((1, s, wv), lambda i, h: (i, 0, h)),
        out_shape=jax.ShapeDtypeStruct((b, s, nh * MLSTM_DV), BF16),
        scratch_shapes=[
            pltpu.VMEM((s, wqk), BF16),
            pltpu.VMEM((s, wqk), BF16),
            pltpu.VMEM((s, wv), F32),
            pltpu.VMEM((s, wv), F32),
        ],
        compiler_params=_params("parallel", "parallel"),
        name="mlstm_core",
    )(p, p, p, p, conv_w, conv_w, gates_rows, out_norm.reshape(1, -1))


def mlstm_mixer(x2d, norm_g, w_in, conv_w, gate_b, out_norm, w_out, bsz, s):
    nh = MLSTM_HEADS
    main_cols = 2 * nh * MLSTM_DQK + 2 * nh * MLSTM_DV
    n_gate = 4 * nh
    all_cols = main_cols + LANES
    w_all = jnp.pad(w_in.astype(BF16), ((0, 0), (0, all_cols - w_in.shape[1])))
    p = norm_matmul(x2d, norm_g, w_all, kdim=x2d.shape[1], tm=MM_TM, tn=all_cols // MLSTM_PROJ_SPLIT)
    gates = p[:, main_cols:main_cols + n_gate] + gate_b.astype(F32)
    gates_rows = gates.reshape(bsz, s, 4, nh).transpose(0, 3, 2, 1)
    hg = mlstm_core(p.reshape(bsz, s, all_cols), conv_w, gates_rows, out_norm)
    return matmul_res(hg.reshape(bsz * s, nh * MLSTM_DV), w_out.astype(BF16), x2d, tm=MM_TM, tn=MM_TN)


MLA_TQ = 256


def _mla_kernel(qn_ref, qr_ref, kv_ref, kr_ref, cos_ref, sin_ref, gqn_ref, gqr_ref, gkn_ref, gkr_ref,
                out_ref, qs_ref, ks_ref, vs_ref):
    s = qn_ref.shape[1]
    dqk = MLA_NOPE + MLA_ROPE
    half = MLA_ROPE // 2
    cos = cos_ref[...]
    sin = sin_ref[...]
    lane = lax.broadcasted_iota(jnp.int32, (1, LANES), 1)
    first_half = (lane // half) % 2 == 0

    def rope(x):
        rot = jnp.where(first_half, -pltpu.roll(x, LANES - half, axis=1), pltpu.roll(x, half, axis=1))
        return x * cos + rot * sin

    qr_raw = qr_ref[0]
    kr_raw = kr_ref[0]
    q_rope = rope(qr_raw * gqr_ref[...])
    k_rope = rope(kr_raw * gkr_ref[...])
    kr_ss = 0.5 * jnp.sum(kr_raw * kr_raw, axis=-1, keepdims=True)
    for hh in range(2):
        own = (lane // MLA_ROPE) == hh
        qn = qn_ref[0, :, hh * MLA_NOPE:(hh + 1) * MLA_NOPE]
        kn = kv_ref[0, :, hh * 2 * LANES:hh * 2 * LANES + MLA_NOPE]
        vs_ref[...] = kv_ref[0, :, hh * 2 * LANES + MLA_NOPE:(hh + 1) * 2 * LANES].astype(BF16)
        q_ss = (jnp.sum(qn * qn, axis=-1, keepdims=True)
                + jnp.sum(jnp.where(own, qr_raw * qr_raw, 0.0), axis=-1, keepdims=True))
        rq = lax.rsqrt(q_ss * (1.0 / dqk) + EPS) * (dqk ** -0.5)
        rk = lax.rsqrt((jnp.sum(kn * kn, axis=-1, keepdims=True) + kr_ss) * (1.0 / dqk) + EPS)
        qs_ref[:, 0:LANES] = (qn * rq * gqn_ref[...]).astype(BF16)
        qs_ref[:, LANES:2 * LANES] = (jnp.where(own, q_rope, 0.0) * rq).astype(BF16)
        ks_ref[:, 0:LANES] = (kn * rk * gkn_ref[...]).astype(BF16)
        ks_ref[:, LANES:2 * LANES] = (k_rope * rk).astype(BF16)

        def qblock(i, _):
            r0 = pl.multiple_of(i * MLA_TQ, MLA_TQ)
            sc = lax.dot_general(qs_ref[pl.ds(r0, MLA_TQ), :], ks_ref[...], NT_DIMS, preferred_element_type=F32)
            mx = jnp.max(sc, axis=-1, keepdims=True)
            e = jnp.exp(sc - mx)
            den = jnp.sum(e, axis=-1, keepdims=True)
            o = jnp.dot(e.astype(BF16), vs_ref[...], preferred_element_type=F32) / den
            out_ref[0, pl.ds(r0, MLA_TQ), hh * MLA_DV:(hh + 1) * MLA_DV] = o.astype(out_ref.dtype)
            return 0

        lax.fori_loop(0, s // MLA_TQ, qblock, 0, unroll=2)


def mla_attention(qall, kv, kr, cos, sin, q_norm, k_norm):
    b, s, _ = qall.shape
    nh = MLA_HEADS
    qr_off = nh * MLA_NOPE // (2 * MLA_ROPE)
    row = lambda t: t.reshape(1, -1).astype(F32)
    twice = lambda t: jnp.concatenate([t, t], axis=-1)
    vec = pl.BlockSpec((1, LANES), lambda i, h: (0, 0))
    cos = jnp.tile(cos, (1, LANES // cos.shape[1]))
    sin = jnp.tile(sin, (1, LANES // sin.shape[1]))
    return pl.pallas_call(
        _mla_kernel,
        grid=(b, nh // 2),
        in_specs=[
            pl.BlockSpec((1, s, 2 * MLA_NOPE), lambda i, h: (i, 0, h)),
            pl.BlockSpec((1, s, 2 * MLA_ROPE), lambda i, h: (i, 0, qr_off + h)),
            pl.BlockSpec((1, s, 4 * LANES), lambda i, h: (i, 0, h)),
            pl.BlockSpec((1, s, LANES), lambda i, h: (i, 0, 0)),
            pl.BlockSpec((s, LANES), lambda i, h: (0, 0)),
            pl.BlockSpec((s, LANES), lambda i, h: (0, 0)),
            vec, vec, vec, vec,
        ],
        out_specs=pl.BlockSpec((1, s, 2 * MLA_DV), lambda i, h: (i, 0, h)),
        out_shape=jax.ShapeDtypeStruct((b, s, nh * MLA_DV), BF16),
        scratch_shapes=[
            pltpu.VMEM((s, 2 * LANES), BF16),
            pltpu.VMEM((s, 2 * LANES), BF16),
            pltpu.VMEM((s, MLA_DV), BF16),
        ],
        compiler_params=_params("parallel", "parallel"),
        name="mla_attention",
    )(qall, qall, kv, twice(kr), cos, sin,
      row(q_norm[:MLA_NOPE]), twice(row(q_norm[MLA_NOPE:])), row(k_norm[:MLA_NOPE]), twice(row(k_norm[MLA_NOPE:])))


def mla_mixer(x2d, norm_g, w_in, kv_norm, w_ukv, q_norm, k_norm, w_out, bsz, s):
    nh = MLA_HEADS
    dqk = MLA_NOPE + MLA_ROPE
    d = x2d.shape[1]
    w_q = w_in[:, :nh * dqk].reshape(d, nh, dqk)
    w_qcat = jnp.concatenate([w_q[:, :, :MLA_NOPE].reshape(d, nh * MLA_NOPE),
                              w_q[:, :, MLA_NOPE:].reshape(d, nh * MLA_ROPE)], axis=1).astype(BF16)
    qall = norm_matmul(x2d, norm_g, w_qcat, kdim=d, tm=MM_TM, tn=MM_TN)
    lat = norm_matmul(x2d, norm_g, w_in[:, nh * dqk:].astype(BF16), kdim=d, tm=MM_TM, tn=MLA_KV_RANK + MLA_ROPE)
    kv = norm_matmul(lat, kv_norm, w_ukv.astype(BF16), kdim=MLA_KV_RANK, tm=MM_TM, tn=MM_TN)
    kr = lat[:, MLA_KV_RANK:]
    half = MLA_ROPE // 2
    inv = ROPE_BASE ** (-jnp.arange(half, dtype=F32) / half)
    ang = jnp.arange(s).astype(F32)[:, None] * inv[None, :]
    o = mla_attention(qall.reshape(bsz, s, -1), kv.reshape(bsz, s, -1), kr.reshape(bsz, s, MLA_ROPE),
                      jnp.cos(ang), jnp.sin(ang), q_norm, k_norm)
    return matmul_res(o.reshape(bsz * s, nh * MLA_DV), w_out.astype(BF16), x2d, tm=MM_TM, tn=MM_TN)


DIL_QB = 128
DIL_KW = 256
DIL_UNROLL = 8


def _dilated_kernel(*refs):
    n_g = len(DIL_GROUPS)
    qkv_refs = refs[:3 * n_g]
    gq_ref, gk_ref = refs[3 * n_g], refs[3 * n_g + 1]
    out_ref = refs[3 * n_g + 2]
    qs_ref, ks_ref, vs_ref = refs[3 * n_g + 3:3 * n_g + 6]
    acc_refs = refs[3 * n_g + 6:3 * n_g + 6 + n_g]
    m_refs = refs[3 * n_g + 6 + n_g:3 * n_g + 6 + 2 * n_g]
    l_refs = refs[3 * n_g + 6 + 2 * n_g:3 * n_g + 6 + 3 * n_g]
    s = out_ref.shape[1]
    head = pl.program_id(1)
    scale = DIL_DH ** -0.5

    for gi, (_, dil) in enumerate(DIL_GROUPS):
        q_ref, k_ref, v_ref = qkv_refs[3 * gi:3 * gi + 3]
        n = s // dil
        kw = min(DIL_KW, n)
        slope_arg = jnp.full((1, 1), -8.0 / (n_g * DIL_HEADS), F32) * (gi * DIL_HEADS + head + 1).astype(F32)
        bias_step = jnp.exp2(slope_arg) * float(dil)
        gq = gq_ref[gi:gi + 1, :]
        gk = gk_ref[gi:gi + 1, :]

        def residue(r, _, q_ref=q_ref, k_ref=k_ref, v_ref=v_ref, n=n, kw=kw, dil=dil, gi=gi,
                    bias_step=bias_step, gq=gq, gk=gk):
            def sub(ref):
                if dil == 1:
                    return ref[0]
                return ref[0, pl.ds(r, n, stride=dil), :]

            q = sub(q_ref)
            k = sub(k_ref)
            base = 0 if dil == 1 else pl.multiple_of(r * n, n)
            qs_ref[pl.ds(base, n), :] = (q * _rms_scale(q, DIL_DH) * gq * scale).astype(BF16)
            ks_ref[pl.ds(base, n), :] = (k * _rms_scale(k, DIL_DH) * gk).astype(BF16)
            vs_ref[pl.ds(base, n), :] = sub(v_ref).astype(BF16)
            for qb in range(n // DIL_QB):
                q0 = qb * DIL_QB
                k0 = min(max(q0 - DIL_SIDE, 0), n - kw)
                sc = lax.dot_general(qs_ref[pl.ds(base + q0, DIL_QB), :], ks_ref[pl.ds(base + k0, kw), :],
                                     NT_DIMS, preferred_element_type=F32)
                dist = jnp.abs((k0 + lax.broadcasted_iota(jnp.int32, (DIL_QB, kw), 1))
                               - (q0 + lax.broadcasted_iota(jnp.int32, (DIL_QB, kw), 0)))
                sc = sc - bias_step * dist.astype(F32)
                sc = jnp.where(dist <= DIL_SIDE, sc, -jnp.inf)
                mx = jnp.max(sc, axis=-1, keepdims=True)
                e = jnp.exp(sc - mx)
                den = jnp.sum(e, axis=-1, keepdims=True)
                acc = jnp.dot(e.astype(BF16), vs_ref[pl.ds(base + k0, kw), :], preferred_element_type=F32)
                if dil == 1:
                    rows = pl.ds(q0, DIL_QB)
                else:
                    rows = pl.ds(r + dil * q0, DIL_QB, stride=dil)
                acc_refs[gi][rows, :] = acc
                m_refs[gi][rows, :] = jnp.broadcast_to(mx, (DIL_QB, DIL_DH))
                l_refs[gi][rows, :] = jnp.broadcast_to(den, (DIL_QB, DIL_DH))
            return 0

        if dil == 1:
            residue(0, 0)
        else:
            lax.fori_loop(0, dil, residue, 0, unroll=min(dil, DIL_UNROLL))

    m_all = m_refs[0][...]
    for gi in range(1, n_g):
        m_all = jnp.maximum(m_all, m_refs[gi][...])
    num = jnp.zeros((s, DIL_DH), F32)
    den = jnp.zeros((s, DIL_DH), F32)
    for gi in range(n_g):
        w = jnp.exp(m_refs[gi][...] - m_all)
        num = num + acc_refs[gi][...] * w
        den = den + l_refs[gi][...] * w
    out_ref[0] = (num / den).astype(out_ref.dtype)


def dilated_attention(p, q_norm, k_norm):
    b, s, _ = p.shape
    n_g = len(DIL_GROUPS)
    nh = DIL_HEADS

    def spec(col0):
        return pl.BlockSpec((1, s, DIL_DH), lambda i, h: (i, 0, col0 + h))

    in_specs = [spec((gi * 3 + which) * nh) for gi in range(n_g) for which in range(3)]
    in_specs += [pl.BlockSpec((n_g, DIL_DH), lambda i, h: (0, 0))] * 2
    scratch = [pltpu.VMEM((s, DIL_DH), BF16)] * 3 + [pltpu.VMEM((s, DIL_DH), F32)] * (3 * n_g)
    return pl.pallas_call(
        _dilated_kernel,
        grid=(b, nh),
        in_specs=in_specs,
        out_specs=pl.BlockSpec((1, s, DIL_DH), lambda i, h: (i, 0, h)),
        out_shape=jax.ShapeDtypeStruct((b, s, nh * DIL_DH), BF16),
        scratch_shapes=scratch,
        compiler_params=_params("parallel", "parallel"),
        name="dilated_attention",
    )(*([p] * (3 * n_g)), q_norm.astype(F32), k_norm.astype(F32))


def dilated_mixer(x2d, norm_g, w_in, q_norm, k_norm, w_out, bsz, s):
    p = norm_matmul(x2d, norm_g, w_in.astype(BF16), kdim=x2d.shape[1], tm=MM_TM, tn=MM_TN)
    o = dilated_attention(p.reshape(bsz, s, -1), q_norm, k_norm)
    return matmul_res(o.reshape(bsz * s, -1), w_out.astype(BF16), x2d, tm=MM_TM, tn=MM_TN)


def _router_kernel(x_ref, g_ref, rw_ref, rb_ref, grp_ref, xrows_ref):
    x = x_ref[...]
    tm, d = x.shape
    xn = x * _rms_scale(x, d) * g_ref[...]
    n_sub = d // LANES
    n_rec = xrows_ref.shape[0] // tm
    for a in range(n_sub):
        xrows_ref[pl.ds(a, tm, stride=n_rec), :] = xn[:, a * LANES:(a + 1) * LANES]
    for a in range(n_sub + 1, n_rec):
        xrows_ref[pl.ds(a, tm, stride=n_rec), :] = jnp.zeros((tm, LANES), F32)
    def split(v):
        hi = v.astype(BF16)
        return hi, (v - hi.astype(F32)).astype(BF16)

    def nt(a, b):
        return lax.dot_general(a, b, NT_DIMS, preferred_element_type=F32)

    x_hi, x_lo = split(xn)
    w_hi, w_lo = split(rw_ref[...])
    logits = nt(w_hi, x_hi) + (nt(w_hi, x_lo) + nt(w_lo, x_hi))
    scores = jax.nn.sigmoid(logits)
    sel = scores + rb_ref[...]
    ng = N_GROUPS
    slots = range(EXPERTS_PER_GROUP)
    v = [sel[j * ng:(j + 1) * ng, :] for j in slots]
    sc = [scores[j * ng:(j + 1) * ng, :] for j in slots]

    def first_argmax(vals):
        best = vals[0]
        for t in vals[1:]:
            best = jnp.maximum(best, t)
        idx = jnp.full(best.shape, len(vals) - 1, jnp.int32)
        for j in reversed(range(len(vals) - 1)):
            idx = jnp.where(vals[j] == best, j, idx)
        return best, idx

    m1, i1 = first_argmax(v)
    m2, i2 = first_argmax([jnp.where(i1 == j, -jnp.inf, v[j]) for j in slots])
    grp_score = m1 + m2
    grp_iota = lax.broadcasted_iota(jnp.int32, grp_score.shape, 0)
    best = jnp.max(grp_score, axis=0, keepdims=True)
    grp = jnp.min(jnp.where(grp_score == best, grp_iota, ng), axis=0, keepdims=True)
    w1 = sum(jnp.where(i1 == j, sc[j], 0.0) for j in slots)
    w2 = sum(jnp.where(i2 == j, sc[j], 0.0) for j in slots)
    tot = w1 + w2
    chosen = grp_iota == grp

    def pick(v):
        return jnp.sum(jnp.where(chosen, v, jnp.zeros_like(v)), axis=0, keepdims=True)

    grp_ref[...] = grp
    gates = [pick(jnp.where(i1 == j, w1 / tot, 0.0) + jnp.where(i2 == j, w2 / tot, 0.0)) for j in slots]
    gates_t = jnp.concatenate(gates + [jnp.zeros((SUBLANES - len(gates), tm), F32)], axis=0)
    sel_row = lax.broadcasted_iota(jnp.int32, (SUBLANES, LANES), 0)
    sel_col = lax.broadcasted_iota(jnp.int32, (SUBLANES, LANES), 1)
    select = (sel_row == sel_col).astype(F32)
    xrows_ref[pl.ds(n_sub, tm, stride=n_rec), :] = lax.dot_general(
        gates_t, select, TN_DIMS, precision=lax.Precision.HIGHEST, preferred_element_type=F32)


def router_topk(x2d, norm_g, router_w, router_b, *, tm):
    t, d = x2d.shape
    ne = N_EXPERTS
    perm = jnp.arange(ne).reshape(N_GROUPS, EXPERTS_PER_GROUP).T.reshape(-1)
    rw_t = router_w.astype(F32).T[perm]
    rb = router_b.astype(F32)[perm].reshape(ne, 1)
    return pl.pallas_call(
        _router_kernel,
        grid=(t // tm,),
        in_specs=[
            pl.BlockSpec((tm, d), lambda i: (i, 0)),
            pl.BlockSpec((1, d), lambda i: (0, 0)),
            pl.BlockSpec((ne, d), lambda i: (0, 0)),
            pl.BlockSpec((ne, 1), lambda i: (0, 0)),
        ],
        out_specs=[pl.BlockSpec((1, tm), lambda i: (0, i)),
                   pl.BlockSpec((tm * MOE_REC, LANES), lambda i: (i, 0))],
        out_shape=[jax.ShapeDtypeStruct((1, t), jnp.int32),
                   jax.ShapeDtypeStruct((t * MOE_REC, LANES), F32)],
        compiler_params=_params("parallel"),
        name="router",
    )(x2d, norm_g.reshape(1, d), rw_t, rb)


def moe_plan(grp, tm):
    ne = N_GROUPS
    e_flat = grp.reshape(-1)
    n_tiles = -(-e_flat.shape[0] // tm) + ne + 1
    onehot = (e_flat[:, None] == jnp.arange(ne, dtype=jnp.int32)[None, :]).astype(jnp.int32)
    csum = jnp.cumsum(onehot, axis=0)
    counts = csum[-1]
    padded = (counts + tm - 1) // tm * tm
    ends = jnp.cumsum(padded)
    starts = ends - padded
    pos = jnp.sum(onehot * (csum - 1 + starts[None, :]), axis=1)
    tile_row0 = jnp.arange(n_tiles, dtype=jnp.int32) * tm
    tile_expert = jnp.minimum(jnp.sum((tile_row0[:, None] >= ends[None, :]).astype(jnp.int32), axis=1), ne - 1)
    n_used = (ends[-1] // tm).reshape(1)
    return pos.astype(jnp.int32), tile_expert.astype(jnp.int32), n_used.astype(jnp.int32), n_tiles


def _moe_group_kernel(pos_ref, tg_ref, nused_ref, xrows_hbm, wg_ref, wu_ref, wd_ref, ys_ref,
                      src_ref, xbuf, sem, xn_bf, gate_ref):
    i = pl.program_id(0)
    j = pl.program_id(1)
    n_slots = pl.num_programs(1)
    tm, d = xn_bf.shape
    n_sub = d // LANES
    n_rec = xbuf.shape[1] // tm
    n_used = nused_ref[0]
    n_tok = pos_ref.shape[0]
    first = j == 0

    def row_copy(tok, slot, r):
        return pltpu.make_async_copy(xrows_hbm.at[pl.ds(pl.multiple_of(tok * n_rec, SUBLANES), n_rec), :],
                                     xbuf.at[slot, pl.ds(pl.multiple_of(r * n_rec, SUBLANES), n_rec), :],
                                     sem.at[slot])

    def wait(slot):
        pltpu.make_async_copy(xrows_hbm.at[pl.ds(0, tm * n_rec), :], xbuf.at[slot], sem.at[slot]).wait()

    @pl.when((i == 0) & first)
    def _():
        def clear(r, _):
            src_ref[r] = 0
            return 0
        lax.fori_loop(0, src_ref.shape[0], clear, 0, unroll=DMA_UNROLL)

        def scatter(n, _):
            src_ref[pos_ref[n]] = n
            return 0
        lax.fori_loop(0, n_tok, scatter, 0, unroll=DMA_UNROLL)

        def head(r, _):
            row_copy(src_ref[r], 0, r).start()
            return 0
        lax.fori_loop(0, tm, head, 0, unroll=DMA_UNROLL)

    slot = i % 2

    @pl.when(first & (i <= n_used))
    def _():
        wait(slot)

    @pl.when(i < n_used)
    def _():
        @pl.when(first)
        def _():
            for a in range(n_sub):
                xn_bf[:, a * LANES:(a + 1) * LANES] = xbuf[slot, pl.ds(a, tm, stride=n_rec), :].astype(BF16)
            gate_ref[...] = xbuf[slot, pl.ds(n_sub, tm, stride=n_rec), :]

        share = tm // MOE_SLOTS
        parts = 4
        next_base = (i + 1) * tm + j * share

        def issue(q):
            for r in range(share * q // parts, share * (q + 1) // parts):
                row_copy(src_ref[next_base + r], 1 - slot, j * share + r).start()

        xn = xn_bf[...]
        issue(0)
        hg = jnp.dot(xn, wg_ref[0, 0].astype(BF16), preferred_element_type=F32)
        issue(1)
        hu = jnp.dot(xn, wu_ref[0, 0].astype(BF16), preferred_element_type=F32)
        issue(2)
        gates = gate_ref[...]
        lane = lax.broadcasted_iota(jnp.int32, gates.shape, 1)
        gate = jnp.sum(jnp.where(lane == j, gates, 0.0), axis=1, keepdims=True)
        a = hg * jax.nn.sigmoid(hg) * hu * gate
        issue(3)
        y = jnp.dot(a.astype(BF16), wd_ref[0, 0].astype(BF16), preferred_element_type=F32)

        @pl.when(first)
        def _():
            ys_ref[...] = y

        @pl.when(j > 0)
        def _():
            ys_ref[...] += y

    @pl.when(first & (i >= n_used))
    def _():
        ys_ref[...] = jnp.zeros_like(ys_ref)


def moe_experts(xrows, pos, tile_group, n_used, n_tiles, layer, w_gate, w_up, w_down, *, tm):
    d, f = w_gate.shape[-2:]
    ns = MOE_SLOTS
    wmap = lambda i, j, pos, tg, nu: (layer, tg[i] * ns + j, 0, 0)
    grid_spec = pltpu.PrefetchScalarGridSpec(
        num_scalar_prefetch=3,
        grid=(n_tiles, ns),
        in_specs=[
            pl.BlockSpec(memory_space=pl.ANY),
            pl.BlockSpec((1, 1, d, f), wmap),
            pl.BlockSpec((1, 1, d, f), wmap),
            pl.BlockSpec((1, 1, f, d), wmap),
        ],
        out_specs=pl.BlockSpec((tm, d), lambda i, j, pos, tg, nu: (i, 0)),
        scratch_shapes=[
            pltpu.SMEM((n_tiles * tm,), jnp.int32),
            pltpu.VMEM((2, tm * MOE_REC, LANES), F32),
            pltpu.SemaphoreType.DMA((2,)),
            pltpu.VMEM((tm, d), BF16),
            pltpu.VMEM((tm, LANES), F32),
        ],
    )
    return pl.pallas_call(
        _moe_group_kernel,
        grid_spec=grid_spec,
        out_shape=jax.ShapeDtypeStruct((n_tiles * tm, d), F32),
        compiler_params=pltpu.CompilerParams(dimension_semantics=("arbitrary", "arbitrary"),
                                             vmem_limit_bytes=MOE_VMEM_LIMIT_BYTES),
        name="moe_experts",
    )(pos, tile_group, n_used, xrows, w_gate, w_up, w_down)


def _moe_combine_kernel(pos_ref, x_ref, ys_hbm, o_ref, buf, sem):
    i = pl.program_id(0)
    tc = x_ref.shape[0]

    def row_copy(p, slot, r):
        return pltpu.make_async_copy(ys_hbm.at[pl.ds(p, 1), :], buf.at[slot, pl.ds(r, 1), :], sem.at[slot])

    def issue(tile, slot):
        def body(r, _):
            row_copy(pos_ref[tile * tc + r], slot, r).start()
            return 0
        lax.fori_loop(0, tc, body, 0, unroll=DMA_UNROLL)

    def wait(slot):
        pltpu.make_async_copy(ys_hbm.at[pl.ds(0, tc), :], buf.at[slot], sem.at[slot]).wait()

    @pl.when(i == 0)
    def _():
        issue(0, 0)

    slot = i % 2
    wait(slot)

    @pl.when(i + 1 < pl.num_programs(0))
    def _():
        issue(i + 1, 1 - slot)

    o_ref[...] = x_ref[...] + buf[slot]


def moe_combine(x2d, pos, ys, *, tc):
    t, d = x2d.shape
    grid_spec = pltpu.PrefetchScalarGridSpec(
        num_scalar_prefetch=1,
        grid=(t // tc,),
        in_specs=[
            pl.BlockSpec((tc, d), lambda i, pos: (i, 0)),
            pl.BlockSpec(memory_space=pl.ANY),
        ],
        out_specs=pl.BlockSpec((tc, d), lambda i, pos: (i, 0)),
        scratch_shapes=[pltpu.VMEM((2, tc, d), F32), pltpu.SemaphoreType.DMA((2,))],
    )
    return pl.pallas_call(
        _moe_combine_kernel,
        grid_spec=grid_spec,
        out_shape=jax.ShapeDtypeStruct((t, d), F32),
        compiler_params=_params("arbitrary"),
        name="moe_combine",
    )(pos, x2d, ys)


def moe_ffn(x2d, norm_g, router_w, router_b, layer, w_gate, w_up, w_down):
    grp, xrows = router_topk(x2d, norm_g, router_w, router_b, tm=512)
    pos, tile_group, n_used, n_tiles = moe_plan(grp, MOE_TM)
    ys = moe_experts(xrows, pos, tile_group, n_used, n_tiles, layer, w_gate, w_up, w_down, tm=MOE_TM)
    return moe_combine(x2d, pos, ys, tc=MOE_TC)


def kernel(x, norm_mix, norm_ffn, a_w_in, a_conv_w, a_gate_b, a_out_norm, a_w_out, b_w_in, b_kv_norm, b_w_ukv, b_q_norm, b_k_norm, b_w_out, c_w_in, c_q_norm, c_k_norm, c_w_out, router_w, router_b, moe_w_gate, moe_w_up, moe_w_down):
    bsz, s, d = x.shape
    depth = norm_mix.shape[0]
    x2d = x.reshape(bsz * s, d)
    for i in range(depth):
        kind = i % 3
        j = i // 3
        if kind == 0:
            x2d = mlstm_mixer(x2d, norm_mix[i], a_w_in[j], a_conv_w[j], a_gate_b[j], a_out_norm[j], a_w_out[j],
                              bsz, s)
        elif kind == 1:
            x2d = mla_mixer(x2d, norm_mix[i], b_w_in[j], b_kv_norm[j], b_w_ukv[j], b_q_norm[j], b_k_norm[j],
                            b_w_out[j], bsz, s)
        else:
            x2d = dilated_mixer(x2d, norm_mix[i], c_w_in[j], c_q_norm[j], c_k_norm[j], c_w_out[j], bsz, s)
        x2d = moe_ffn(x2d, norm_ffn[i], router_w, router_b, i, moe_w_gate, moe_w_up, moe_w_down)
    return x2d.reshape(bsz, s, d)
```

```python
import functools
import math

import jax
import jax.numpy as jnp
from jax import lax
from jax.experimental import pallas as pl
from jax.experimental.pallas import tpu as pltpu

F32 = jnp.float32
BF16 = jnp.bfloat16

EPS = 1e-6
LANES = 128
SUBLANES = 8
VMEM_LIMIT_BYTES = 56 * 1024 * 1024
MM_TM = 1024
MM_TN = 1024

MLSTM_HEADS = 8
MLSTM_DQK = 128
MLSTM_DV = 256
MLSTM_CHUNK = 128
MLSTM_HEADS_PER_STEP = 1

MLA_HEADS = 16
MLA_NOPE = 128
MLA_ROPE = 64
MLA_DV = 128
MLA_KV_RANK = 512
ROPE_BASE = 10000.0

DIL_GROUPS = ((128, 1), (512, 4), (2048, 16))
DIL_HEADS = 8
DIL_DH = 128
DIL_SIDE = 64

N_EXPERTS = 16
N_GROUPS = 4
EXPERTS_PER_GROUP = 4
MOE_SLOTS = EXPERTS_PER_GROUP
MOE_TM = 512
MOE_TC = 256
MOE_REC = 24
MOE_VMEM_LIMIT_BYTES = 60 * 1024 * 1024
DMA_UNROLL = 8

NT_DIMS = (((1,), (1,)), ((), ()))
TN_DIMS = (((0,), (0,)), ((), ()))


def _params(*sem):
    return pltpu.CompilerParams(dimension_semantics=sem, vmem_limit_bytes=VMEM_LIMIT_BYTES)


def _rms_scale(x, width):
    return lax.rsqrt(jnp.sum(x * x, axis=-1, keepdims=True) * (1.0 / width) + EPS)


def _norm_matmul_kernel(x_ref, g_ref, w_ref, o_ref, xn_ref):
    @pl.when(pl.program_id(1) == 0)
    def _():
        x = x_ref[...].astype(F32)
        xn_ref[...] = (x * _rms_scale(x, x.shape[-1]) * g_ref[...]).astype(BF16)

    o_ref[...] = jnp.dot(xn_ref[...], w_ref[...], preferred_element_type=F32).astype(o_ref.dtype)


def norm_matmul(x, g, w, *, kdim, tm, tn, out_dtype=F32):
    t = x.shape[0]
    n = w.shape[1]
    return pl.pallas_call(
        _norm_matmul_kernel,
        grid=(t // tm, n // tn),
        in_specs=[
            pl.BlockSpec((tm, kdim), lambda i, j: (i, 0)),
            pl.BlockSpec((1, kdim), lambda i, j: (0, 0)),
            pl.BlockSpec((kdim, tn), lambda i, j: (0, j)),
        ],
        out_specs=pl.BlockSpec((tm, tn), lambda i, j: (i, j)),
        out_shape=jax.ShapeDtypeStruct((t, n), out_dtype),
        scratch_shapes=[pltpu.VMEM((tm, kdim), BF16)],
        compiler_params=_params("parallel", "arbitrary"),
        name="norm_matmul",
    )(x, g.reshape(1, kdim), w)


def _matmul_res_kernel(a_ref, w_ref, r_ref, o_ref):
    o_ref[...] = r_ref[...] + jnp.dot(a_ref[...].astype(BF16), w_ref[...], preferred_element_type=F32)


def matmul_res(a, w, res, *, tm, tn):
    t, k = a.shape
    n = w.shape[1]
    return pl.pallas_call(
        _matmul_res_kernel,
        grid=(t // tm, n // tn),
        in_specs=[
            pl.BlockSpec((tm, k), lambda i, j: (i, 0)),
            pl.BlockSpec((k, tn), lambda i, j: (0, j)),
            pl.BlockSpec((tm, tn), lambda i, j: (i, j)),
        ],
        out_specs=pl.BlockSpec((tm, tn), lambda i, j: (i, j)),
        out_shape=jax.ShapeDtypeStruct((t, n), F32),
        compiler_params=_params("parallel", "arbitrary"),
        name="matmul_res",
    )(a, w, res)


def _log_sigmoid(x):
    return jnp.minimum(x, 0.0) - jnp.log(1.0 + jnp.exp(-jnp.abs(x)))


def _mlstm_kernel(q_ref, k_ref, v_ref, o_ref, cwq_ref, cwk_ref, gates_ref, onorm_ref, out_ref,
                  qs_ref, ks_ref, hf_ref, hb_ref):
    s = q_ref.shape[1]
    chunk = MLSTM_CHUNK
    n_chunks = s // chunk
    row = lax.broadcasted_iota(jnp.int32, (s, 1), 0)

    def conv_silu(p, w):
        prev = jnp.where(row == 0, 0.0, pltpu.roll(p, 1, axis=0))
        nxt = jnp.where(row == s - 1, 0.0, pltpu.roll(p, s - 1, axis=0))
        y = w[0:1, :] * prev + w[1:2, :] * p + w[2:3, :] * nxt
        return y * jax.nn.sigmoid(y)

    qs_ref[...] = (conv_silu(q_ref[0], cwq_ref[...]) * (MLSTM_DQK ** -0.5)).astype(BF16)
    ks_ref[...] = conv_silu(k_ref[0], cwk_ref[...]).astype(BF16)

    ti = lax.broadcasted_iota(jnp.int32, (chunk, chunk), 0)
    si = lax.broadcasted_iota(jnp.int32, (chunk, chunk), 1)
    eye = ti == si

    def to_col(r):
        return jnp.sum(jnp.where(eye, jnp.broadcast_to(r, (chunk, chunk)), 0.0), axis=1, keepdims=True)

    def to_row(c):
        return jnp.sum(jnp.where(eye, jnp.broadcast_to(c, (chunk, chunk)), 0.0), axis=0, keepdims=True)

    def chunk_step(c, hh, i_idx, f_idx, reverse, carry, hdir_ref):
        mask = (si >= ti) if reverse else (si <= ti)
        c_st, n_st, m_st = carry
        r0 = pl.multiple_of(c * chunk, chunk)
        q = qs_ref[pl.ds(r0, chunk), hh * MLSTM_DQK:(hh + 1) * MLSTM_DQK]
        k = ks_ref[pl.ds(r0, chunk), hh * MLSTM_DQK:(hh + 1) * MLSTM_DQK]
        v = v_ref[0, pl.ds(r0, chunk), hh * MLSTM_DV:(hh + 1) * MLSTM_DV].astype(BF16)
        li = gates_ref[0, hh, i_idx:i_idx + 1, pl.ds(r0, chunk)]
        lf = _log_sigmoid(gates_ref[0, hh, f_idx:f_idx + 1, pl.ds(r0, chunk)])
        b_col = jnp.sum(jnp.where(mask, jnp.broadcast_to(lf, (chunk, chunk)), 0.0), axis=1, keepdims=True)
        b_row = to_row(b_col)
        g = jnp.sum(lf, axis=1, keepdims=True)
        dlog = jnp.where(mask, b_col - b_row + li, -jnp.inf)
        m_intra = jnp.max(dlog, axis=1, keepdims=True)
        a_inter = b_col + m_st
        m = jnp.maximum(a_inter, m_intra)
        e_inter = jnp.exp(a_inter - m)
        s_qk = lax.dot_general(q, k, NT_DIMS, preferred_element_type=F32)
        dmat = jnp.exp(dlog - m) * s_qk
        num = (e_inter * jnp.dot(q, c_st.astype(BF16), preferred_element_type=F32)
               + jnp.dot(dmat.astype(BF16), v, preferred_element_type=F32))
        den = (e_inter * jnp.sum(q.astype(F32) * n_st, axis=1, keepdims=True)
               + jnp.sum(dmat, axis=1, keepdims=True))
        hdir_ref[pl.ds(r0, chunk), hh * MLSTM_DV:(hh + 1) * MLSTM_DV] = (
            num / jnp.maximum(jnp.abs(den), jnp.exp(-m)))
        w_row = g - b_row + li
        m_loc = jnp.max(w_row, axis=1, keepdims=True)
        ew_col = to_col(jnp.exp(w_row - m_loc))
        kw = k.astype(F32) * ew_col
        c_loc = lax.dot_general(kw.astype(BF16), v, TN_DIMS, preferred_element_type=F32)
        n_loc = jnp.sum(kw, axis=0, keepdims=True)
        m_new = jnp.maximum(g + m_st, m_loc)
        a = jnp.exp(g + m_st - m_new)
        e = jnp.exp(m_loc - m_new)
        return a * c_st + e * c_loc, a * n_st + e * n_loc, m_new

    def body(ci, carry):
        out = []
        for hh in range(MLSTM_HEADS_PER_STEP):
            out.append(chunk_step(ci, hh, 0, 1, False, carry[2 * hh], hf_ref))
            out.append(chunk_step(n_chunks - 1 - ci, hh, 2, 3, True, carry[2 * hh + 1], hb_ref))
        return tuple(out)

    init = (jnp.zeros((MLSTM_DQK, MLSTM_DV), F32), jnp.zeros((1, MLSTM_DQK), F32), jnp.zeros((1, 1), F32))
    lax.fori_loop(0, n_chunks, body, (init,) * (2 * MLSTM_HEADS_PER_STEP), unroll=2)

    for hh in range(MLSTM_HEADS_PER_STEP):
        cols = slice(hh * MLSTM_DV, (hh + 1) * MLSTM_DV)
        h = hf_ref[:, cols] + hb_ref[:, cols]
        y = h * _rms_scale(h, MLSTM_DV) * onorm_ref[:, cols]
        out_ref[0, :, cols] = (jax.nn.sigmoid(o_ref[0, :, cols]) * y).astype(out_ref.dtype)


def mlstm_core(p, conv_w, gates_rows, out_norm):
    b, s, _ = p.shape
    nh = MLSTM_HEADS
    hps = MLSTM_HEADS_PER_STEP
    wqk, wv = hps * MLSTM_DQK, hps * MLSTM_DV
    k_off = nh * MLSTM_DQK // wqk
    v_off = 2 * nh * MLSTM_DQK // wv
    o_off = v_off + nh // hps
    return pl.pallas_call(
        _mlstm_kernel,
        grid=(b, nh // hps),
        in_specs=[
            pl.BlockSpec((1, s, wqk), lambda i, h: (i, 0, h)),
            pl.BlockSpec((1, s, wqk), lambda i, h: (i, 0, k_off + h)),
            pl.BlockSpec((1, s, wv), lambda i, h: (i, 0, v_off + h)),
            pl.BlockSpec((1, s, wv), lambda i, h: (i, 0, o_off + h)),
            pl.BlockSpec((3, wqk), lambda i, h: (0, h)),
            pl.BlockSpec((3, wqk), lambda i, h: (0, k_off + h)),
            pl.BlockSpec((1, hps, 4, s), lambda i, h: (i, h, 0, 0)),
            pl.BlockSpec((1, wv), lambda i, h: (0, h)),
        ],
        out_specs=pl.BlockSpec((1, s, wv), lambda i, h: (i, 0, h)),
        out_shape=jax.ShapeDtypeStruct((b, s, nh * MLSTM_DV), BF16),
        scratch_shapes=[
            pltpu.VMEM((s, wqk), BF16),
            pltpu.VMEM((s, wqk), BF16),
            pltpu.VMEM((s, wv), F32),
            pltpu.VMEM((s, wv), F32),
        ],
        compiler_params=_params("parallel", "parallel"),
        name="mlstm_core",
    )(p, p, p, p, conv_w, conv_w, gates_rows, out_norm.reshape(1, -1))


def mlstm_mixer(x2d, norm_g, w_in, conv_w, gate_b, out_norm, w_out, bsz, s):
    nh = MLSTM_HEADS
    main_cols = 2 * nh * MLSTM_DQK + 2 * nh * MLSTM_DV
    p = norm_matmul(x2d, norm_g, w_in[:, :main_cols].astype(BF16), kdim=x2d.shape[1], tm=MM_TM, tn=MM_TN)
    gates = norm_matmul(x2d, norm_g, w_in[:, main_cols:].astype(BF16), kdim=x2d.shape[1], tm=MM_TM, tn=4 * nh)
    gates = gates + gate_b.astype(F32)
    gates_rows = gates.reshape(bsz, s, 4, nh).transpose(0, 3, 2, 1)
    hg = mlstm_core(p.reshape(bsz, s, main_cols), conv_w, gates_rows, out_norm)
    return matmul_res(hg.reshape(bsz * s, nh * MLSTM_DV), w_out.astype(BF16), x2d, tm=MM_TM, tn=MM_TN)


MLA_TQ = 256


def _mla_kernel(qn_ref, qr_ref, kv_ref, kr_ref, cos_ref, sin_ref, gqn_ref, gqr_ref, gkn_ref, gkr_ref,
                out_ref, qs_ref, ks_ref, vs_ref):
    s = qn_ref.shape[1]
    dqk = MLA_NOPE + MLA_ROPE
    half = MLA_ROPE // 2
    cos = cos_ref[...]
    sin = sin_ref[...]
    lane = lax.broadcasted_iota(jnp.int32, (1, LANES), 1)
    first_half = (lane // half) % 2 == 0

    def rope(x):
        rot = jnp.where(first_half, -pltpu.roll(x, LANES - half, axis=1), pltpu.roll(x, half, axis=1))
        return x * cos + rot * sin

    qr_raw = qr_ref[0]
    kr_raw = kr_ref[0]
    q_rope = rope(qr_raw * gqr_ref[...])
    k_rope = rope(kr_raw * gkr_ref[...])
    kr_ss = 0.5 * jnp.sum(kr_raw * kr_raw, axis=-1, keepdims=True)
    for hh in range(2):
        own = (lane // MLA_ROPE) == hh
        qn = qn_ref[0, :, hh * MLA_NOPE:(hh + 1) * MLA_NOPE]
        kn = kv_ref[0, :, hh * 2 * LANES:hh * 2 * LANES + MLA_NOPE]
        vs_ref[...] = kv_ref[0, :, hh * 2 * LANES + MLA_NOPE:(hh + 1) * 2 * LANES].astype(BF16)
        q_ss = (jnp.sum(qn * qn, axis=-1, keepdims=True)
                + jnp.sum(jnp.where(own, qr_raw * qr_raw, 0.0), axis=-1, keepdims=True))
        rq = lax.rsqrt(q_ss * (1.0 / dqk) + EPS) * (dqk ** -0.5)
        rk = lax.rsqrt((jnp.sum(kn * kn, axis=-1, keepdims=True) + kr_ss) * (1.0 / dqk) + EPS)
        qs_ref[:, 0:LANES] = (qn * rq * gqn_ref[...]).astype(BF16)
        qs_ref[:, LANES:2 * LANES] = (jnp.where(own, q_rope, 0.0) * rq).astype(BF16)
        ks_ref[:, 0:LANES] = (kn * rk * gkn_ref[...]).astype(BF16)
        ks_ref[:, LANES:2 * LANES] = (k_rope * rk).astype(BF16)

        def qblock(i, _):
            r0 = pl.multiple_of(i * MLA_TQ, MLA_TQ)
            sc = lax.dot_general(qs_ref[pl.ds(r0, MLA_TQ), :], ks_ref[...], NT_DIMS, preferred_element_type=F32)
            mx = jnp.max(sc, axis=-1, keepdims=True)
            e = jnp.exp(sc - mx)
            den = jnp.sum(e, axis=-1, keepdims=True)
            o = jnp.dot(e.astype(BF16), vs_ref[...], preferred_element_type=F32) / den
            out_ref[0, pl.ds(r0, MLA_TQ), hh * MLA_DV:(hh + 1) * MLA_DV] = o.astype(out_ref.dtype)
            return 0

        lax.fori_loop(0, s // MLA_TQ, qblock, 0, unroll=4)


def mla_attention(qall, kv, kr, cos, sin, q_norm, k_norm):
    b, s, _ = qall.shape
    nh = MLA_HEADS
    qr_off = nh * MLA_NOPE // (2 * MLA_ROPE)
    row = lambda t: t.reshape(1, -1).astype(F32)
    twice = lambda t: jnp.concatenate([t, t], axis=-1)
    vec = pl.BlockSpec((1, LANES), lambda i, h: (0, 0))
    cos = jnp.tile(cos, (1, LANES // cos.shape[1]))
    sin = jnp.tile(sin, (1, LANES // sin.shape[1]))
    return pl.pallas_call(
        _mla_kernel,
        grid=(b, nh // 2),
        in_specs=[
            pl.BlockSpec((1, s, 2 * MLA_NOPE), lambda i, h: (i, 0, h)),
            pl.BlockSpec((1, s, 2 * MLA_ROPE), lambda i, h: (i, 0, qr_off + h)),
            pl.BlockSpec((1, s, 4 * LANES), lambda i, h: (i, 0, h)),
            pl.BlockSpec((1, s, LANES), lambda i, h: (i, 0, 0)),
            pl.BlockSpec((s, LANES), lambda i, h: (0, 0)),
            pl.BlockSpec((s, LANES), lambda i, h: (0, 0)),
            vec, vec, vec, vec,
        ],
        out_specs=pl.BlockSpec((1, s, 2 * MLA_DV), lambda i, h: (i, 0, h)),
        out_shape=jax.ShapeDtypeStruct((b, s, nh * MLA_DV), BF16),
        scratch_shapes=[
            pltpu.VMEM((s, 2 * LANES), BF16),
            pltpu.VMEM((s, 2 * LANES), BF16),
            pltpu.VMEM((s, MLA_DV), BF16),
        ],
        compiler_params=_params("parallel", "parallel"),
        name="mla_attention",
    )(qall, qall, kv, twice(kr), cos, sin,
      row(q_norm[:MLA_NOPE]), twice(row(q_norm[MLA_NOPE:])), row(k_norm[:MLA_NOPE]), twice(row(k_norm[MLA_NOPE:])))


def mla_mixer(x2d, norm_g, w_in, kv_norm, w_ukv, q_norm, k_norm, w_out, bsz, s):
    nh = MLA_HEADS
    dqk = MLA_NOPE + MLA_ROPE
    d = x2d.shape[1]
    w_q = w_in[:, :nh * dqk].reshape(d, nh, dqk)
    w_qcat = jnp.concatenate([w_q[:, :, :MLA_NOPE].reshape(d, nh * MLA_NOPE),
                              w_q[:, :, MLA_NOPE:].reshape(d, nh * MLA_ROPE)], axis=1).astype(BF16)
    qall = norm_matmul(x2d, norm_g, w_qcat, kdim=d, tm=MM_TM, tn=MM_TN)
    lat = norm_matmul(x2d, norm_g, w_in[:, nh * dqk:].astype(BF16), kdim=d, tm=MM_TM, tn=MLA_KV_RANK + MLA_ROPE)
    kv = norm_matmul(lat, kv_norm, w_ukv.astype(BF16), kdim=MLA_KV_RANK, tm=MM_TM, tn=MM_TN)
    kr = lat[:, MLA_KV_RANK:]
    half = MLA_ROPE // 2
    inv = ROPE_BASE ** (-jnp.arange(half, dtype=F32) / half)
    ang = jnp.arange(s).astype(F32)[:, None] * inv[None, :]
    o = mla_attention(qall.reshape(bsz, s, -1), kv.reshape(bsz, s, -1), kr.reshape(bsz, s, MLA_ROPE),
                      jnp.cos(ang), jnp.sin(ang), q_norm, k_norm)
    return matmul_res(o.reshape(bsz * s, nh * MLA_DV), w_out.astype(BF16), x2d, tm=MM_TM, tn=MM_TN)


DIL_QB = 128
DIL_KW = 256
DIL_UNROLL = 8


def _dilated_kernel(*refs):
    n_g = len(DIL_GROUPS)
    qkv_refs = refs[:3 * n_g]
    gq_ref, gk_ref = refs[3 * n_g], refs[3 * n_g + 1]
    out_ref = refs[3 * n_g + 2]
    qs_ref, ks_ref, vs_ref = refs[3 * n_g + 3:3 * n_g + 6]
    acc_refs = refs[3 * n_g + 6:3 * n_g + 6 + n_g]
    m_refs = refs[3 * n_g + 6 + n_g:3 * n_g + 6 + 2 * n_g]
    l_refs = refs[3 * n_g + 6 + 2 * n_g:3 * n_g + 6 + 3 * n_g]
    s = out_ref.shape[1]
    head = pl.program_id(1)
    scale = DIL_DH ** -0.5

    for gi, (_, dil) in enumerate(DIL_GROUPS):
        q_ref, k_ref, v_ref = qkv_refs[3 * gi:3 * gi + 3]
        n = s // dil
        kw = min(DIL_KW, n)
        slope_arg = jnp.full((1, 1), -8.0 / (n_g * DIL_HEADS), F32) * (gi * DIL_HEADS + head + 1).astype(F32)
        bias_step = jnp.exp2(slope_arg) * float(dil)
        gq = gq_ref[gi:gi + 1, :]
        gk = gk_ref[gi:gi + 1, :]

        def residue(r, _, q_ref=q_ref, k_ref=k_ref, v_ref=v_ref, n=n, kw=kw, dil=dil, gi=gi,
                    bias_step=bias_step, gq=gq, gk=gk):
            def sub(ref):
                if dil == 1:
                    return ref[0]
                return ref[0, pl.ds(r, n, stride=dil), :]

            q = sub(q_ref)
            k = sub(k_ref)
            base = 0 if dil == 1 else pl.multiple_of(r * n, n)
            qs_ref[pl.ds(base, n), :] = (q * _rms_scale(q, DIL_DH) * gq * scale).astype(BF16)
            ks_ref[pl.ds(base, n), :] = (k * _rms_scale(k, DIL_DH) * gk).astype(BF16)
            vs_ref[pl.ds(base, n), :] = sub(v_ref).astype(BF16)
            for qb in range(n // DIL_QB):
                q0 = qb * DIL_QB
                k0 = min(max(q0 - DIL_SIDE, 0), n - kw)
                sc = lax.dot_general(qs_ref[pl.ds(base + q0, DIL_QB), :], ks_ref[pl.ds(base + k0, kw), :],
                                     NT_DIMS, preferred_element_type=F32)
                dist = jnp.abs((k0 + lax.broadcasted_iota(jnp.int32, (DIL_QB, kw), 1))
                               - (q0 + lax.broadcasted_iota(jnp.int32, (DIL_QB, kw), 0)))
                sc = sc - bias_step * dist.astype(F32)
                sc = jnp.where(dist <= DIL_SIDE, sc, -jnp.inf)
                mx = jnp.max(sc, axis=-1, keepdims=True)
                e = jnp.exp(sc - mx)
                den = jnp.sum(e, axis=-1, keepdims=True)
                acc = jnp.dot(e.astype(BF16), vs_ref[pl.ds(base + k0, kw), :], preferred_element_type=F32)
                if dil == 1:
                    rows = pl.ds(q0, DIL_QB)
                else:
                    rows = pl.ds(r + dil * q0, DIL_QB, stride=dil)
                acc_refs[gi][rows, :] = acc
                m_refs[gi][rows, :] = jnp.broadcast_to(mx, (DIL_QB, DIL_DH))
                l_refs[gi][rows, :] = jnp.broadcast_to(den, (DIL_QB, DIL_DH))
            return 0

        if dil == 1:
            residue(0, 0)
        else:
            lax.fori_loop(0, dil, residue, 0, unroll=min(dil, DIL_UNROLL))

    m_all = m_refs[0][...]
    for gi in range(1, n_g):
        m_all = jnp.maximum(m_all, m_refs[gi][...])
    num = jnp.zeros((s, DIL_DH), F32)
    den = jnp.zeros((s, DIL_DH), F32)
    for gi in range(n_g):
        w = jnp.exp(m_refs[gi][...] - m_all)
        num = num + acc_refs[gi][...] * w
        den = den + l_refs[gi][...] * w
    out_ref[0] = (num / den).astype(out_ref.dtype)


def dilated_attention(p, q_norm, k_norm):
    b, s, _ = p.shape
    n_g = len(DIL_GROUPS)
    nh = DIL_HEADS

    def spec(col0):
        return pl.BlockSpec((1, s, DIL_DH), lambda i, h: (i, 0, col0 + h))

    in_specs = [spec((gi * 3 + which) * nh) for gi in range(n_g) for which in range(3)]
    in_specs += [pl.BlockSpec((n_g, DIL_DH), lambda i, h: (0, 0))] * 2
    scratch = [pltpu.VMEM((s, DIL_DH), BF16)] * 3 + [pltpu.VMEM((s, DIL_DH), F32)] * (3 * n_g)
    return pl.pallas_call(
        _dilated_kernel,
        grid=(b, nh),
        in_specs=in_specs,
        out_specs=pl.BlockSpec((1, s, DIL_DH), lambda i, h: (i, 0, h)),
        out_shape=jax.ShapeDtypeStruct((b, s, nh * DIL_DH), BF16),
        scratch_shapes=scratch,
        compiler_params=_params("parallel", "parallel"),
        name="dilated_attention",
    )(*([p] * (3 * n_g)), q_norm.astype(F32), k_norm.astype(F32))


def dilated_mixer(x2d, norm_g, w_in, q_norm, k_norm, w_out, bsz, s):
    p = norm_matmul(x2d, norm_g, w_in.astype(BF16), kdim=x2d.shape[1], tm=MM_TM, tn=MM_TN)
    o = dilated_attention(p.reshape(bsz, s, -1), q_norm, k_norm)
    return matmul_res(o.reshape(bsz * s, -1), w_out.astype(BF16), x2d, tm=MM_TM, tn=MM_TN)


def _router_kernel(x_ref, g_ref, rw_ref, rb_ref, grp_ref, xrows_ref):
    x = x_ref[...]
    tm, d = x.shape
    xn = x * _rms_scale(x, d) * g_ref[...]
    n_sub = d // LANES
    n_rec = xrows_ref.shape[0] // tm
    for a in range(n_sub):
        xrows_ref[pl.ds(a, tm, stride=n_rec), :] = xn[:, a * LANES:(a + 1) * LANES]
    for a in range(n_sub + 1, n_rec):
        xrows_ref[pl.ds(a, tm, stride=n_rec), :] = jnp.zeros((tm, LANES), F32)
    def split(v):
        hi = v.astype(BF16)
        return hi, (v - hi.astype(F32)).astype(BF16)

    def nt(a, b):
        return lax.dot_general(a, b, NT_DIMS, preferred_element_type=F32)

    x_hi, x_lo = split(xn)
    w_hi, w_lo = split(rw_ref[...])
    logits = nt(w_hi, x_hi) + (nt(w_hi, x_lo) + nt(w_lo, x_hi))
    scores = jax.nn.sigmoid(logits)
    sel = scores + rb_ref[...]
    ng = N_GROUPS
    slots = range(EXPERTS_PER_GROUP)
    v = [sel[j * ng:(j + 1) * ng, :] for j in slots]
    sc = [scores[j * ng:(j + 1) * ng, :] for j in slots]

    def first_argmax(vals):
        best = vals[0]
        for t in vals[1:]:
            best = jnp.maximum(best, t)
        idx = jnp.full(best.shape, len(vals) - 1, jnp.int32)
        for j in reversed(range(len(vals) - 1)):
            idx = jnp.where(vals[j] == best, j, idx)
        return best, idx

    m1, i1 = first_argmax(v)
    m2, i2 = first_argmax([jnp.where(i1 == j, -jnp.inf, v[j]) for j in slots])
    grp_score = m1 + m2
    grp_iota = lax.broadcasted_iota(jnp.int32, grp_score.shape, 0)
    best = jnp.max(grp_score, axis=0, keepdims=True)
    grp = jnp.min(jnp.where(grp_score == best, grp_iota, ng), axis=0, keepdims=True)
    w1 = sum(jnp.where(i1 == j, sc[j], 0.0) for j in slots)
    w2 = sum(jnp.where(i2 == j, sc[j], 0.0) for j in slots)
    tot = w1 + w2
    chosen = grp_iota == grp

    def pick(v):
        return jnp.sum(jnp.where(chosen, v, jnp.zeros_like(v)), axis=0, keepdims=True)

    grp_ref[...] = grp
    gates = [pick(jnp.where(i1 == j, w1 / tot, 0.0) + jnp.where(i2 == j, w2 / tot, 0.0)) for j in slots]
    gates_t = jnp.concatenate(gates + [jnp.zeros((SUBLANES - len(gates), tm), F32)], axis=0)
    sel_row = lax.broadcasted_iota(jnp.int32, (SUBLANES, LANES), 0)
    sel_col = lax.broadcasted_iota(jnp.int32, (SUBLANES, LANES), 1)
    select = (sel_row == sel_col).astype(F32)
    xrows_ref[pl.ds(n_sub, tm, stride=n_rec), :] = lax.dot_general(
        gates_t, select, TN_DIMS, precision=lax.Precision.HIGHEST, preferred_element_type=F32)


def router_topk(x2d, norm_g, router_w, router_b, *, tm):
    t, d = x2d.shape
    ne = N_EXPERTS
    perm = jnp.arange(ne).reshape(N_GROUPS, EXPERTS_PER_GROUP).T.reshape(-1)
    rw_t = router_w.astype(F32).T[perm]
    rb = router_b.astype(F32)[perm].reshape(ne, 1)
    return pl.pallas_call(
        _router_kernel,
        grid=(t // tm,),
        in_specs=[
            pl.BlockSpec((tm, d), lambda i: (i, 0)),
            pl.BlockSpec((1, d), lambda i: (0, 0)),
            pl.BlockSpec((ne, d), lambda i: (0, 0)),
            pl.BlockSpec((ne, 1), lambda i: (0, 0)),
        ],
        out_specs=[pl.BlockSpec((1, tm), lambda i: (0, i)),
                   pl.BlockSpec((tm * MOE_REC, LANES), lambda i: (i, 0))],
        out_shape=[jax.ShapeDtypeStruct((1, t), jnp.int32),
                   jax.ShapeDtypeStruct((t * MOE_REC, LANES), F32)],
        compiler_params=_params("parallel"),
        name="router",
    )(x2d, norm_g.reshape(1, d), rw_t, rb)


def moe_plan(grp, tm):
    ne = N_GROUPS
    e_flat = grp.reshape(-1)
    n_tiles = -(-e_flat.shape[0] // tm) + ne + 1
    onehot = (e_flat[:, None] == jnp.arange(ne, dtype=jnp.int32)[None, :]).astype(jnp.int32)
    csum = jnp.cumsum(onehot, axis=0)
    counts = csum[-1]
    padded = (counts + tm - 1) // tm * tm
    ends = jnp.cumsum(padded)
    starts = ends - padded
    pos = jnp.sum(onehot * (csum - 1 + starts[None, :]), axis=1)
    tile_row0 = jnp.arange(n_tiles, dtype=jnp.int32) * tm
    tile_expert = jnp.minimum(jnp.sum((tile_row0[:, None] >= ends[None, :]).astype(jnp.int32), axis=1), ne - 1)
    n_used = (ends[-1] // tm).reshape(1)
    return pos.astype(jnp.int32), tile_expert.astype(jnp.int32), n_used.astype(jnp.int32), n_tiles


def _moe_group_kernel(pos_ref, tg_ref, nused_ref, xrows_hbm, wg_ref, wu_ref, wd_ref, ys_ref,
                      src_ref, xbuf, sem, xn_bf, gate_ref):
    i = pl.program_id(0)
    j = pl.program_id(1)
    n_slots = pl.num_programs(1)
    tm, d = xn_bf.shape
    n_sub = d // LANES
    n_rec = xbuf.shape[1] // tm
    n_used = nused_ref[0]
    n_tok = pos_ref.shape[0]
    first = j == 0

    def row_copy(tok, slot, r):
        return pltpu.make_async_copy(xrows_hbm.at[pl.ds(pl.multiple_of(tok * n_rec, SUBLANES), n_rec), :],
                                     xbuf.at[slot, pl.ds(pl.multiple_of(r * n_rec, SUBLANES), n_rec), :],
                                     sem.at[slot])

    def wait(slot):
        pltpu.make_async_copy(xrows_hbm.at[pl.ds(0, tm * n_rec), :], xbuf.at[slot], sem.at[slot]).wait()

    @pl.when((i == 0) & first)
    def _():
        def clear(r, _):
            src_ref[r] = 0
            return 0
        lax.fori_loop(0, src_ref.shape[0], clear, 0, unroll=DMA_UNROLL)

        def scatter(n, _):
            src_ref[pos_ref[n]] = n
            return 0
        lax.fori_loop(0, n_tok, scatter, 0, unroll=DMA_UNROLL)

        def head(r, _):
            row_copy(src_ref[r], 0, r).start()
            return 0
        lax.fori_loop(0, tm, head, 0, unroll=DMA_UNROLL)

    slot = i % 2

    @pl.when(first & (i <= n_used))
    def _():
        wait(slot)

    @pl.when(i < n_used)
    def _():
        @pl.when(first)
        def _():
            for a in range(n_sub):
                xn_bf[:, a * LANES:(a + 1) * LANES] = xbuf[slot, pl.ds(a, tm, stride=n_rec), :].astype(BF16)
            gate_ref[...] = xbuf[slot, pl.ds(n_sub, tm, stride=n_rec), :]

        share = tm // MOE_SLOTS
        parts = 4
        next_base = (i + 1) * tm + j * share

        def issue(q):
            for r in range(share * q // parts, share * (q + 1) // parts):
                row_copy(src_ref[next_base + r], 1 - slot, j * share + r).start()

        xn = xn_bf[...]
        issue(0)
        hg = jnp.dot(xn, wg_ref[0, 0].astype(BF16), preferred_element_type=F32)
        issue(1)
        hu = jnp.dot(xn, wu_ref[0, 0].astype(BF16), preferred_element_type=F32)
        issue(2)
        gates = gate_ref[...]
        lane = lax.broadcasted_iota(jnp.int32, gates.shape, 1)
        gate = jnp.sum(jnp.where(lane == j, gates, 0.0), axis=1, keepdims=True)
        a = hg * jax.nn.sigmoid(hg) * hu * gate
        issue(3)
        y = jnp.dot(a.astype(BF16), wd_ref[0, 0].astype(BF16), preferred_element_type=F32)

        @pl.when(first)
        def _():
            ys_ref[...] = y

        @pl.when(j > 0)
        def _():
            ys_ref[...] += y

    @pl.when(first & (i >= n_used))
    def _():
        ys_ref[...] = jnp.zeros_like(ys_ref)


def moe_experts(xrows, pos, tile_group, n_used, n_tiles, layer, w_gate, w_up, w_down, *, tm):
    d, f = w_gate.shape[-2:]
    ns = MOE_SLOTS
    wmap = lambda i, j, pos, tg, nu: (layer, tg[i] * ns + j, 0, 0)
    grid_spec = pltpu.PrefetchScalarGridSpec(
        num_scalar_prefetch=3,
        grid=(n_tiles, ns),
        in_specs=[
            pl.BlockSpec(memory_space=pl.ANY),
            pl.BlockSpec((1, 1, d, f), wmap),
            pl.BlockSpec((1, 1, d, f), wmap),
            pl.BlockSpec((1, 1, f, d), wmap),
        ],
        out_specs=pl.BlockSpec((tm, d), lambda i, j, pos, tg, nu: (i, 0)),
        scratch_shapes=[
            pltpu.SMEM((n_tiles * tm,), jnp.int32),
            pltpu.VMEM((2, tm * MOE_REC, LANES), F32),
            pltpu.SemaphoreType.DMA((2,)),
            pltpu.VMEM((tm, d), BF16),
            pltpu.VMEM((tm, LANES), F32),
        ],
    )
    return pl.pallas_call(
        _moe_group_kernel,
        grid_spec=grid_spec,
        out_shape=jax.ShapeDtypeStruct((n_tiles * tm, d), F32),
        compiler_params=pltpu.CompilerParams(dimension_semantics=("arbitrary", "arbitrary"),
                                             vmem_limit_bytes=MOE_VMEM_LIMIT_BYTES),
        name="moe_experts",
    )(pos, tile_group, n_used, xrows, w_gate, w_up, w_down)


def _moe_combine_kernel(pos_ref, x_ref, ys_hbm, o_ref, buf, sem):
    i = pl.program_id(0)
    tc = x_ref.shape[0]

    def row_copy(p, slot, r):
        return pltpu.make_async_copy(ys_hbm.at[pl.ds(p, 1), :], buf.at[slot, pl.ds(r, 1), :], sem.at[slot])

    def issue(tile, slot):
        def body(r, _):
            row_copy(pos_ref[tile * tc + r], slot, r).start()
            return 0
        lax.fori_loop(0, tc, body, 0, unroll=DMA_UNROLL)

    def wait(slot):
        pltpu.make_async_copy(ys_hbm.at[pl.ds(0, tc), :], buf.at[slot], sem.at[slot]).wait()

    @pl.when(i == 0)
    def _():
        issue(0, 0)

    slot = i % 2
    wait(slot)

    @pl.when(i + 1 < pl.num_programs(0))
    def _():
        issue(i + 1, 1 - slot)

    o_ref[...] = x_ref[...] + buf[slot]


def moe_combine(x2d, pos, ys, *, tc):
    t, d = x2d.shape
    grid_spec = pltpu.PrefetchScalarGridSpec(
        num_scalar_prefetch=1,
        grid=(t // tc,),
        in_specs=[
            pl.BlockSpec((tc, d), lambda i, pos: (i, 0)),
            pl.BlockSpec(memory_space=pl.ANY),
        ],
        out_specs=pl.BlockSpec((tc, d), lambda i, pos: (i, 0)),
        scratch_shapes=[pltpu.VMEM((2, tc, d), F32), pltpu.SemaphoreType.DMA((2,))],
    )
    return pl.pallas_call(
        _moe_combine_kernel,
        grid_spec=grid_spec,
        out_shape=jax.ShapeDtypeStruct((t, d), F32),
        compiler_params=_params("arbitrary"),
        name="moe_combine",
    )(pos, x2d, ys)


def moe_ffn(x2d, norm_g, router_w, router_b, layer, w_gate, w_up, w_down):
    grp, xrows = router_topk(x2d, norm_g, router_w, router_b, tm=512)
    pos, tile_group, n_used, n_tiles = moe_plan(grp, MOE_TM)
    ys = moe_experts(xrows, pos, tile_group, n_used, n_tiles, layer, w_gate, w_up, w_down, tm=MOE_TM)
    return moe_combine(x2d, pos, ys, tc=MOE_TC)


def kernel(x, norm_mix, norm_ffn, a_w_in, a_conv_w, a_gate_b, a_out_norm, a_w_out, b_w_in, b_kv_norm, b_w_ukv, b_q_norm, b_k_norm, b_w_out, c_w_in, c_q_norm, c_k_norm, c_w_out, router_w, router_b, moe_w_gate, moe_w_up, moe_w_down):
    bsz, s, d = x.shape
    depth = norm_mix.shape[0]
    x2d = x.reshape(bsz * s, d)
    for i in range(depth):
        kind = i % 3
        j = i // 3
        if kind == 0:
            x2d = mlstm_mixer(x2d, norm_mix[i], a_w_in[j], a_conv_w[j], a_gate_b[j], a_out_norm[j], a_w_out[j],
                              bsz, s)
        elif kind == 1:
            x2d = mla_mixer(x2d, norm_mix[i], b_w_in[j], b_kv_norm[j], b_w_ukv[j], b_q_norm[j], b_k_norm[j],
                            b_w_out[j], bsz, s)
        else:
            x2d = dilated_mixer(x2d, norm_mix[i], c_w_in[j], c_q_norm[j], c_k_norm[j], c_w_out[j], bsz, s)
        x2d = moe_ffn(x2d, norm_ffn[i], router_w, router_b, i, moe_w_gate, moe_w_up, moe_w_down)
    return x2d.reshape(bsz, s, d)
```
